```python
import math
import jax
import jax.numpy as jnp
from jax import lax
import numpy as np

D_MODEL = 2048
BATCH = 1
SEQ = 8192
DEPTH = 2

GRID_W = 64
N_MEM = 256
EPS = 1e-6

SSM_GROUP = 16
SSM_STATE = 64
SSM_GROUPS = 48
SSM_WIDTH = SSM_GROUPS * SSM_GROUP

DN_HEADS = 6
DN_HEAD_DIM = 128
DN_WIDTH = DN_HEADS * DN_HEAD_DIM
DN_CONV = 5
DN_CHUNK = 64

ATT_HEADS = 8
ATT_KV_HEADS = 2
ATT_HEAD_DIM = 128
ATT_WIDTH = ATT_HEADS * ATT_HEAD_DIM
ATT_KV_WIDTH = ATT_KV_HEADS * ATT_HEAD_DIM
ATT_BLOCK = 128
ROPE_THETA = 10000.0

MEM_HEADS = 4
MEM_HEAD_DIM = 128
MEM_WIDTH = MEM_HEADS * MEM_HEAD_DIM

N_BRANCH = 4
BRANCH_WIDTHS = (SSM_WIDTH, DN_WIDTH, ATT_WIDTH, MEM_WIDTH)
BRANCH_OFFSETS = (0, SSM_WIDTH, SSM_WIDTH + DN_WIDTH, SSM_WIDTH + DN_WIDTH + ATT_WIDTH)
BRANCH_TOTAL = SSM_WIDTH + DN_WIDTH + ATT_WIDTH + MEM_WIDTH

IN_SPLITS = (
    SSM_WIDTH, SSM_WIDTH,
    DN_WIDTH, DN_WIDTH, DN_WIDTH, 2 * DN_HEADS, 2 * DN_HEADS, DN_WIDTH,
    ATT_WIDTH, ATT_KV_WIDTH, ATT_KV_WIDTH, ATT_WIDTH,
    MEM_WIDTH, MEM_WIDTH,
    N_BRANCH * D_MODEL,
)
IN_WIDTH = (2 * SSM_WIDTH + 4 * DN_WIDTH + 4 * DN_HEADS + 2 * ATT_WIDTH
            + 2 * ATT_KV_WIDTH + 2 * MEM_WIDTH + N_BRANCH * D_MODEL)

kernel_name = "hybrid_gated_s5_deltanet_gridattn_encoder"


def rmsnorm(x, g):
    xf = x.astype(jnp.float32)
    y = xf * lax.rsqrt(jnp.mean(xf * xf, axis=-1, keepdims=True) + EPS)
    return (y * g.astype(jnp.float32)).astype(x.dtype)


def l2norm(x):
    return x * lax.rsqrt(jnp.sum(x * x, axis=-1, keepdims=True) + EPS)


def _cmul(ar, ai, br, bi):
    return ar * br - ai * bi, ar * bi + ai * br


def _ssm_combine(e1, e2):
    a1r, a1i, b1r, b1i = e1
    a2r, a2i, b2r, b2i = e2
    ar, ai = _cmul(a2r, a2i, a1r, a1i)
    br, bi = _cmul(a2r, a2i, b1r, b1i)
    return ar, ai, br + b2r, bi + b2i


def s5_direction(u, a_re, a_im, log_step, b_re, b_im, c_re, c_im, reverse):
    step = jnp.exp(log_step)[:, None]
    mag = jnp.exp(a_re * step)
    lam_re = mag * jnp.cos(a_im * step)
    lam_im = mag * jnp.sin(a_im * step)
    den = a_re * a_re + a_im * a_im
    nr = lam_re - 1.0
    ni = lam_im
    coef_re = (nr * a_re + ni * a_im) / den
    coef_im = (ni * a_re - nr * a_im) / den
    bb_re = coef_re[..., None] * b_re - coef_im[..., None] * b_im
    bb_im = coef_re[..., None] * b_im + coef_im[..., None] * b_re
    bu_re = jnp.einsum("blgp,gnp->blgn", u, bb_re)
    bu_im = jnp.einsum("blgp,gnp->blgn", u, bb_im)
    lr = jnp.broadcast_to(lam_re, bu_re.shape)
    li = jnp.broadcast_to(lam_im, bu_re.shape)
    _, _, s_re, s_im = lax.associative_scan(
        _ssm_combine, (lr, li, bu_re, bu_im), reverse=reverse, axis=1)
    return (jnp.einsum("blgn,gpn->blgp", s_re, c_re)
            - jnp.einsum("blgn,gpn->blgp", s_im, c_im))


def s5_mixer(u, a_re, a_im, log_step, b_re, b_im, c_re, c_im, d, w_glu, b_glu):
    dtype = u.dtype
    bsz, seq, _ = u.shape
    f = lambda t: t.astype(jnp.float32)
    ug = f(u).reshape(bsz, seq, SSM_GROUPS, SSM_GROUP)
    y = s5_direction(ug, f(a_re[0]), f(a_im[0]), f(log_step[0]), f(b_re[0]), f(b_im[0]),
                     f(c_re[0]), f(c_im[0]), reverse=False)
    y = y + s5_direction(ug, f(a_re[1]), f(a_im[1]), f(log_step[1]), f(b_re[1]), f(b_im[1]),
                         f(c_re[1]), f(c_im[1]), reverse=True)
    y = y + f(d).reshape(SSM_GROUPS, SSM_GROUP) * ug
    y = jax.nn.gelu(y.reshape(bsz, seq, SSM_WIDTH))
    y = y * jax.nn.sigmoid(y @ f(w_glu) + f(b_glu))
    return y.astype(dtype)


def short_conv(x, w):
    ch = x.shape[-1]
    rhs = jnp.transpose(w)[:, None, :].astype(x.dtype)
    return lax.conv_general_dilated(
        x, rhs, window_strides=(1,), padding=[(DN_CONV // 2, DN_CONV // 2)],
        dimension_numbers=("NWC", "WIO", "NWC"), feature_group_count=ch)


def gated_delta_rule(q, k, v, beta, g):
    b, h, l, dk = q.shape
    dv = v.shape[-1]
    c = DN_CHUNK
    n = l // c
    q = q.reshape(b, h, n, c, dk)
    k = k.reshape(b, h, n, c, dk)
    v = v.reshape(b, h, n, c, dv)
    beta = beta.reshape(b, h, n, c)
    g = jnp.cumsum(g.reshape(b, h, n, c), axis=-1)
    idx = jnp.arange(c)
    incl = idx[:, None] >= idx[None, :]
    strict = idx[:, None] > idx[None, :]
    diff = g[..., :, None] - g[..., None, :]
    decay = jnp.where(incl, jnp.exp(jnp.where(incl, diff, 0.0)), 0.0)
    k_beta = k * beta[..., None]
    lower = jnp.where(strict, jnp.einsum("bhncd,bhnsd->bhncs", k_beta, k) * decay, 0.0)
    eye = jnp.eye(c, dtype=q.dtype)
    rhs = jnp.concatenate([v * beta[..., None], k_beta * jnp.exp(g)[..., None]], axis=-1)
    sol = lax.linalg.triangular_solve(eye + lower, rhs, left_side=True, lower=True,
                                      unit_diagonal=True)
    u_c = sol[..., :dv]
    w_c = sol[..., dv:]
    intra = jnp.einsum("bhncd,bhnsd->bhncs", q, k) * decay
    q_dec = q * jnp.exp(g)[..., None]
    k_dec = k * jnp.exp(g[..., -1:] - g)[..., None]
    g_last = jnp.exp(g[..., -1])
    xs = (jnp.moveaxis(u_c, 2, 0), jnp.moveaxis(w_c, 2, 0), jnp.moveaxis(q_dec, 2, 0),
          jnp.moveaxis(k_dec, 2, 0), jnp.moveaxis(intra, 2, 0), jnp.moveaxis(g_last, 2, 0))

    def step(state, inp):
        u_i, w_i, qd_i, kd_i, a_i, gl_i = inp
        v_new = u_i - jnp.einsum("bhck,bhkv->bhcv", w_i, state)
        o = (jnp.einsum("bhck,bhkv->bhcv", qd_i, state)
             + jnp.einsum("bhcs,bhsv->bhcv", a_i, v_new))
        state = state * gl_i[..., None, None] + jnp.einsum("bhck,bhcv->bhkv", kd_i, v_new)
        return state, o

    s0 = jnp.zeros((b, h, dk, dv), q.dtype)
    _, o = lax.scan(step, s0, xs)
    return jnp.moveaxis(o, 0, 2).reshape(b, h, l, dv)


def deltanet_mixer(q, k, v, a_logit, b_logit, conv_w, a_log, dt_bias, norm_g):
    dtype = q.dtype
    bsz, seq, _ = q.shape
    qkv = jax.nn.silu(short_conv(jnp.concatenate([q, k, v], axis=-1), conv_w))
    qkv = qkv.astype(jnp.float32).reshape(bsz, seq, 3, DN_HEADS, DN_HEAD_DIM)
    qh = jnp.transpose(l2norm(qkv[:, :, 0]) * (DN_HEAD_DIM ** -0.5), (0, 2, 1, 3))
    kh = jnp.transpose(l2norm(qkv[:, :, 1]), (0, 2, 1, 3))
    vh = jnp.transpose(qkv[:, :, 2], (0, 2, 1, 3))
    a4 = a_logit.astype(jnp.float32).reshape(bsz, seq, 2, DN_HEADS)
    b4 = b_logit.astype(jnp.float32).reshape(bsz, seq, 2, DN_HEADS)
    beta = jax.nn.sigmoid(b4)
    g = -jnp.exp(a_log.astype(jnp.float32)) * jax.nn.softplus(a4 + dt_bias.astype(jnp.float32))
    beta_f = jnp.transpose(beta[:, :, 0], (0, 2, 1))
    beta_b = jnp.transpose(beta[:, :, 1], (0, 2, 1))
    g_f = jnp.transpose(g[:, :, 0], (0, 2, 1))
    g_b = jnp.transpose(g[:, :, 1], (0, 2, 1))
    o_f = gated_delta_rule(qh, kh, vh, beta_f, g_f)
    o_b = jnp.flip(gated_delta_rule(jnp.flip(qh, 2), jnp.flip(kh, 2), jnp.flip(vh, 2),
                                    jnp.flip(beta_b, 2), jnp.flip(g_b, 2)), 2)
    o = jnp.transpose(o_f + o_b, (0, 2, 1, 3))
    o = rmsnorm(o, norm_g)
    return o.reshape(bsz, seq, DN_WIDTH).astype(dtype)


def axial_rope(rows):
    row = jnp.repeat(jnp.arange(rows), GRID_W).astype(jnp.float32)
    col = jnp.tile(jnp.arange(GRID_W), rows).astype(jnp.float32)
    axis_dim = ATT_HEAD_DIM // 2
    freqs = ROPE_THETA ** (-jnp.arange(0, axis_dim, 2, dtype=jnp.float32) / axis_dim)
    ang = jnp.concatenate([row[:, None] * freqs, col[:, None] * freqs], axis=-1)
    return jnp.cos(ang), jnp.sin(ang)


def apply_rope(x, cos, sin):
    xp = x.reshape(x.shape[:-1] + (x.shape[-1] // 2, 2))
    x0, x1 = xp[..., 0], xp[..., 1]
    c = cos[None, :, None, :]
    s = sin[None, :, None, :]
    return jnp.stack([x0 * c - x1 * s, x0 * s + x1 * c], axis=-1).reshape(x.shape)


def grid_attention(q, k, v, qn_g, kn_g, cos, sin):
    dtype = q.dtype
    bsz, seq, _ = q.shape
    grp = ATT_HEADS // ATT_KV_HEADS
    qh = rmsnorm(q.reshape(bsz, seq, ATT_HEADS, ATT_HEAD_DIM), qn_g).astype(jnp.float32)
    kh = rmsnorm(k.reshape(bsz, seq, ATT_KV_HEADS, ATT_HEAD_DIM), kn_g).astype(jnp.float32)
    vh = v.reshape(bsz, seq, ATT_KV_HEADS, ATT_HEAD_DIM).astype(jnp.float32)
    qh = apply_rope(qh, cos, sin) * (ATT_HEAD_DIM ** -0.5)
    kh = apply_rope(kh, cos, sin)
    nblk = seq // ATT_BLOCK
    qb = qh.reshape(bsz, nblk, ATT_BLOCK, ATT_KV_HEADS, grp, ATT_HEAD_DIM)
    qb = jnp.transpose(qb, (1, 0, 2, 3, 4, 5))

    def block(qi):
        s = jnp.einsum("bqhgd,bkhd->bhgqk", qi, kh)
        p = jax.nn.softmax(s, axis=-1)
        return jnp.einsum("bhgqk,bkhd->bqhgd", p, vh)

    o = lax.map(block, qb)
    o = jnp.transpose(o, (1, 0, 2, 3, 4, 5)).reshape(bsz, seq, ATT_WIDTH)
    return o.astype(dtype)


def memory_attention(q, mem_n, w_kv):
    dtype = q.dtype
    bsz, seq, _ = q.shape
    kv = mem_n @ w_kv
    km = kv[..., :MEM_WIDTH].reshape(bsz, -1, MEM_HEADS, MEM_HEAD_DIM).astype(jnp.float32)
    vm = kv[..., MEM_WIDTH:].reshape(bsz, -1, MEM_HEADS, MEM_HEAD_DIM).astype(jnp.float32)
    qh = q.reshape(bsz, seq, MEM_HEADS, MEM_HEAD_DIM).astype(jnp.float32)
    s = jnp.einsum("bqhd,bkhd->bhqk", qh, km) * (MEM_HEAD_DIM ** -0.5)
    p = jax.nn.softmax(s, axis=-1)
    o = jnp.einsum("bhqk,bkhd->bqhd", p, vm).reshape(bsz, seq, MEM_WIDTH)
    return o.astype(dtype)


def setup_inputs(seed: int = 0) -> dict:
    key = jax.random.key(seed)
    ks = jax.random.split(key, 32)
    f32 = jnp.float32

    def nrm(k, shape, scale):
        return jax.random.normal(k, shape, f32) * scale

    x = nrm(ks[0], (BATCH, SEQ, D_MODEL), 1.0)
    mem = nrm(ks[1], (BATCH, N_MEM, D_MODEL), 1.0)
    norm_g = 1.0 + nrm(ks[2], (DEPTH, D_MODEL), 0.02)
    w_in = nrm(ks[3], (DEPTH, D_MODEL, IN_WIDTH), D_MODEL ** -0.5)
    ssm_shape = (DEPTH, 2, SSM_GROUPS, SSM_STATE)
    ssm_a_re = -0.5 + nrm(ks[4], ssm_shape, 0.01)
    ssm_a_im = jnp.pi * jnp.arange(SSM_STATE, dtype=f32) + nrm(ks[5], ssm_shape, 0.01)
    ssm_log_step = jax.random.uniform(ks[6], (DEPTH, 2, SSM_GROUPS), f32,
                                      math.log(1e-3), math.log(1e-1))
    ssm_b_re = nrm(ks[7], (DEPTH, 2, SSM_GROUPS, SSM_STATE, SSM_GROUP), (2 * SSM_GROUP) ** -0.5)
    ssm_b_im = nrm(ks[8], (DEPTH, 2, SSM_GROUPS, SSM_STATE, SSM_GROUP), (2 * SSM_GROUP) ** -0.5)
    ssm_c_re = nrm(ks[9], (DEPTH, 2, SSM_GROUPS, SSM_GROUP, SSM_STATE), SSM_STATE ** -0.5)
    ssm_c_im = nrm(ks[10], (DEPTH, 2, SSM_GROUPS, SSM_GROUP, SSM_STATE), SSM_STATE ** -0.5)
    ssm_d = nrm(ks[11], (DEPTH, SSM_WIDTH), 1.0)
    ssm_w_glu = nrm(ks[12], (DEPTH, SSM_WIDTH, SSM_WIDTH), SSM_WIDTH ** -0.5)
    ssm_b_glu = nrm(ks[13], (DEPTH, SSM_WIDTH), 0.02)
    dn_conv = nrm(ks[14], (DEPTH, 3 * DN_WIDTH, DN_CONV), DN_CONV ** -0.5)
    dn_a_log = jnp.log(jax.random.uniform(ks[15], (DEPTH, 2, DN_HEADS), f32, 1.0, 16.0))
    dt = jnp.exp(jax.random.uniform(ks[16], (DEPTH, 2, DN_HEADS), f32,
                                    math.log(1e-3), math.log(1e-1)))
    dn_dt_bias = dt + jnp.log(-jnp.expm1(-dt))
    dn_norm_g = 1.0 + nrm(ks[17], (DEPTH, DN_HEAD_DIM), 0.02)
    attn_q_norm = 1.0 + nrm(ks[18], (DEPTH, ATT_HEAD_DIM), 0.02)
    attn_k_norm = 1.0 + nrm(ks[19], (DEPTH, ATT_HEAD_DIM), 0.02)
    mem_norm_g = 1.0 + nrm(ks[20], (DEPTH, D_MODEL), 0.02)
    w_mem_kv = nrm(ks[21], (DEPTH, D_MODEL, 2 * MEM_WIDTH), D_MODEL ** -0.5)
    bks = jax.random.split(ks[22], N_BRANCH)
    w_branch = jnp.concatenate(
        [nrm(bks[i], (DEPTH, BRANCH_WIDTHS[i], D_MODEL), BRANCH_WIDTHS[i] ** -0.5)
         for i in range(N_BRANCH)], axis=1)
    w_out = nrm(ks[23], (DEPTH, D_MODEL, D_MODEL), D_MODEL ** -0.5)
    final_norm_g = 1.0 + nrm(ks[24], (D_MODEL,), 0.02)
    return {
        "x": x, "mem": mem, "norm_g": norm_g, "w_in": w_in,
        "ssm_a_re": ssm_a_re, "ssm_a_im": ssm_a_im, "ssm_log_step": ssm_log_step,
        "ssm_b_re": ssm_b_re, "ssm_b_im": ssm_b_im, "ssm_c_re": ssm_c_re,
        "ssm_c_im": ssm_c_im, "ssm_d": ssm_d, "ssm_w_glu": ssm_w_glu,
        "ssm_b_glu": ssm_b_glu, "dn_conv": dn_conv, "dn_a_log": dn_a_log,
        "dn_dt_bias": dn_dt_bias, "dn_norm_g": dn_norm_g, "attn_q_norm": attn_q_norm,
        "attn_k_norm": attn_k_norm, "mem_norm_g": mem_norm_g, "w_mem_kv": w_mem_kv,
        "w_branch": w_branch, "w_out": w_out, "final_norm_g": final_norm_g,
    }


def reference(x, mem, norm_g, w_in, ssm_a_re, ssm_a_im, ssm_log_step, ssm_b_re, ssm_b_im,
              ssm_c_re, ssm_c_im, ssm_d, ssm_w_glu, ssm_b_glu, dn_conv, dn_a_log,
              dn_dt_bias, dn_norm_g, attn_q_norm, attn_k_norm, mem_norm_g, w_mem_kv,
              w_branch, w_out, final_norm_g):
    bsz, seq, _ = x.shape
    rows = seq // GRID_W
    cos, sin = axial_rope(rows)
    split_at = [int(i) for i in np.cumsum(IN_SPLITS)[:-1]]
    for layer in range(DEPTH):
        xn = rmsnorm(x, norm_g[layer])
        h = xn @ w_in[layer]
        (u_a, z_a, dq, dk, dv, da, db, z_b, aq, ak, av, z_c, mq, z_m,
         gate_logits) = jnp.split(h, split_at, axis=-1)

        y_a = s5_mixer(u_a, ssm_a_re[layer], ssm_a_im[layer], ssm_log_step[layer],
                       ssm_b_re[layer], ssm_b_im[layer], ssm_c_re[layer], ssm_c_im[layer],
                       ssm_d[layer], ssm_w_glu[layer], ssm_b_glu[layer]) * jax.nn.silu(z_a)
        y_b = deltanet_mixer(dq, dk, dv, da, db, dn_conv[layer], dn_a_log[layer],
                             dn_dt_bias[layer], dn_norm_g[layer]) * jax.nn.silu(z_b)
        y_c = grid_attention(aq, ak, av, attn_q_norm[layer], attn_k_norm[layer],
                             cos, sin) * jax.nn.silu(z_c)
        y_m = memory_attention(mq, rmsnorm(mem, mem_norm_g[layer]),
                               w_mem_kv[layer]) * jax.nn.silu(z_m)

        gates = jax.nn.sigmoid(gate_logits.reshape(bsz, seq, N_BRANCH, D_MODEL))
        merged = jnp.zeros_like(x)
        for bi, y_br in enumerate((y_a, y_b, y_c, y_m)):
            off = BRANCH_OFFSETS[bi]
            w_b = w_branch[layer, off:off + BRANCH_WIDTHS[bi]]
            merged = merged + gates[:, :, bi] * (y_br @ w_b)
        x = x + merged @ w_out[layer]
    return rmsnorm(x, final_norm_g)
```

```python
import functools
import math

import numpy as np
import jax
import jax.numpy as jnp
from jax import lax
from jax.experimental import pallas as pl
from jax.experimental.pallas import tpu as pltpu

F32 = jnp.float32
BF16 = jnp.bfloat16

D_MODEL = 2048
GRID_W = 64
EPS = 1e-6

SSM_GROUP = 16
SSM_STATE = 64
SSM_GROUPS = 48
SSM_WIDTH = SSM_GROUPS * SSM_GROUP
SSM_T = 16
SSM_PAIRS = SSM_GROUPS // 2
SSM_CW = SSM_T * SSM_GROUP

DN_HEADS = 6
DN_HEAD_DIM = 128
DN_WIDTH = DN_HEADS * DN_HEAD_DIM
DN_CONV = 5
DN_CHUNK = 64
DN_GROUP = 256

ATT_HEADS = 8
ATT_KV_HEADS = 2
ATT_HEAD_DIM = 128
ATT_WIDTH = ATT_HEADS * ATT_HEAD_DIM
ATT_KV_WIDTH = ATT_KV_HEADS * ATT_HEAD_DIM
ROPE_THETA = 10000.0

MEM_HEADS = 4
MEM_HEAD_DIM = 128
MEM_WIDTH = MEM_HEADS * MEM_HEAD_DIM

N_BRANCH = 4

OFF_UA = 0
OFF_ZA = 768
OFF_DQ = 1536
OFF_DK = 2304
OFF_DV = 3072
OFF_ZB = 3840
OFF_MQ = 4608
OFF_ZM = 5120
OFF_AK = 5632
OFF_AV = 5888
OFF_AQ = 6144
OFF_ZC = 7168
OFF_GATE = 8192
H_WIDTH = OFF_GATE + N_BRANCH * D_MODEL
AB_PAD = 128

VMEM_LIMIT = 56 * 1024 * 1024


def _cparams(sem):
    return pltpu.CompilerParams(dimension_semantics=sem, vmem_limit_bytes=VMEM_LIMIT)


def _silu(x):
    return x * jax.nn.sigmoid(x)


def _dot(a, b):
    return jnp.dot(a, b, preferred_element_type=F32)


def _dot_nt(a, b):
    return lax.dot_general(a, b, (((1,), (1,)), ((), ())), preferred_element_type=F32)


def _dot_tn(a, b):
    return lax.dot_general(a, b, (((0,), (0,)), ((), ())), preferred_element_type=F32)


def _split3(x):
    x1 = x.astype(BF16)
    r1 = x - x1.astype(F32)
    x2 = r1.astype(BF16)
    x3 = (r1 - x2.astype(F32)).astype(BF16)
    return x1, x2, x3


def _dot_exact_lhs(a_bf16, x):
    x1, x2, x3 = _split3(x)
    return _dot(a_bf16, x1) + _dot(a_bf16, x2) + _dot(a_bf16, x3)


def _dot_f32(a, b):
    a1, a2, a3 = _split3(a)
    b1, b2, b3 = _split3(b)
    return (_dot(a1, b1) + (_dot(a1, b2) + _dot(a2, b1))
            + (_dot(a1, b3) + _dot(a2, b2) + _dot(a3, b1)))


def _inproj_kernel(x_ref, g_ref, w_ref, wab_ref, h_ref, ab_ref, xn_ref):
    @pl.when(pl.program_id(1) == 0)
    def _():
        x = x_ref[...]
        ms = jnp.mean(x * x, axis=-1, keepdims=True)
        xn_ref[...] = (x * lax.rsqrt(ms + EPS) * g_ref[...]).astype(BF16)
        ab_ref[...] = _dot(xn_ref[...], wab_ref[...])

    h_ref[...] = _dot(xn_ref[...], w_ref[...]).astype(BF16)


def _inproj(x, g, w_main, w_ab):
    seq = x.shape[0]
    tm = min(1024, seq)
    tn = 1024
    return pl.pallas_call(
        _inproj_kernel,
        grid=(seq // tm, H_WIDTH // tn),
        in_specs=[
            pl.BlockSpec((tm, D_MODEL), lambda i, j: (i, 0)),
            pl.BlockSpec((1, D_MODEL), lambda i, j: (0, 0)),
            pl.BlockSpec((D_MODEL, tn), lambda i, j: (0, j)),
            pl.BlockSpec((D_MODEL, AB_PAD), lambda i, j: (0, 0)),
        ],
        out_specs=[
            pl.BlockSpec((tm, tn), lambda i, j: (i, j)),
            pl.BlockSpec((tm, AB_PAD), lambda i, j: (i, 0)),
        ],
        out_shape=[
            jax.ShapeDtypeStruct((seq, H_WIDTH), BF16),
            jax.ShapeDtypeStruct((seq, AB_PAD), F32),
        ],
        scratch_shapes=[pltpu.VMEM((tm, D_MODEL), BF16)],
        compiler_params=_cparams(("parallel", "arbitrary")),
        name="inproj",
    )(x, g, w_main, w_ab)


def _ssm_prep_kernel(are_r, aim_r, ls_r, are_c, aim_c, ls_c, bre_ref, bim_ref, cre_ref, cim_ref,
                     win_ref, wout_ref, m_ref):
    t = SSM_T
    cw = SSM_CW
    lane128 = lax.broadcasted_iota(jnp.int32, (1, 128), 1)
    row_s = (lax.broadcasted_iota(jnp.int32, (cw, 1), 0) // SSM_GROUP).astype(F32)
    col_t = (lax.broadcasted_iota(jnp.int32, (1, cw), 1) // SSM_GROUP).astype(F32)
    sub128 = lax.broadcasted_iota(jnp.int32, (128, 1), 0)
    lane_cw = lax.broadcasted_iota(jnp.int32, (SSM_GROUP, cw), 1)

    m_ref[...] = jnp.zeros(m_ref.shape, m_ref.dtype)

    krows = [[None, None], [None, None]]
    for d in range(2):
        step = jnp.exp(ls_r[d])
        a_re = are_r[d]
        a_im = aim_r[d]
        ar = a_re * step
        ai = a_im * step
        mag = jnp.exp(ar)
        lam_re = mag * jnp.cos(ai)
        lam_im = mag * jnp.sin(ai)
        den = a_re * a_re + a_im * a_im
        nr = lam_re - 1.0
        ni = lam_im
        coef_re = (nr * a_re + ni * a_im) / den
        coef_im = (ni * a_re - nr * a_im) / den
        b_re = bre_ref[d]
        b_im = bim_ref[d]
        bb_re = coef_re * b_re - coef_im * b_im
        bb_im = coef_re * b_im + coef_im * b_re
        e_in = (t - 1.0 - row_s) if d == 0 else row_s
        p_mag = jnp.exp(e_in * ar)
        p_re = p_mag * jnp.cos(e_in * ai)
        p_im = p_mag * jnp.sin(e_in * ai)
        w_re = p_re * bb_re - p_im * bb_im
        w_im = p_re * bb_im + p_im * bb_re
        for par in range(2):
            keep = (lane128 < 64) if par == 0 else (lane128 >= 64)
            win_ref[par * cw:(par + 1) * cw, (2 * d) * 128:(2 * d + 1) * 128] = (
                jnp.where(keep, w_re, 0.0).astype(BF16))
            win_ref[par * cw:(par + 1) * cw, (2 * d + 1) * 128:(2 * d + 2) * 128] = (
                jnp.where(keep, w_im, 0.0).astype(BF16))

        step_c = jnp.exp(ls_c[d])
        ar_c = are_c[d] * step_c
        ai_c = aim_c[d] * step_c
        c_re = cre_ref[d]
        c_im = cim_ref[d]

        def c_lam(e_row):
            q_mag = jnp.exp(e_row * ar_c)
            q_re = q_mag * jnp.cos(e_row * ai_c)
            q_im = q_mag * jnp.sin(e_row * ai_c)
            return c_re * q_re - c_im * q_im, -(c_re * q_im + c_im * q_re)

        e_out = (col_t + 1.0) if d == 0 else (t - col_t)
        o_re, o_im = c_lam(e_out)
        for par in range(2):
            keep = (sub128 < 64) if par == 0 else (sub128 >= 64)
            wout_ref[(2 * d) * 128:(2 * d + 1) * 128, par * cw:(par + 1) * cw] = (
                jnp.where(keep, o_re, 0.0).astype(BF16))
            wout_ref[(2 * d + 1) * 128:(2 * d + 2) * 128, par * cw:(par + 1) * cw] = (
                jnp.where(keep, o_im, 0.0).astype(BF16))

        e_k = col_t if d == 0 else (t - 1.0 - col_t)
        r_re, r_im = c_lam(e_k)
        bbt_re = bb_re[0:SSM_GROUP, :]
        bbt_im = bb_im[0:SSM_GROUP, :]
        for par in range(2):
            keep = (lane128 < 64) if par == 0 else (lane128 >= 64)
            krows[d][par] = (_dot_f32(jnp.where(keep, bbt_re, 0.0), r_re)
                             + _dot_f32(jnp.where(keep, bbt_im, 0.0), r_im))

    for par in range(2):
        kf = krows[0][par]
        kb = krows[1][par]
        for s in range(t):
            sh_f = SSM_GROUP * s
            blk = jnp.where(lane_cw >= sh_f, pltpu.roll(kf, sh_f, axis=1) if sh_f else kf, 0.0)
            sh_b = SSM_GROUP * (t - 1 - s)
            rolled_b = pltpu.roll(kb, cw - sh_b, axis=1) if sh_b else kb
            blk = blk + jnp.where(lane_cw < cw - sh_b, rolled_b, 0.0)
            m_ref[par * cw + s * SSM_GROUP:par * cw + (s + 1) * SSM_GROUP,
                  par * cw:(par + 1) * cw] = blk.astype(BF16)


def _ssm_prep(rows, cols, b_t, c_t):
    pw = 2 * SSM_CW
    row_spec = pl.BlockSpec((None, 2, 1, 128), lambda g: (g, 0, 0, 0))
    col_spec = pl.BlockSpec((None, 2, 128, 1), lambda g: (g, 0, 0, 0))
    b_spec = pl.BlockSpec((None, 2, SSM_CW, 128), lambda g: (g, 0, 0, 0))
    c_spec = pl.BlockSpec((None, 2, 128, SSM_CW), lambda g: (g, 0, 0, 0))
    w_spec = pl.BlockSpec((None, pw, pw), lambda g: (g, 0, 0))
    return pl.pallas_call(
        _ssm_prep_kernel,
        grid=(SSM_PAIRS,),
        in_specs=[row_spec] * 3 + [col_spec] * 3 + [b_spec] * 2 + [c_spec] * 2,
        out_specs=[w_spec] * 3,
        out_shape=[jax.ShapeDtypeStruct((SSM_PAIRS, pw, pw), BF16)] * 3,
        compiler_params=_cparams(("parallel",)),
        name="ssm_prep",
    )(*rows, *cols, *b_t, *c_t)


def _ssm_in_kernel(u_ref, win_ref, h_ref):
    h = _dot(u_ref[...], win_ref[...])
    for k in range(4):
        h_ref[k] = h[:, k * 128:(k + 1) * 128]


def _ssm_in(u2, win):
    nchunk = u2.shape[0]
    pw = 2 * SSM_CW
    return pl.pallas_call(
        _ssm_in_kernel,
        grid=(SSM_PAIRS,),
        in_specs=[
            pl.BlockSpec((nchunk, pw), lambda g: (0, g)),
            pl.BlockSpec((None, pw, pw), lambda g: (g, 0, 0)),
        ],
        out_specs=pl.BlockSpec((4, nchunk, 128), lambda g: (0, 0, g)),
        out_shape=jax.ShapeDtypeStruct((4, nchunk, SSM_PAIRS * 128), F32),
        compiler_params=_cparams(("parallel",)),
        name="ssm_in",
    )(u2, win)


def _ssm_scan_kernel(are_ref, aim_ref, ls_ref, h_ref, p_ref, *, nchunk):
    width = h_ref.shape[2]

    def lam_pow_t(d):
        step = jnp.exp(ls_ref[d]) * float(SSM_T)
        mag = jnp.exp(are_ref[d] * step)
        ang = aim_ref[d] * step
        return mag * jnp.cos(ang), mag * jnp.sin(ang)

    lfr, lfi = lam_pow_t(0)
    lbr, lbi = lam_pow_t(1)

    def body(c, carry):
        fr, fi, br, bi = carry
        cb = nchunk - 1 - c
        p_ref[0, pl.ds(c, 1), :] = fr
        p_ref[1, pl.ds(c, 1), :] = fi
        p_ref[2, pl.ds(cb, 1), :] = br
        p_ref[3, pl.ds(cb, 1), :] = bi
        hfr = h_ref[0, pl.ds(c, 1), :]
        hfi = h_ref[1, pl.ds(c, 1), :]
        hbr = h_ref[2, pl.ds(cb, 1), :]
        hbi = h_ref[3, pl.ds(cb, 1), :]
        nfr = lfr * fr - lfi * fi + hfr
        nfi = lfr * fi + lfi * fr + hfi
        nbr = lbr * br - lbi * bi + hbr
        nbi = lbr * bi + lbi * br + hbi
        return nfr, nfi, nbr, nbi

    z = jnp.zeros((1, width), F32)
    lax.fori_loop(0, nchunk, body, (z, z, z, z))


def _ssm_scan(flat_params, h):
    nchunk = h.shape[1]
    width = h.shape[2]
    wt = 768
    return pl.pallas_call(
        functools.partial(_ssm_scan_kernel, nchunk=nchunk),
        grid=(width // wt,),
        in_specs=[pl.BlockSpec((2, 1, wt), lambda j: (0, 0, j))] * 3 + [
            pl.BlockSpec((4, nchunk, wt), lambda j: (0, 0, j)),
        ],
        out_specs=pl.BlockSpec((4, nchunk, wt), lambda j: (0, 0, j)),
        out_shape=jax.ShapeDtypeStruct(h.shape, F32),
        compiler_params=_cparams(("parallel",)),
        name="ssm_scan",
    )(*flat_params, h)


def _ssm_out_kernel(u_ref, p_ref, m_ref, wout_ref, y_ref):
    acc = _dot(u_ref[...], m_ref[...])
    for k in range(4):
        acc = acc + _dot(p_ref[k].astype(BF16), wout_ref[k * 128:(k + 1) * 128, :])
    y_ref[...] = acc


def _ssm_out(u2, p, m, wout):
    nchunk = u2.shape[0]
    pw = 2 * SSM_CW
    return pl.pallas_call(
        _ssm_out_kernel,
        grid=(SSM_PAIRS,),
        in_specs=[
            pl.BlockSpec((nchunk, pw), lambda g: (0, g)),
            pl.BlockSpec((4, nchunk, 128), lambda g: (0, 0, g)),
            pl.BlockSpec((None, pw, pw), lambda g: (g, 0, 0)),
            pl.BlockSpec((None, pw, pw), lambda g: (g, 0, 0)),
        ],
        out_specs=pl.BlockSpec((nchunk, pw), lambda g: (0, g)),
        out_shape=jax.ShapeDtypeStruct((nchunk, SSM_PAIRS * pw), F32),
        compiler_params=_cparams(("parallel",)),
        name="ssm_out",
    )(u2, p, m, wout)


def _ssm_epi_kernel(y_ref, u_ref, z_ref, d_ref, wg_ref, bg_ref, o_ref):
    y = y_ref[...] + d_ref[...] * u_ref[...].astype(F32)
    y = jax.nn.gelu(y)
    glu = _dot(y.astype(BF16), wg_ref[...].astype(BF16)) + bg_ref[...]
    y = y * jax.nn.sigmoid(glu)
    o_ref[...] = (y * _silu(z_ref[...].astype(F32))).astype(BF16)


def _ssm_epi(y, h, d, w_glu, b_glu):
    seq = y.shape[0]
    tm = min(512, seq)
    w = SSM_WIDTH
    return pl.pallas_call(
        _ssm_epi_kernel,
        grid=(seq // tm,),
        in_specs=[
            pl.BlockSpec((tm, w), lambda i: (i, 0)),
            pl.BlockSpec((tm, w), lambda i: (i, OFF_UA // w)),
            pl.BlockSpec((tm, w), lambda i: (i, OFF_ZA // w)),
            pl.BlockSpec((1, w), lambda i: (0, 0)),
            pl.BlockSpec((w, w), lambda i: (0, 0)),
            pl.BlockSpec((1, w), lambda i: (0, 0)),
        ],
        out_specs=pl.BlockSpec((tm, w), lambda i: (i, 0)),
        out_shape=jax.ShapeDtypeStruct((seq, w), BF16),
        compiler_params=_cparams(("parallel",)),
        name="ssm_epi",
    )(y, h, h, d, w_glu, b_glu)


def _ssm_branch(h, lp):
    seq = h.shape[0]
    nchunk = seq // SSM_T
    win, wout, m = _ssm_prep(lp["ssm_rows"], lp["ssm_cols"], lp["ssm_b"], lp["ssm_c"])
    u = h[:, OFF_UA:OFF_UA + SSM_WIDTH]
    u2 = u.reshape(nchunk, SSM_T, SSM_GROUPS, SSM_GROUP).transpose(0, 2, 1, 3)
    u2 = u2.reshape(nchunk, SSM_GROUPS * SSM_CW)
    hs = _ssm_in(u2, win)
    p = _ssm_scan(lp["ssm_flat"], hs)
    y2 = _ssm_out(u2, p, m, wout)
    y = y2.reshape(nchunk, SSM_GROUPS, SSM_T, SSM_GROUP).transpose(0, 2, 1, 3)
    y = y.reshape(seq, SSM_WIDTH)
    return _ssm_epi(y, h, lp["ssm_d"], lp["ssm_w_glu"], lp["ssm_b_glu"])


def _dn_prep_kernel(qc, qp, qn, kc, kp, kn, vc, vp, vn, ab_ref, cw_ref, alog_ref, dtb_ref,
                    qo_ref, ko_ref, vo_ref, sc_ref, ext_ref, *, tm, nblk):
    i = pl.program_id(0)
    halo = 16
    pad = DN_CONV // 2

    def conv_silu(cur, prev, nxt, part):
        ext_ref[0:halo, :] = jnp.where(i > 0, prev[...].astype(F32), 0.0)
        ext_ref[halo:halo + tm, :] = cur[...].astype(F32)
        ext_ref[halo + tm:halo + tm + halo, :] = jnp.where(i < nblk - 1, nxt[...].astype(F32), 0.0)
        acc = jnp.zeros((tm, DN_WIDTH), F32)
        for j in range(DN_CONV):
            acc = acc + cw_ref[part, j:j + 1, :] * ext_ref[pl.ds(halo - pad + j, tm), :]
        return _silu(acc)

    def l2n(x):
        outs = []
        for hd in range(DN_HEADS):
            xh = x[:, hd * DN_HEAD_DIM:(hd + 1) * DN_HEAD_DIM]
            outs.append(xh * lax.rsqrt(jnp.sum(xh * xh, axis=-1, keepdims=True) + EPS))
        return jnp.concatenate(outs, axis=1)

    qo_ref[...] = (l2n(conv_silu(qc, qp, qn, 0)) * (DN_HEAD_DIM ** -0.5)).astype(BF16)
    ko_ref[...] = l2n(conv_silu(kc, kp, kn, 1)).astype(BF16)
    vo_ref[...] = conv_silu(vc, vp, vn, 2).astype(BF16)

    ab = ab_ref[...]
    g_all = -jnp.exp(alog_ref[...]) * jax.nn.softplus(ab + dtb_ref[...])
    beta_all = jax.nn.sigmoid(ab)
    r = lax.broadcasted_iota(jnp.int32, (tm, tm), 0)
    c = lax.broadcasted_iota(jnp.int32, (tm, tm), 1)
    same = (r // DN_CHUNK) == (c // DN_CHUNK)
    tri_f = jnp.where(same & (c <= r), 1.0, 0.0).astype(BF16)
    tri_b = jnp.where(same & (c >= r), 1.0, 0.0).astype(BF16)
    blk = jnp.where(same, 1.0, 0.0).astype(BF16)
    gcf = _dot_exact_lhs(tri_f, g_all)
    gcb = _dot_exact_lhs(tri_b, g_all)
    gtot = _dot_exact_lhs(blk, g_all)
    lane = lax.broadcasted_iota(jnp.int32, (tm, AB_PAD), 1)
    sc = jnp.where(lane < 6, gcf,
                   jnp.where(lane < 12, gcb,
                             jnp.where(lane < 24, beta_all,
                                       jnp.where(lane < 36, pltpu.roll(gtot, 24, axis=1), 0.0))))
    sc_ref[...] = sc


def _dn_prep(h, ab, conv_w, alog, dtb):
    seq = h.shape[0]
    tm = min(256, seq)
    nblk = seq // tm
    w = DN_WIDTH
    hb = tm // 16
    nh = seq // 16

    def cur(ci):
        return pl.BlockSpec((tm, w), lambda i: (i, ci))

    def prev(ci):
        return pl.BlockSpec((16, w), lambda i: (jnp.maximum(i * hb - 1, 0), ci))

    def nxt(ci):
        return pl.BlockSpec((16, w), lambda i: (jnp.minimum((i + 1) * hb, nh - 1), ci))

    in_specs = []
    for off in (OFF_DQ, OFF_DK, OFF_DV):
        ci = off // w
        in_specs += [cur(ci), prev(ci), nxt(ci)]
    in_specs += [
        pl.BlockSpec((tm, AB_PAD), lambda i: (i, 0)),
        pl.BlockSpec((3, 8, w), lambda i: (0, 0, 0)),
        pl.BlockSpec((1, AB_PAD), lambda i: (0, 0)),
        pl.BlockSpec((1, AB_PAD), lambda i: (0, 0)),
    ]
    return pl.pallas_call(
        functools.partial(_dn_prep_kernel, tm=tm, nblk=nblk),
        grid=(nblk,),
        in_specs=in_specs,
        out_specs=[pl.BlockSpec((tm, w), lambda i: (i, 0))] * 3
        + [pl.BlockSpec((tm, AB_PAD), lambda i: (i, 0))],
        out_shape=[jax.ShapeDtypeStruct((seq, w), BF16)] * 3
        + [jax.ShapeDtypeStruct((seq, AB_PAD), F32)],
        scratch_shapes=[pltpu.VMEM((tm + 32, w), F32)],
        compiler_params=_cparams(("parallel",)),
        name="dn_prep",
    )(h, h, h, h, h, h, h, h, h, ab, conv_w, alog, dtb)


def _dn_direction(d, hd, q_ref, k_ref, v_ref, sc_ref, sct_ref, s_ref, o_ref):
    gsz = DN_GROUP
    nck = gsz // DN_CHUNK
    q = q_ref[...]
    k = k_ref[...]
    qf = q.astype(F32)
    kf = k.astype(F32)
    vf = v_ref[...].astype(F32)
    sc = sc_ref[...]
    sct = sct_ref[...]
    lane = lax.broadcasted_iota(jnp.int32, sc.shape, 1)
    sub = lax.broadcasted_iota(jnp.int32, sct.shape, 0)

    def col(cidx):
        return jnp.sum(jnp.where(lane == cidx, sc, 0.0), axis=1, keepdims=True)

    def row(cidx):
        return jnp.sum(jnp.where(sub == cidx, sct, 0.0), axis=0, keepdims=True)

    gc_col = col(d * DN_HEADS + hd)
    gc_row = row(d * DN_HEADS + hd)
    beta_col = col(12 + d * DN_HEADS + hd)
    gtot_col = col(24 + d * DN_HEADS + hd)

    r = lax.broadcasted_iota(jnp.int32, (gsz, gsz), 0)
    c = lax.broadcasted_iota(jnp.int32, (gsz, gsz), 1)
    same = (r // DN_CHUNK) == (c // DN_CHUNK)
    if d == 0:
        incl = same & (r >= c)
        strict = same & (r > c)
    else:
        incl = same & (r <= c)
        strict = same & (r < c)
    decay = jnp.where(incl, jnp.exp(jnp.where(incl, gc_col - gc_row, 0.0)), 0.0)

    kb = kf * beta_col
    kb16 = kb.astype(BF16)
    nmat = jnp.where(strict, _dot_nt(kb16, k) * decay, 0.0)
    x = jnp.concatenate([vf * beta_col, kb * jnp.exp(gc_col)], axis=1)
    p = nmat.astype(BF16)
    x = x - _dot(p, x.astype(BF16))
    npow = 2
    while npow < DN_CHUNK:
        p = _dot(p, p).astype(BF16)
        x = x + _dot(p, x.astype(BF16))
        npow *= 2
    u = x[:, :DN_HEAD_DIM]
    w16 = x[:, DN_HEAD_DIM:].astype(BF16)
    intra = jnp.where(incl, _dot_nt(q, k) * decay, 0.0).astype(BF16)
    qd16 = (qf * jnp.exp(gc_col)).astype(BF16)
    kd16 = (kf * jnp.exp(gtot_col - gc_col)).astype(BF16)

    s = s_ref[d]
    vnew = [None] * nck
    oq = [None] * nck
    order = range(nck) if d == 0 else range(nck - 1, -1, -1)
    for j in order:
        lo, hi = j * DN_CHUNK, (j + 1) * DN_CHUNK
        s16 = s.astype(BF16)
        vn = u[lo:hi] - _dot(w16[lo:hi], s16)
        oq[j] = _dot(qd16[lo:hi], s16)
        vnew[j] = vn
        s = s * jnp.exp(gtot_col[lo:lo + 1, :]) + _dot_tn(kd16[lo:hi], vn.astype(BF16))
    s_ref[d] = s
    vn_all = jnp.concatenate(vnew, axis=0).astype(BF16)
    o_ref[...] = jnp.concatenate(oq, axis=0) + _dot(intra, vn_all)


def _dn_main_kernel(qf_ref, kf_ref, vf_ref, scf_ref, sctf_ref,
                    qb_ref, kb_ref, vb_ref, scb_ref, sctb_ref, of_ref, ob_ref, s_ref):
    hd = pl.program_id(0)

    @pl.when(pl.program_id(1) == 0)
    def _():
        s_ref[...] = jnp.zeros(s_ref.shape, F32)

    _dn_direction(0, hd, qf_ref, kf_ref, vf_ref, scf_ref, sctf_ref, s_ref, of_ref)
    _dn_direction(1, hd, qb_ref, kb_ref, vb_ref, scb_ref, sctb_ref, s_ref, ob_ref)


def _dn_main(qn, kn, vc, sc, sct):
    seq = qn.shape[0]
    gsz = DN_GROUP
    ng = seq // gsz
    hw = DN_HEAD_DIM
    sct_rows = sct.shape[0]
    fwd = lambda h, i: (i, h)
    bwd = lambda h, i: (ng - 1 - i, h)
    in_specs = [
        pl.BlockSpec((gsz, hw), fwd), pl.BlockSpec((gsz, hw), fwd), pl.BlockSpec((gsz, hw), fwd),
        pl.BlockSpec((gsz, AB_PAD), lambda h, i: (i, 0)),
        pl.BlockSpec((sct_rows, gsz), lambda h, i: (0, i)),
        pl.BlockSpec((gsz, hw), bwd), pl.BlockSpec((gsz, hw), bwd), pl.BlockSpec((gsz, hw), bwd),
        pl.BlockSpec((gsz, AB_PAD), lambda h, i: (ng - 1 - i, 0)),
        pl.BlockSpec((sct_rows, gsz), lambda h, i: (0, ng - 1 - i)),
    ]
    return pl.pallas_call(
        _dn_main_kernel,
        grid=(DN_HEADS, ng),
        in_specs=in_specs,
        out_specs=[pl.BlockSpec((gsz, hw), fwd), pl.BlockSpec((gsz, hw), bwd)],
        out_shape=[jax.ShapeDtypeStruct((seq, DN_WIDTH), F32)] * 2,
        scratch_shapes=[pltpu.VMEM((2, hw, hw), F32)],
        compiler_params=_cparams(("parallel", "arbitrary")),
        name="dn_main",
    )(qn, kn, vc, sc, sct, qn, kn, vc, sc, sct)


def _dn_epi_kernel(of_ref, ob_ref, z_ref, g_ref, o_ref):
    o = of_ref[...] + ob_ref[...]
    g = g_ref[...]
    outs = []
    for hd in range(DN_HEADS):
        oh = o[:, hd * DN_HEAD_DIM:(hd + 1) * DN_HEAD_DIM]
        ms = jnp.mean(oh * oh, axis=-1, keepdims=True)
        outs.append(oh * lax.rsqrt(ms + EPS) * g)
    y = jnp.concatenate(outs, axis=1)
    o_ref[...] = (y * _silu(z_ref[...].astype(F32))).astype(BF16)


def _dn_epi(o_f, o_b, h, norm_g):
    seq = o_f.shape[0]
    tm = min(512, seq)
    w = DN_WIDTH
    return pl.pallas_call(
        _dn_epi_kernel,
        grid=(seq // tm,),
        in_specs=[
            pl.BlockSpec((tm, w), lambda i: (i, 0)),
            pl.BlockSpec((tm, w), lambda i: (i, 0)),
            pl.BlockSpec((tm, w), lambda i: (i, OFF_ZB // w)),
            pl.BlockSpec((1, DN_HEAD_DIM), lambda i: (0, 0)),
        ],
        out_specs=pl.BlockSpec((tm, w), lambda i: (i, 0)),
        out_shape=jax.ShapeDtypeStruct((seq, w), BF16),
        compiler_params=_cparams(("parallel",)),
        name="dn_epi",
    )(o_f, o_b, h, norm_g)


def _dn_branch(h, ab, lp):
    qn, kn, vc, sc = _dn_prep(h, ab, lp["dn_conv"], lp["dn_alog"], lp["dn_dtb"])
    sct = jnp.pad(sc[:, :36].T, ((0, 4), (0, 0)))
    o_f, o_b = _dn_main(qn, kn, vc, sc, sct)
    return _dn_epi(o_f, o_b, h, lp["dn_norm_g"])


def _att_prep_kernel(q_ref, k_ref, cos_ref, sin_ref, gq_ref, gk_ref, qo_ref, ko_ref):
    cos2 = cos_ref[...]
    sin2 = sin_ref[...]

    def norm_rope(x, g):
        ms = jnp.mean(x * x, axis=-1, keepdims=True)
        xn = x * lax.rsqrt(ms + EPS) * g
        return xn * cos2 + pltpu.roll(xn, ATT_HEAD_DIM // 2, axis=1) * sin2

    gq = gq_ref[...]
    gk = gk_ref[...]
    scale = ATT_HEAD_DIM ** -0.5
    for hd in range(ATT_HEADS):
        sl = slice(hd * ATT_HEAD_DIM, (hd + 1) * ATT_HEAD_DIM)
        qo_ref[:, sl] = (norm_rope(q_ref[:, sl].astype(F32), gq) * scale).astype(BF16)
    for hd in range(ATT_KV_HEADS):
        sl = slice(hd * ATT_HEAD_DIM, (hd + 1) * ATT_HEAD_DIM)
        ko_ref[:, sl] = norm_rope(k_ref[:, sl].astype(F32), gk).astype(BF16)


def _att_prep(h, cos2, sin2, gq, gk):
    seq = h.shape[0]
    tm = min(512, seq)
    return pl.pallas_call(
        _att_prep_kernel,
        grid=(seq // tm,),
        in_specs=[
            pl.BlockSpec((tm, ATT_WIDTH), lambda i: (i, OFF_AQ // ATT_WIDTH)),
            pl.BlockSpec((tm, ATT_KV_WIDTH), lambda i: (i, OFF_AK // ATT_KV_WIDTH)),
            pl.BlockSpec((tm, ATT_HEAD_DIM), lambda i: (i, 0)),
            pl.BlockSpec((tm, ATT_HEAD_DIM), lambda i: (i, 0)),
            pl.BlockSpec((1, ATT_HEAD_DIM), lambda i: (0, 0)),
            pl.BlockSpec((1, ATT_HEAD_DIM), lambda i: (0, 0)),
        ],
        out_specs=[
            pl.BlockSpec((tm, ATT_WIDTH), lambda i: (i, 0)),
            pl.BlockSpec((tm, ATT_KV_WIDTH), lambda i: (i, 0)),
        ],
        out_shape=[
            jax.ShapeDtypeStruct((seq, ATT_WIDTH), BF16),
            jax.ShapeDtypeStruct((seq, ATT_KV_WIDTH), BF16),
        ],
        compiler_params=_cparams(("parallel",)),
        name="att_prep",
    )(h, h, cos2, sin2, gq, gk)


def _att_kernel(q_ref, k_ref, v_ref, z_ref, o_ref):
    k = k_ref[...]
    v = v_ref[...]
    grp = ATT_HEADS // ATT_KV_HEADS
    for hh in range(grp):
        sl = slice(hh * ATT_HEAD_DIM, (hh + 1) * ATT_HEAD_DIM)
        s = _dot_nt(q_ref[:, sl], k)
        m = jnp.max(s, axis=-1, keepdims=True)
        p = jnp.exp(s - m)
        l = jnp.sum(p, axis=-1, keepdims=True)
        o = _dot(p.astype(BF16), v) / l
        o_ref[:, sl] = (o * _silu(z_ref[:, sl].astype(F32))).astype(BF16)


def _att(qr, kr, h):
    seq = qr.shape[0]
    tq = min(128, seq)
    gw = ATT_WIDTH // ATT_KV_HEADS
    return pl.pallas_call(
        _att_kernel,
        grid=(ATT_KV_HEADS, seq // tq),
        in_specs=[
            pl.BlockSpec((tq, gw), lambda g, i: (i, g)),
            pl.BlockSpec((seq, ATT_HEAD_DIM), lambda g, i: (0, g)),
            pl.BlockSpec((seq, ATT_HEAD_DIM), lambda g, i: (0, OFF_AV // ATT_HEAD_DIM + g)),
            pl.BlockSpec((tq, gw), lambda g, i: (i, OFF_ZC // gw + g)),
        ],
        out_specs=pl.BlockSpec((tq, gw), lambda g, i: (i, g)),
        out_shape=jax.ShapeDtypeStruct((seq, ATT_WIDTH), BF16),
        compiler_params=_cparams(("parallel", "parallel")),
        name="grid_att",
    )(qr, kr, h, h)


def _mem_kv_kernel(mem_ref, g_ref, w_ref, o_ref):
    x = mem_ref[...]
    ms = jnp.mean(x * x, axis=-1, keepdims=True)
    xn = (x * lax.rsqrt(ms + EPS) * g_ref[...]).astype(BF16)
    o_ref[...] = _dot(xn, w_ref[...].astype(BF16)).astype(BF16)


def _mem_kv(mem, g, w_kv):
    n_mem = mem.shape[0]
    tn = 512
    return pl.pallas_call(
        _mem_kv_kernel,
        grid=(2 * MEM_WIDTH // tn,),
        in_specs=[
            pl.BlockSpec((n_mem, D_MODEL), lambda j: (0, 0)),
            pl.BlockSpec((1, D_MODEL), lambda j: (0, 0)),
            pl.BlockSpec((D_MODEL, tn), lambda j: (0, j)),
        ],
        out_specs=pl.BlockSpec((n_mem, tn), lambda j: (0, j)),
        out_shape=jax.ShapeDtypeStruct((n_mem, 2 * MEM_WIDTH), BF16),
        compiler_params=_cparams(("parallel",)),
        name="mem_kv",
    )(mem, g, w_kv)


def _mem_att_kernel(q_ref, z_ref, kv_ref, o_ref):
    scale = MEM_HEAD_DIM ** -0.5
    for hd in range(MEM_HEADS):
        sl = slice(hd * MEM_HEAD_DIM, (hd + 1) * MEM_HEAD_DIM)
        k = kv_ref[:, sl]
        v = kv_ref[:, MEM_WIDTH + hd * MEM_HEAD_DIM:MEM_WIDTH + (hd + 1) * MEM_HEAD_DIM]
        s = _dot_nt(q_ref[:, sl], k) * scale
        m = jnp.max(s, axis=-1, keepdims=True)
        p = jnp.exp(s - m)
        l = jnp.sum(p, axis=-1, keepdims=True)
        o = _dot(p.astype(BF16), v) / l
        o_ref[:, sl] = (o * _silu(z_ref[:, sl].astype(F32))).astype(BF16)


def _mem_att(h, kv):
    seq = h.shape[0]
    tm = min(1024, seq)
    n_mem = kv.shape[0]
    return pl.pallas_call(
        _mem_att_kernel,
        grid=(seq // tm,),
        in_specs=[
            pl.BlockSpec((tm, MEM_WIDTH), lambda i: (i, OFF_MQ // MEM_WIDTH)),
            pl.BlockSpec((tm, MEM_WIDTH), lambda i: (i, OFF_ZM // MEM_WIDTH)),
            pl.BlockSpec((n_mem, 2 * MEM_WIDTH), lambda i: (0, 0)),
        ],
        out_specs=pl.BlockSpec((tm, MEM_WIDTH), lambda i: (i, 0)),
        out_shape=jax.ShapeDtypeStruct((seq, MEM_WIDTH), BF16),
        compiler_params=_cparams(("parallel",)),
        name="mem_att",
    )(h, h, kv)


def _merge_kernel(ya_ref, yb_ref, yc_ref, ym_ref, ga_ref, gb_ref, gc_ref, gm_ref,
                  wa_ref, wb_ref, wc_ref, wm_ref, wo_ref, x_ref, fg_ref, o_ref, acc_ref,
                  *, final_norm, nj):
    j = pl.program_id(1)

    @pl.when(j == 0)
    def _():
        acc_ref[...] = jnp.zeros(acc_ref.shape, F32)

    merged = jax.nn.sigmoid(ga_ref[...].astype(F32)) * _dot(ya_ref[...], wa_ref[...])
    merged = merged + jax.nn.sigmoid(gb_ref[...].astype(F32)) * _dot(yb_ref[...], wb_ref[...])
    merged = merged + jax.nn.sigmoid(gc_ref[...].astype(F32)) * _dot(yc_ref[...], wc_ref[...])
    merged = merged + jax.nn.sigmoid(gm_ref[...].astype(F32)) * _dot(ym_ref[...], wm_ref[...])
    acc_ref[...] += _dot(merged.astype(BF16), wo_ref[...])

    @pl.when(j == nj - 1)
    def _():
        y = x_ref[...] + acc_ref[...]
        if final_norm:
            ms = jnp.mean(y * y, axis=-1, keepdims=True)
            y = y * lax.rsqrt(ms + EPS) * fg_ref[...]
        o_ref[...] = y


def _merge(ys, h, wbs, w_out, x, final_g, final_norm):
    seq = x.shape[0]
    tm = min(512, seq)
    tn = 512
    nj = D_MODEL // tn
    widths = (SSM_WIDTH, DN_WIDTH, ATT_WIDTH, MEM_WIDTH)
    in_specs = [pl.BlockSpec((tm, wd), lambda i, j: (i, 0)) for wd in widths]
    for b in range(N_BRANCH):
        base = (OFF_GATE + b * D_MODEL) // tn
        in_specs.append(pl.BlockSpec((tm, tn), lambda i, j, base=base: (i, base + j)))
    in_specs += [pl.BlockSpec((wd, tn), lambda i, j: (0, j)) for wd in widths]
    in_specs += [
        pl.BlockSpec((tn, D_MODEL), lambda i, j: (j, 0)),
        pl.BlockSpec((tm, D_MODEL), lambda i, j: (i, 0)),
        pl.BlockSpec((1, D_MODEL), lambda i, j: (0, 0)),
    ]
    return pl.pallas_call(
        functools.partial(_merge_kernel, final_norm=final_norm, nj=nj),
        grid=(seq // tm, nj),
        in_specs=in_specs,
        out_specs=pl.BlockSpec((tm, D_MODEL), lambda i, j: (i, 0)),
        out_shape=jax.ShapeDtypeStruct((seq, D_MODEL), F32),
        scratch_shapes=[pltpu.VMEM((tm, D_MODEL), F32)],
        compiler_params=_cparams(("parallel", "arbitrary")),
        name="merge_out",
    )(*ys, h, h, h, h, *wbs, w_out, x, final_g)


def _deinterleave_heads(w, nheads):
    lead = w.shape[:-1]
    w = w.reshape(lead + (nheads, ATT_HEAD_DIM // 2, 2))
    w = jnp.swapaxes(w, -1, -2)
    return w.reshape(lead + (nheads * ATT_HEAD_DIM,))


def _rope_tables(seq):
    rows = seq // GRID_W
    row = jnp.repeat(jnp.arange(rows), GRID_W).astype(F32)
    col = jnp.tile(jnp.arange(GRID_W), rows).astype(F32)
    axis_dim = ATT_HEAD_DIM // 2
    freqs = ROPE_THETA ** (-jnp.arange(0, axis_dim, 2, dtype=F32) / axis_dim)
    ang = jnp.concatenate([row[:, None] * freqs, col[:, None] * freqs], axis=-1)
    cos = jnp.cos(ang)
    sin = jnp.sin(ang)
    return jnp.concatenate([cos, cos], axis=-1), jnp.concatenate([-sin, sin], axis=-1)


def _pair_rows(p):
    return p.reshape(2, SSM_PAIRS, 1, 2 * SSM_STATE).transpose(1, 0, 2, 3)


def _pair_cols(p):
    return p.reshape(2, SSM_PAIRS, 2 * SSM_STATE, 1).transpose(1, 0, 2, 3)


def _layer_params(layer, w_in, ssm_a_re, ssm_a_im, ssm_log_step, ssm_b_re, ssm_b_im, ssm_c_re,
                  ssm_c_im, ssm_d, ssm_w_glu, ssm_b_glu, dn_conv, dn_a_log, dn_dt_bias,
                  dn_norm_g, attn_q_norm, attn_k_norm, w_branch, w_out):
    w = w_in[layer]
    o_da = 1536 + 3 * DN_WIDTH
    o_zb = o_da + 4 * DN_HEADS
    o_aq = o_zb + DN_WIDTH
    o_ak = o_aq + ATT_WIDTH
    o_av = o_ak + ATT_KV_WIDTH
    o_zc = o_av + ATT_KV_WIDTH
    o_mq = o_zc + ATT_WIDTH
    o_zm = o_mq + MEM_WIDTH
    o_gate = o_zm + MEM_WIDTH
    w_main = jnp.concatenate([
        w[:, :o_da],
        w[:, o_zb:o_aq],
        w[:, o_mq:o_gate],
        _deinterleave_heads(w[:, o_ak:o_av], ATT_KV_HEADS),
        w[:, o_av:o_zc],
        _deinterleave_heads(w[:, o_aq:o_ak], ATT_HEADS),
        w[:, o_zc:o_mq],
        w[:, o_gate:],
    ], axis=1).astype(BF16)
    w_ab = jnp.pad(w[:, o_da:o_zb], ((0, 0), (0, AB_PAD - 4 * DN_HEADS))).astype(BF16)

    ls_n = jnp.broadcast_to(ssm_log_step[layer][:, :, None], ssm_a_re[layer].shape)
    flat = lambda p: p.reshape(2, 1, SSM_GROUPS * SSM_STATE)

    def b_tiles(b):
        bt = b.transpose(0, 1, 3, 2).reshape(2, SSM_PAIRS, 2, SSM_GROUP, SSM_STATE)
        bt = bt.transpose(1, 0, 3, 2, 4).reshape(SSM_PAIRS, 2, SSM_GROUP, 2 * SSM_STATE)
        return jnp.tile(bt, (1, 1, SSM_T, 1))

    def c_tiles(c):
        ct = c.transpose(0, 1, 3, 2).reshape(2, SSM_PAIRS, 2 * SSM_STATE, SSM_GROUP)
        ct = ct.transpose(1, 0, 2, 3)
        return jnp.tile(ct, (1, 1, 1, SSM_T))

    conv = dn_conv[layer].T.reshape(DN_CONV, 3, DN_WIDTH).transpose(1, 0, 2)
    conv = jnp.pad(conv, ((0, 0), (0, 8 - DN_CONV), (0, 0)))

    def lane_vec(p):
        return jnp.pad(p.reshape(1, -1), ((0, 0), (0, AB_PAD - p.size)))

    wb = w_branch[layer].astype(BF16)
    offs = np.cumsum((0, SSM_WIDTH, DN_WIDTH, ATT_WIDTH, MEM_WIDTH))
    return {
        "w_main": w_main, "w_ab": w_ab,
        "ssm_rows": [_pair_rows(ssm_a_re[layer]), _pair_rows(ssm_a_im[layer]), _pair_rows(ls_n)],
        "ssm_cols": [_pair_cols(ssm_a_re[layer]), _pair_cols(ssm_a_im[layer]), _pair_cols(ls_n)],
        "ssm_b": [b_tiles(ssm_b_re[layer]), b_tiles(ssm_b_im[layer])],
        "ssm_c": [c_tiles(ssm_c_re[layer]), c_tiles(ssm_c_im[layer])],
        "ssm_flat": [flat(ssm_a_re[layer]), flat(ssm_a_im[layer]), flat(ls_n)],
        "ssm_d": ssm_d[layer].reshape(1, SSM_WIDTH),
        "ssm_w_glu": ssm_w_glu[layer],
        "ssm_b_glu": ssm_b_glu[layer].reshape(1, SSM_WIDTH),
        "dn_conv": conv,
        "dn_alog": lane_vec(dn_a_log[layer]),
        "dn_dtb": lane_vec(dn_dt_bias[layer]),
        "dn_norm_g": dn_norm_g[layer].reshape(1, DN_HEAD_DIM),
        "att_gq": _deinterleave_heads(attn_q_norm[layer].reshape(1, ATT_HEAD_DIM), 1),
        "att_gk": _deinterleave_heads(attn_k_norm[layer].reshape(1, ATT_HEAD_DIM), 1),
        "w_branch": [wb[int(offs[b]):int(offs[b + 1])] for b in range(N_BRANCH)],
        "w_out": w_out[layer].astype(BF16),
    }


def kernel(x, mem, norm_g, w_in, ssm_a_re, ssm_a_im, ssm_log_step, ssm_b_re, ssm_b_im, ssm_c_re, ssm_c_im, ssm_d, ssm_w_glu, ssm_b_glu, dn_conv, dn_a_log, dn_dt_bias, dn_norm_g, attn_q_norm, attn_k_norm, mem_norm_g, w_mem_kv, w_branch, w_out, final_norm_g):
    bsz, seq, _ = x.shape
    depth = w_in.shape[0]
    cos2, sin2 = _rope_tables(seq)
    final_g = final_norm_g.reshape(1, D_MODEL)
    outs = []
    for b in range(bsz):
        xb = x[b]
        for layer in range(depth):
            lp = _layer_params(layer, w_in, ssm_a_re, ssm_a_im, ssm_log_step, ssm_b_re, ssm_b_im,
                               ssm_c_re, ssm_c_im, ssm_d, ssm_w_glu, ssm_b_glu, dn_conv, dn_a_log,
                               dn_dt_bias, dn_norm_g, attn_q_norm, attn_k_norm, w_branch, w_out)
            h, ab = _inproj(xb, norm_g[layer].reshape(1, D_MODEL), lp["w_main"], lp["w_ab"])
            y_a = _ssm_branch(h, lp)
            y_b = _dn_branch(h, ab, lp)
            qr, kr = _att_prep(h, cos2, sin2, lp["att_gq"], lp["att_gk"])
            y_c = _att(qr, kr, h)
            kv = _mem_kv(mem[b], mem_norm_g[layer].reshape(1, D_MODEL), w_mem_kv[layer])
            y_m = _mem_att(h, kv)
            xb = _merge((y_a, y_b, y_c, y_m), h, lp["w_branch"], lp["w_out"], xb, final_g,
                        final_norm=(layer == depth - 1))
        outs.append(xb)
    return outs[0][None] if bsz == 1 else jnp.stack(outs, axis=0)
```

```python
import functools
import math

import numpy as np
import jax
import jax.numpy as jnp
from jax import lax
from jax.experimental import pallas as pl
from jax.experimental.pallas import tpu as pltpu

F32 = jnp.float32
BF16 = jnp.bfloat16

D_MODEL = 2048
GRID_W = 64
EPS = 1e-6

SSM_GROUP = 16
SSM_STATE = 64
SSM_GROUPS = 48
SSM_WIDTH = SSM_GROUPS * SSM_GROUP
SSM_T = 16
SSM_PAIRS = SSM_GROUPS // 2
SSM_CW = SSM_T * SSM_GROUP

DN_HEADS = 6
DN_HEAD_DIM = 128
DN_WIDTH = DN_HEADS * DN_HEAD_DIM
DN_CONV = 5
DN_CHUNK = 64
DN_GROUP = 256
DN_HB = 6

ATT_HEADS = 8
ATT_KV_HEADS = 2
ATT_HEAD_DIM = 128
ATT_WIDTH = ATT_HEADS * ATT_HEAD_DIM
ATT_KV_WIDTH = ATT_KV_HEADS * ATT_HEAD_DIM
ROPE_THETA = 10000.0

MEM_HEADS = 4
MEM_HEAD_DIM = 128
MEM_WIDTH = MEM_HEADS * MEM_HEAD_DIM

N_BRANCH = 4

OFF_UA = 0
OFF_ZA = 768
OFF_DQ = 1536
OFF_DK = 2304
OFF_DV = 3072
OFF_ZB = 3840
OFF_MQ = 4608
OFF_ZM = 5120
OFF_AK = 5632
OFF_AV = 5888
OFF_AQ = 6144
OFF_ZC = 7168
OFF_GATE = 8192
H_WIDTH = OFF_GATE + N_BRANCH * D_MODEL
AB_PAD = 128

VMEM_LIMIT = 56 * 1024 * 1024


def _cparams(sem):
    return pltpu.CompilerParams(dimension_semantics=sem, vmem_limit_bytes=VMEM_LIMIT)


def _silu(x):
    return x * jax.nn.sigmoid(x)


def _dot(a, b):
    return jnp.dot(a, b, preferred_element_type=F32)


def _dot_nt(a, b):
    return lax.dot_general(a, b, (((1,), (1,)), ((), ())), preferred_element_type=F32)


def _dot_tn(a, b):
    return lax.dot_general(a, b, (((0,), (0,)), ((), ())), preferred_element_type=F32)


def _split3(x):
    x1 = x.astype(BF16)
    r1 = x - x1.astype(F32)
    x2 = r1.astype(BF16)
    x3 = (r1 - x2.astype(F32)).astype(BF16)
    return x1, x2, x3


def _dot_exact_lhs(a_bf16, x):
    x1, x2, x3 = _split3(x)
    return _dot(a_bf16, x1) + _dot(a_bf16, x2) + _dot(a_bf16, x3)


def _dot_exact_rhs(x, b_bf16):
    x1, x2, x3 = _split3(x)
    return _dot(x1, b_bf16) + _dot(x2, b_bf16) + _dot(x3, b_bf16)


def _dot_f32(a, b):
    a1, a2, a3 = _split3(a)
    b1, b2, b3 = _split3(b)
    return (_dot(a1, b1) + (_dot(a1, b2) + _dot(a2, b1))
            + (_dot(a1, b3) + _dot(a2, b2) + _dot(a3, b1)))


def _inproj_kernel(x_ref, g_ref, w_ref, wab_ref, h_ref, ab_ref, xn_ref):
    @pl.when(pl.program_id(1) == 0)
    def _():
        x = x_ref[...]
        ms = jnp.mean(x * x, axis=-1, keepdims=True)
        xn_ref[...] = (x * lax.rsqrt(ms + EPS) * g_ref[...]).astype(BF16)
        ab_ref[...] = _dot(xn_ref[...], wab_ref[...])

    h_ref[...] = _dot(xn_ref[...], w_ref[...]).astype(BF16)


def _inproj(x, g, w_main, w_ab):
    seq = x.shape[0]
    tm = min(1024, seq)
    tn = 1024
    return pl.pallas_call(
        _inproj_kernel,
        grid=(seq // tm, H_WIDTH // tn),
        in_specs=[
            pl.BlockSpec((tm, D_MODEL), lambda i, j: (i, 0)),
            pl.BlockSpec((1, D_MODEL), lambda i, j: (0, 0)),
            pl.BlockSpec((D_MODEL, tn), lambda i, j: (0, j)),
            pl.BlockSpec((D_MODEL, AB_PAD), lambda i, j: (0, 0)),
        ],
        out_specs=[
            pl.BlockSpec((tm, tn), lambda i, j: (i, j)),
            pl.BlockSpec((tm, AB_PAD), lambda i, j: (i, 0)),
        ],
        out_shape=[
            jax.ShapeDtypeStruct((seq, H_WIDTH), BF16),
            jax.ShapeDtypeStruct((seq, AB_PAD), F32),
        ],
        scratch_shapes=[pltpu.VMEM((tm, D_MODEL), BF16)],
        compiler_params=_cparams(("parallel", "arbitrary")),
        name="inproj",
    )(x, g, w_main, w_ab)


def _cpow_table(base_re, base_im, n):
    re = [jnp.ones_like(base_re)]
    im = [jnp.zeros_like(base_im)]
    for _ in range(n):
        re_n = re[-1] * base_re - im[-1] * base_im
        im_n = re[-1] * base_im + im[-1] * base_re
        re.append(re_n)
        im.append(im_n)
    return re, im


def _ssm_prep_kernel(are_r, aim_r, ls_r, bre_ref, bim_ref, cre_ref, cim_ref,
                     win_ref, wout_ref, m_ref):
    t = SSM_T
    cw = SSM_CW
    lane128 = lax.broadcasted_iota(jnp.int32, (1, 128), 1)
    col_t = lax.broadcasted_iota(jnp.int32, (1, cw), 1) // SSM_GROUP
    sub128 = lax.broadcasted_iota(jnp.int32, (128, 1), 0)
    lane_cw = lax.broadcasted_iota(jnp.int32, (SSM_GROUP, cw), 1)

    m_ref[...] = jnp.zeros(m_ref.shape, m_ref.dtype)

    krows = [[None, None], [None, None]]
    for d in range(2):
        step = jnp.exp(ls_r[d])
        a_re = are_r[d]
        a_im = aim_r[d]
        mag = jnp.exp(a_re * step)
        lam_re = mag * jnp.cos(a_im * step)
        lam_im = mag * jnp.sin(a_im * step)
        den = a_re * a_re + a_im * a_im
        nr = lam_re - 1.0
        ni = lam_im
        coef_re = (nr * a_re + ni * a_im) / den
        coef_im = (ni * a_re - nr * a_im) / den
        b_re = bre_ref[d]
        b_im = bim_ref[d]
        bb_re = coef_re * b_re - coef_im * b_im
        bb_im = coef_re * b_im + coef_im * b_re
        pr_re, pr_im = _cpow_table(lam_re, lam_im, t)
        e_in = [(t - 1 - s) if d == 0 else s for s in range(t)]
        p_re = jnp.concatenate([jnp.broadcast_to(pr_re[e], (SSM_GROUP, 128)) for e in e_in], axis=0)
        p_im = jnp.concatenate([jnp.broadcast_to(pr_im[e], (SSM_GROUP, 128)) for e in e_in], axis=0)
        w_re = p_re * bb_re - p_im * bb_im
        w_im = p_re * bb_im + p_im * bb_re
        for par in range(2):
            keep = (lane128 < 64) if par == 0 else (lane128 >= 64)
            win_ref[par * cw:(par + 1) * cw, (2 * d) * 128:(2 * d + 1) * 128] = (
                jnp.where(keep, w_re, 0.0).astype(BF16))
            win_ref[par * cw:(par + 1) * cw, (2 * d + 1) * 128:(2 * d + 2) * 128] = (
                jnp.where(keep, w_im, 0.0).astype(BF16))

        tab_re = jnp.zeros((128, 128), F32)
        tab_im = jnp.zeros((128, 128), F32)
        for e in range(t + 1):
            tab_re = jnp.where(sub128 == e, pr_re[e], tab_re)
            tab_im = jnp.where(sub128 == e, pr_im[e], tab_im)
        tab_re = tab_re.T
        tab_im = tab_im.T
        c_re = cre_ref[d]
        c_im = cim_ref[d]

        def c_lam(expo_row):
            sel = jnp.where(sub128 == expo_row, 1.0, 0.0).astype(BF16)
            q_re = _dot_exact_rhs(tab_re, sel)
            q_im = _dot_exact_rhs(tab_im, sel)
            return c_re * q_re - c_im * q_im, -(c_re * q_im + c_im * q_re)

        o_re, o_im = c_lam((col_t + 1) if d == 0 else (t - col_t))
        for par in range(2):
            keep = (sub128 < 64) if par == 0 else (sub128 >= 64)
            wout_ref[(2 * d) * 128:(2 * d + 1) * 128, par * cw:(par + 1) * cw] = (
                jnp.where(keep, o_re, 0.0).astype(BF16))
            wout_ref[(2 * d + 1) * 128:(2 * d + 2) * 128, par * cw:(par + 1) * cw] = (
                jnp.where(keep, o_im, 0.0).astype(BF16))

        r_re, r_im = c_lam(col_t if d == 0 else (t - 1 - col_t))
        bbt_re = bb_re[0:SSM_GROUP, :]
        bbt_im = bb_im[0:SSM_GROUP, :]
        for par in range(2):
            keep = (lane128 < 64) if par == 0 else (lane128 >= 64)
            krows[d][par] = (_dot_f32(jnp.where(keep, bbt_re, 0.0), r_re)
                             + _dot_f32(jnp.where(keep, bbt_im, 0.0), r_im))

    for par in range(2):
        kf = krows[0][par]
        kb = krows[1][par]
        for s in range(t):
            sh_f = SSM_GROUP * s
            blk = jnp.where(lane_cw >= sh_f, pltpu.roll(kf, sh_f, axis=1) if sh_f else kf, 0.0)
            sh_b = SSM_GROUP * (t - 1 - s)
            rolled_b = pltpu.roll(kb, cw - sh_b, axis=1) if sh_b else kb
            blk = blk + jnp.where(lane_cw < cw - sh_b, rolled_b, 0.0)
            m_ref[par * cw + s * SSM_GROUP:par * cw + (s + 1) * SSM_GROUP,
                  par * cw:(par + 1) * cw] = blk.astype(BF16)


def _ssm_prep(rows, b_t, c_t):
    pw = 2 * SSM_CW
    row_spec = pl.BlockSpec((None, 2, 1, 128), lambda g: (g, 0, 0, 0))
    b_spec = pl.BlockSpec((None, 2, SSM_CW, 128), lambda g: (g, 0, 0, 0))
    c_spec = pl.BlockSpec((None, 2, 128, SSM_CW), lambda g: (g, 0, 0, 0))
    w_spec = pl.BlockSpec((None, pw, pw), lambda g: (g, 0, 0))
    return pl.pallas_call(
        _ssm_prep_kernel,
        grid=(SSM_PAIRS,),
        in_specs=[row_spec] * 3 + [b_spec] * 2 + [c_spec] * 2,
        out_specs=[w_spec] * 3,
        out_shape=[jax.ShapeDtypeStruct((SSM_PAIRS, pw, pw), BF16)] * 3,
        compiler_params=_cparams(("parallel",)),
        name="ssm_prep",
    )(*rows, *b_t, *c_t)


SSM_BG = 8
SSM_BP = SSM_BG // 2
SSM_BW = SSM_BG * SSM_CW


def _ssm_place(t0):
    r = lax.broadcasted_iota(jnp.int32, (256, SSM_BW), 0)
    c = lax.broadcasted_iota(jnp.int32, (256, SSM_BW), 1)
    j = r % 128
    target = (j // SSM_GROUP) * SSM_CW + (t0 + r // 128) * SSM_GROUP + j % SSM_GROUP
    return jnp.where(c == target, 1.0, 0.0).astype(BF16)


def _ssm_in_kernel(u_ref, win_ref, u2_ref, h_ref, uf_ref):
    nchunk = u2_ref.shape[0]
    uf_ref[...] = u_ref[...].astype(F32)
    acc = jnp.zeros((nchunk, SSM_BW), F32)
    for t0 in range(0, SSM_T, 2):
        lhs = jnp.concatenate([uf_ref[pl.ds(t0, nchunk, stride=SSM_T), :],
                               uf_ref[pl.ds(t0 + 1, nchunk, stride=SSM_T), :]], axis=1)
        acc = acc + _dot(lhs.astype(BF16), _ssm_place(t0))
    u2_ref[...] = acc.astype(BF16)
    pw = 2 * SSM_CW
    for j in range(SSM_BP):
        h = _dot(u2_ref[:, j * pw:(j + 1) * pw], win_ref[j])
        for k in range(4):
            h_ref[k, :, j * 128:(j + 1) * 128] = h[:, k * 128:(k + 1) * 128]


def _ssm_in(h, win):
    seq = h.shape[0]
    nchunk = seq // SSM_T
    pw = 2 * SSM_CW
    nblk = SSM_GROUPS // SSM_BG
    return pl.pallas_call(
        _ssm_in_kernel,
        grid=(nblk,),
        in_specs=[
            pl.BlockSpec((seq, 128), lambda b: (0, OFF_UA // 128 + b)),
            pl.BlockSpec((SSM_BP, pw, pw), lambda b: (b, 0, 0)),
        ],
        out_specs=[
            pl.BlockSpec((nchunk, SSM_BW), lambda b: (0, b)),
            pl.BlockSpec((4, nchunk, SSM_BP * 128), lambda b: (0, 0, b)),
        ],
        out_shape=[
            jax.ShapeDtypeStruct((nchunk, SSM_GROUPS * SSM_CW), BF16),
            jax.ShapeDtypeStruct((4, nchunk, SSM_PAIRS * 128), F32),
        ],
        scratch_shapes=[pltpu.VMEM((seq, 128), F32)],
        compiler_params=_cparams(("parallel",)),
        name="ssm_in",
    )(h, win)


def _ssm_scan_kernel(are_ref, aim_ref, ls_ref, h_ref, p_ref, *, nchunk):
    width = h_ref.shape[2]

    def lam_pow_t(d):
        step = jnp.exp(ls_ref[d]) * float(SSM_T)
        mag = jnp.exp(are_ref[d] * step)
        ang = aim_ref[d] * step
        return mag * jnp.cos(ang), mag * jnp.sin(ang)

    lfr, lfi = lam_pow_t(0)
    lbr, lbi = lam_pow_t(1)

    def body(c, carry):
        fr, fi, br, bi = carry
        cb = nchunk - 1 - c
        p_ref[0, pl.ds(c, 1), :] = fr
        p_ref[1, pl.ds(c, 1), :] = fi
        p_ref[2, pl.ds(cb, 1), :] = br
        p_ref[3, pl.ds(cb, 1), :] = bi
        hfr = h_ref[0, pl.ds(c, 1), :]
        hfi = h_ref[1, pl.ds(c, 1), :]
        hbr = h_ref[2, pl.ds(cb, 1), :]
        hbi = h_ref[3, pl.ds(cb, 1), :]
        nfr = lfr * fr - lfi * fi + hfr
        nfi = lfr * fi + lfi * fr + hfi
        nbr = lbr * br - lbi * bi + hbr
        nbi = lbr * bi + lbi * br + hbi
        return nfr, nfi, nbr, nbi

    z = jnp.zeros((1, width), F32)
    lax.fori_loop(0, nchunk, body, (z, z, z, z))


def _ssm_scan(flat_params, h):
    nchunk = h.shape[1]
    width = h.shape[2]
    wt = 768
    return pl.pallas_call(
        functools.partial(_ssm_scan_kernel, nchunk=nchunk),
        grid=(width // wt,),
        in_specs=[pl.BlockSpec((2, 1, wt), lambda j: (0, 0, j))] * 3 + [
            pl.BlockSpec((4, nchunk, wt), lambda j: (0, 0, j)),
        ],
        out_specs=pl.BlockSpec((4, nchunk, wt), lambda j: (0, 0, j)),
        out_shape=jax.ShapeDtypeStruct(h.shape, F32),
        compiler_params=_cparams(("parallel",)),
        name="ssm_scan",
    )(*flat_params, h)


def _ssm_out_kernel(u2_ref, p_ref, m_ref, wout_ref, y_ref):
    nchunk = u2_ref.shape[0]
    pw = 2 * SSM_CW
    hi = []
    lo = []
    for j in range(SSM_BP):
        acc = _dot(u2_ref[:, j * pw:(j + 1) * pw], m_ref[j])
        for k in range(4):
            acc = acc + _dot(p_ref[k, :, j * 128:(j + 1) * 128].astype(BF16),
                             wout_ref[j, k * 128:(k + 1) * 128, :])
        acc_hi = acc.astype(BF16)
        hi.append(acc_hi)
        lo.append((acc - acc_hi.astype(F32)).astype(BF16))
    y_hi = jnp.concatenate(hi, axis=1)
    y_lo = jnp.concatenate(lo, axis=1)
    for t0 in range(0, SSM_T, 2):
        place = _ssm_place(t0)
        yt = _dot_nt(y_hi, place) + _dot_nt(y_lo, place)
        y_ref[pl.ds(t0, nchunk, stride=SSM_T), :] = yt[:, :128]
        y_ref[pl.ds(t0 + 1, nchunk, stride=SSM_T), :] = yt[:, 128:]


def _ssm_out(u2, p, m, wout):
    nchunk = u2.shape[0]
    seq = nchunk * SSM_T
    pw = 2 * SSM_CW
    nblk = SSM_GROUPS // SSM_BG
    return pl.pallas_call(
        _ssm_out_kernel,
        grid=(nblk,),
        in_specs=[
            pl.BlockSpec((nchunk, SSM_BW), lambda b: (0, b)),
            pl.BlockSpec((4, nchunk, SSM_BP * 128), lambda b: (0, 0, b)),
            pl.BlockSpec((SSM_BP, pw, pw), lambda b: (b, 0, 0)),
            pl.BlockSpec((SSM_BP, pw, pw), lambda b: (b, 0, 0)),
        ],
        out_specs=pl.BlockSpec((seq, 128), lambda b: (0, b)),
        out_shape=jax.ShapeDtypeStruct((seq, SSM_WIDTH), F32),
        compiler_params=_cparams(("parallel",)),
        name="ssm_out",
    )(u2, p, m, wout)


def _ssm_epi_kernel(y_ref, u_ref, z_ref, d_ref, wg_ref, bg_ref, o_ref):
    y = y_ref[...] + d_ref[...] * u_ref[...].astype(F32)
    y = jax.nn.gelu(y)
    glu = _dot(y.astype(BF16), wg_ref[...].astype(BF16)) + bg_ref[...]
    y = y * jax.nn.sigmoid(glu)
    o_ref[...] = (y * _silu(z_ref[...].astype(F32))).astype(BF16)


def _ssm_epi(y, h, d, w_glu, b_glu):
    seq = y.shape[0]
    tm = min(512, seq)
    w = SSM_WIDTH
    return pl.pallas_call(
        _ssm_epi_kernel,
        grid=(seq // tm,),
        in_specs=[
            pl.BlockSpec((tm, w), lambda i: (i, 0)),
            pl.BlockSpec((tm, w), lambda i: (i, OFF_UA // w)),
            pl.BlockSpec((tm, w), lambda i: (i, OFF_ZA // w)),
            pl.BlockSpec((1, w), lambda i: (0, 0)),
            pl.BlockSpec((w, w), lambda i: (0, 0)),
            pl.BlockSpec((1, w), lambda i: (0, 0)),
        ],
        out_specs=pl.BlockSpec((tm, w), lambda i: (i, 0)),
        out_shape=jax.ShapeDtypeStruct((seq, w), BF16),
        compiler_params=_cparams(("parallel",)),
        name="ssm_epi",
    )(y, h, h, d, w_glu, b_glu)


def _ssm_branch(h, lp):
    win, wout, m = _ssm_prep(lp["ssm_rows"], lp["ssm_b"], lp["ssm_c"])
    u2, hs = _ssm_in(h, win)
    p = _ssm_scan(lp["ssm_flat"], hs)
    y = _ssm_out(u2, p, m, wout)
    return _ssm_epi(y, h, lp["ssm_d"], lp["ssm_w_glu"], lp["ssm_b_glu"])


def _dn_prep_kernel(qc, qp, qn, kc, kp, kn, vc, vp, vn, ab_ref, cw_ref, alog_ref, dtb_ref,
                    qo_ref, ko_ref, vo_ref, sc_ref, ext_ref, *, tm, nblk):
    i = pl.program_id(0)
    halo = 16
    pad = DN_CONV // 2

    def conv_silu(cur, prev, nxt, part):
        ext_ref[0:halo, :] = jnp.where(i > 0, prev[...].astype(F32), 0.0)
        ext_ref[halo:halo + tm, :] = cur[...].astype(F32)
        ext_ref[halo + tm:halo + tm + halo, :] = jnp.where(i < nblk - 1, nxt[...].astype(F32), 0.0)
        acc = jnp.zeros((tm, DN_WIDTH), F32)
        for j in range(DN_CONV):
            acc = acc + cw_ref[part, j:j + 1, :] * ext_ref[pl.ds(halo - pad + j, tm), :]
        return _silu(acc)

    def l2n(x):
        outs = []
        for hd in range(DN_HEADS):
            xh = x[:, hd * DN_HEAD_DIM:(hd + 1) * DN_HEAD_DIM]
            outs.append(xh * lax.rsqrt(jnp.sum(xh * xh, axis=-1, keepdims=True) + EPS))
        return jnp.concatenate(outs, axis=1)

    qo_ref[...] = (l2n(conv_silu(qc, qp, qn, 0)) * (DN_HEAD_DIM ** -0.5)).astype(BF16)
    ko_ref[...] = l2n(conv_silu(kc, kp, kn, 1)).astype(BF16)
    vo_ref[...] = conv_silu(vc, vp, vn, 2).astype(BF16)

    ab = ab_ref[...]
    g_all = -jnp.exp(alog_ref[...]) * jax.nn.softplus(ab + dtb_ref[...])
    beta_all = jax.nn.sigmoid(ab)
    r = lax.broadcasted_iota(jnp.int32, (tm, tm), 0)
    c = lax.broadcasted_iota(jnp.int32, (tm, tm), 1)
    same = (r // DN_CHUNK) == (c // DN_CHUNK)
    tri_f = jnp.where(same & (c <= r), 1.0, 0.0).astype(BF16)
    tri_b = jnp.where(same & (c >= r), 1.0, 0.0).astype(BF16)
    blk = jnp.where(same, 1.0, 0.0).astype(BF16)
    gcf = _dot_exact_lhs(tri_f, g_all)
    gcb = _dot_exact_lhs(tri_b, g_all)
    gtot = _dot_exact_lhs(blk, g_all)
    lane = lax.broadcasted_iota(jnp.int32, (tm, AB_PAD), 1)
    sc = jnp.where(lane < 6, gcf,
                   jnp.where(lane < 12, gcb,
                             jnp.where(lane < 24, beta_all,
                                       jnp.where(lane < 36, pltpu.roll(gtot, 24, axis=1), 0.0))))
    sc_ref[...] = sc


def _dn_prep(h, ab, conv_w, alog, dtb):
    seq = h.shape[0]
    tm = min(256, seq)
    nblk = seq // tm
    w = DN_WIDTH
    hb = tm // 16
    nh = seq // 16

    def cur(ci):
        return pl.BlockSpec((tm, w), lambda i: (i, ci))

    def prev(ci):
        return pl.BlockSpec((16, w), lambda i: (jnp.maximum(i * hb - 1, 0), ci))

    def nxt(ci):
        return pl.BlockSpec((16, w), lambda i: (jnp.minimum((i + 1) * hb, nh - 1), ci))

    in_specs = []
    for off in (OFF_DQ, OFF_DK, OFF_DV):
        ci = off // w
        in_specs += [cur(ci), prev(ci), nxt(ci)]
    in_specs += [
        pl.BlockSpec((tm, AB_PAD), lambda i: (i, 0)),
        pl.BlockSpec((3, 8, w), lambda i: (0, 0, 0)),
        pl.BlockSpec((1, AB_PAD), lambda i: (0, 0)),
        pl.BlockSpec((1, AB_PAD), lambda i: (0, 0)),
    ]
    return pl.pallas_call(
        functools.partial(_dn_prep_kernel, tm=tm, nblk=nblk),
        grid=(nblk,),
        in_specs=in_specs,
        out_specs=[pl.BlockSpec((tm, w), lambda i: (i, 0))] * 3
        + [pl.BlockSpec((tm, AB_PAD), lambda i: (i, 0))],
        out_shape=[jax.ShapeDtypeStruct((seq, w), BF16)] * 3
        + [jax.ShapeDtypeStruct((seq, AB_PAD), F32)],
        scratch_shapes=[pltpu.VMEM((tm + 32, w), F32)],
        compiler_params=_cparams(("parallel",)),
        name="dn_prep",
    )(h, h, h, h, h, h, h, h, h, ab, conv_w, alog, dtb)


def _dn_main_kernel(qf_ref, kf_ref, vf_ref, scf_ref, sctf_ref,
                    qb_ref, kb_ref, vb_ref, scb_ref, sctb_ref, of_ref, ob_ref, s_ref):
    gsz = DN_GROUP
    nck = gsz // DN_CHUNK

    @pl.when(pl.program_id(1) == 0)
    def _():
        s_ref[...] = jnp.zeros(s_ref.shape, F32)

    refs = ((qf_ref, kf_ref, vf_ref, scf_ref, sctf_ref, of_ref),
            (qb_ref, kb_ref, vb_ref, scb_ref, sctb_ref, ob_ref))
    r = lax.broadcasted_iota(jnp.int32, (gsz, gsz), 0)
    c = lax.broadcasted_iota(jnp.int32, (gsz, gsz), 1)
    same = (r // DN_CHUNK) == (c // DN_CHUNK)
    incl = (same & (r >= c), same & (r <= c))
    strict = (same & (r > c), same & (r < c))

    chains = []
    for d in range(2):
        q_ref, k_ref, v_ref, sc_ref, sct_ref, o_ref = refs[d]
        sc = sc_ref[...]
        sct = sct_ref[...]
        lane = lax.broadcasted_iota(jnp.int32, sc.shape, 1)
        sub = lax.broadcasted_iota(jnp.int32, sct.shape, 0)
        for hl in range(DN_HB):
            cidx = d * DN_HEADS + pl.program_id(0) * DN_HB + hl
            hs = slice(hl * DN_HEAD_DIM, (hl + 1) * DN_HEAD_DIM)

            def col(ci, sc=sc, lane=lane):
                return jnp.sum(jnp.where(lane == ci, sc, 0.0), axis=1, keepdims=True)

            ch = {"d": d, "hl": hl, "hs": hs, "o_ref": o_ref}
            ch["q"] = q_ref[:, hs]
            ch["k"] = k_ref[:, hs]
            ch["v"] = v_ref[:, hs]
            ch["gc_col"] = col(cidx)
            ch["gc_row"] = jnp.sum(jnp.where(sub == cidx, sct, 0.0), axis=0, keepdims=True)
            ch["beta_col"] = col(12 + cidx)
            ch["gtot_col"] = col(24 + cidx)
            chains.append(ch)

    for ch in chains:
        d = ch["d"]
        ch["decay"] = jnp.where(incl[d], jnp.exp(jnp.where(incl[d], ch["gc_col"] - ch["gc_row"], 0.0)), 0.0)
        kf = ch["k"].astype(F32)
        kb = kf * ch["beta_col"]
        ch["kb16"] = kb.astype(BF16)
        ch["x"] = jnp.concatenate([ch["v"].astype(F32) * ch["beta_col"], kb * jnp.exp(ch["gc_col"])], axis=1)
        ch["qd16"] = (ch["q"].astype(F32) * jnp.exp(ch["gc_col"])).astype(BF16)
        ch["kd16"] = (kf * jnp.exp(ch["gtot_col"] - ch["gc_col"])).astype(BF16)
    for ch in chains:
        ch["p"] = jnp.where(strict[ch["d"]], _dot_nt(ch["kb16"], ch["k"]) * ch["decay"], 0.0).astype(BF16)
    for ch in chains:
        ch["intra"] = (_dot_nt(ch["q"], ch["k"]) * ch["decay"]).astype(BF16)
    for ch in chains:
        ch["x"] = ch["x"] - _dot(ch["p"], ch["x"].astype(BF16))
    npow = 2
    while npow < DN_CHUNK:
        for ch in chains:
            ch["p"] = _dot(ch["p"], ch["p"]).astype(BF16)
        for ch in chains:
            ch["x"] = ch["x"] + _dot(ch["p"], ch["x"].astype(BF16))
        npow *= 2
    for ch in chains:
        ch["u"] = ch["x"][:, :DN_HEAD_DIM]
        ch["w16"] = ch["x"][:, DN_HEAD_DIM:].astype(BF16)
        ch["s"] = s_ref[ch["d"], ch["hl"]]
        ch["vnew"] = [None] * nck
        ch["oq"] = [None] * nck

    for step in range(nck):
        for ch in chains:
            j = step if ch["d"] == 0 else nck - 1 - step
            lo, hi = j * DN_CHUNK, (j + 1) * DN_CHUNK
            s16 = ch["s"].astype(BF16)
            ws = _dot(jnp.concatenate([ch["w16"][lo:hi], ch["qd16"][lo:hi]], axis=0), s16)
            vn = ch["u"][lo:hi] - ws[:DN_CHUNK]
            ch["oq"][j] = ws[DN_CHUNK:]
            ch["vnew"][j] = vn
            ch["s"] = (ch["s"] * jnp.exp(ch["gtot_col"][lo:lo + 1, :])
                       + _dot_tn(ch["kd16"][lo:hi], vn.astype(BF16)))
    for ch in chains:
        s_ref[ch["d"], ch["hl"]] = ch["s"]
        vn_all = jnp.concatenate(ch["vnew"], axis=0).astype(BF16)
        ch["o_ref"][:, ch["hs"]] = jnp.concatenate(ch["oq"], axis=0) + _dot(ch["intra"], vn_all)


def _dn_main(qn, kn, vc, sc, sct):
    seq = qn.shape[0]
    gsz = DN_GROUP
    ng = seq // gsz
    hw = DN_HB * DN_HEAD_DIM
    sct_rows = sct.shape[0]
    fwd = lambda h, i: (i, h)
    bwd = lambda h, i: (ng - 1 - i, h)
    in_specs = [
        pl.BlockSpec((gsz, hw), fwd), pl.BlockSpec((gsz, hw), fwd), pl.BlockSpec((gsz, hw), fwd),
        pl.BlockSpec((gsz, AB_PAD), lambda h, i: (i, 0)),
        pl.BlockSpec((sct_rows, gsz), lambda h, i: (0, i)),
        pl.BlockSpec((gsz, hw), bwd), pl.BlockSpec((gsz, hw), bwd), pl.BlockSpec((gsz, hw), bwd),
        pl.BlockSpec((gsz, AB_PAD), lambda h, i: (ng - 1 - i, 0)),
        pl.BlockSpec((sct_rows, gsz), lambda h, i: (0, ng - 1 - i)),
    ]
    return pl.pallas_call(
        _dn_main_kernel,
        grid=(DN_HEADS // DN_HB, ng),
        in_specs=in_specs,
        out_specs=[pl.BlockSpec((gsz, hw), fwd), pl.BlockSpec((gsz, hw), bwd)],
        out_shape=[jax.ShapeDtypeStruct((seq, DN_WIDTH), F32)] * 2,
        scratch_shapes=[pltpu.VMEM((2, DN_HB, DN_HEAD_DIM, DN_HEAD_DIM), F32)],
        compiler_params=_cparams(("parallel", "arbitrary")),
        name="dn_main",
    )(qn, kn, vc, sc, sct, qn, kn, vc, sc, sct)


def _dn_epi_kernel(of_ref, ob_ref, z_ref, g_ref, o_ref):
    o = of_ref[...] + ob_ref[...]
    g = g_ref[...]
    outs = []
    for hd in range(DN_HEADS):
        oh = o[:, hd * DN_HEAD_DIM:(hd + 1) * DN_HEAD_DIM]
        ms = jnp.mean(oh * oh, axis=-1, keepdims=True)
        outs.append(oh * lax.rsqrt(ms + EPS) * g)
    y = jnp.concatenate(outs, axis=1)
    o_ref[...] = (y * _silu(z_ref[...].astype(F32))).astype(BF16)


def _dn_epi(o_f, o_b, h, norm_g):
    seq = o_f.shape[0]
    tm = min(512, seq)
    w = DN_WIDTH
    return pl.pallas_call(
        _dn_epi_kernel,
        grid=(seq // tm,),
        in_specs=[
            pl.BlockSpec((tm, w), lambda i: (i, 0)),
            pl.BlockSpec((tm, w), lambda i: (i, 0)),
            pl.BlockSpec((tm, w), lambda i: (i, OFF_ZB // w)),
            pl.BlockSpec((1, DN_HEAD_DIM), lambda i: (0, 0)),
        ],
        out_specs=pl.BlockSpec((tm, w), lambda i: (i, 0)),
        out_shape=jax.ShapeDtypeStruct((seq, w), BF16),
        compiler_params=_cparams(("parallel",)),
        name="dn_epi",
    )(o_f, o_b, h, norm_g)


def _dn_branch(h, ab, lp):
    qn, kn, vc, sc = _dn_prep(h, ab, lp["dn_conv"], lp["dn_alog"], lp["dn_dtb"])
    sct = jnp.pad(sc[:, :36].T, ((0, 4), (0, 0)))
    o_f, o_b = _dn_main(qn, kn, vc, sc, sct)
    return _dn_epi(o_f, o_b, h, lp["dn_norm_g"])


def _att_prep_kernel(q_ref, k_ref, v_ref, cos_ref, sin_ref, gq_ref, gk_ref, qt_ref, ko_ref, vt_ref):
    cos2 = cos_ref[...]
    sin2 = sin_ref[...]

    def norm_rope(x, g):
        ms = jnp.mean(x * x, axis=-1, keepdims=True)
        xn = x * lax.rsqrt(ms + EPS) * g
        return xn * cos2 + pltpu.roll(xn, ATT_HEAD_DIM // 2, axis=1) * sin2

    gq = gq_ref[...]
    gk = gk_ref[...]
    scale = ATT_HEAD_DIM ** -0.5 * math.log2(math.e)
    for hd in range(ATT_HEADS):
        sl = slice(hd * ATT_HEAD_DIM, (hd + 1) * ATT_HEAD_DIM)
        qt_ref[sl, :] = (norm_rope(q_ref[:, sl].astype(F32), gq) * scale).T.astype(BF16)
    for hd in range(ATT_KV_HEADS):
        sl = slice(hd * ATT_HEAD_DIM, (hd + 1) * ATT_HEAD_DIM)
        ko_ref[:, sl] = norm_rope(k_ref[:, sl].astype(F32), gk).astype(BF16)
        vt_ref[sl, :] = v_ref[:, sl].astype(F32).T.astype(BF16)


def _att_prep(h, cos2, sin2, gq, gk):
    seq = h.shape[0]
    tm = min(512, seq)
    return pl.pallas_call(
        _att_prep_kernel,
        grid=(seq // tm,),
        in_specs=[
            pl.BlockSpec((tm, ATT_WIDTH), lambda i: (i, OFF_AQ // ATT_WIDTH)),
            pl.BlockSpec((tm, ATT_KV_WIDTH), lambda i: (i, OFF_AK // ATT_KV_WIDTH)),
            pl.BlockSpec((tm, ATT_KV_WIDTH), lambda i: (i, OFF_AV // ATT_KV_WIDTH)),
            pl.BlockSpec((tm, ATT_HEAD_DIM), lambda i: (i, 0)),
            pl.BlockSpec((tm, ATT_HEAD_DIM), lambda i: (i, 0)),
            pl.BlockSpec((1, ATT_HEAD_DIM), lambda i: (0, 0)),
            pl.BlockSpec((1, ATT_HEAD_DIM), lambda i: (0, 0)),
        ],
        out_specs=[
            pl.BlockSpec((ATT_WIDTH, tm), lambda i: (0, i)),
            pl.BlockSpec((tm, ATT_KV_WIDTH), lambda i: (i, 0)),
            pl.BlockSpec((ATT_KV_WIDTH, tm), lambda i: (0, i)),
        ],
        out_shape=[
            jax.ShapeDtypeStruct((ATT_WIDTH, seq), BF16),
            jax.ShapeDtypeStruct((seq, ATT_KV_WIDTH), BF16),
            jax.ShapeDtypeStruct((ATT_KV_WIDTH, seq), BF16),
        ],
        compiler_params=_cparams(("parallel",)),
        name="att_prep",
    )(h, h, h, cos2, sin2, gq, gk)


def _att_kernel(qt_ref, k_ref, vt_ref, z_ref, o_ref, m_ref, l_ref, acc_ref, *, kc, nkc):
    grp = ATT_HEADS // ATT_KV_HEADS
    m_ref[...] = jnp.full(m_ref.shape, -1e30, F32)
    l_ref[...] = jnp.zeros(l_ref.shape, F32)
    acc_ref[...] = jnp.zeros(acc_ref.shape, F32)

    def chunk(c, carry):
        k0 = pl.multiple_of(c * kc, kc)
        k_c = k_ref[pl.ds(k0, kc), :]
        vt_c = vt_ref[:, pl.ds(k0, kc)]
        s = [_dot(k_c, qt_ref[hh * ATT_HEAD_DIM:(hh + 1) * ATT_HEAD_DIM, :])
             for hh in range(grp)]
        p = [None] * grp
        alpha = [None] * grp
        for hh in range(grp):
            m_old = m_ref[hh]
            m_new = jnp.maximum(m_old, jnp.max(s[hh], axis=0, keepdims=True))
            alpha[hh] = jnp.exp2(m_old - m_new)
            ph = jnp.exp2(s[hh] - m_new)
            l_ref[hh] = alpha[hh] * l_ref[hh] + jnp.sum(ph, axis=0, keepdims=True)
            m_ref[hh] = m_new
            p[hh] = ph.astype(BF16)
        for hh in range(grp):
            acc_ref[hh] = alpha[hh] * acc_ref[hh] + _dot(vt_c, p[hh])
        return carry

    lax.fori_loop(0, nkc, chunk, 0)
    for hh in range(grp):
        sl = slice(hh * ATT_HEAD_DIM, (hh + 1) * ATT_HEAD_DIM)
        o = (acc_ref[hh] / l_ref[hh]).T
        o_ref[:, sl] = (o * _silu(z_ref[:, sl].astype(F32))).astype(BF16)


def _att(qt, kr, vt, h):
    seq = kr.shape[0]
    tq = min(256, seq)
    kc = min(1024, seq)
    grp = ATT_HEADS // ATT_KV_HEADS
    gw = ATT_WIDTH // ATT_KV_HEADS
    return pl.pallas_call(
        functools.partial(_att_kernel, kc=kc, nkc=seq // kc),
        grid=(ATT_KV_HEADS, seq // tq),
        in_specs=[
            pl.BlockSpec((gw, tq), lambda g, i: (g, i)),
            pl.BlockSpec((seq, ATT_HEAD_DIM), lambda g, i: (0, g)),
            pl.BlockSpec((ATT_HEAD_DIM, seq), lambda g, i: (g, 0)),
            pl.BlockSpec((tq, gw), lambda g, i: (i, OFF_ZC // gw + g)),
        ],
        out_specs=pl.BlockSpec((tq, gw), lambda g, i: (i, g)),
        out_shape=jax.ShapeDtypeStruct((seq, ATT_WIDTH), BF16),
        scratch_shapes=[
            pltpu.VMEM((grp, 1, tq), F32),
            pltpu.VMEM((grp, 1, tq), F32),
            pltpu.VMEM((grp, ATT_HEAD_DIM, tq), F32),
        ],
        compiler_params=_cparams(("parallel", "parallel")),
        name="grid_att",
    )(qt, kr, vt, h)


def _mem_kv_kernel(mem_ref, g_ref, w_ref, o_ref):
    x = mem_ref[...]
    ms = jnp.mean(x * x, axis=-1, keepdims=True)
    xn = (x * lax.rsqrt(ms + EPS) * g_ref[...]).astype(BF16)
    o_ref[...] = _dot(xn, w_ref[...].astype(BF16)).astype(BF16)


def _mem_kv(mem, g, w_kv):
    n_mem = mem.shape[0]
    tn = 512
    return pl.pallas_call(
        _mem_kv_kernel,
        grid=(2 * MEM_WIDTH // tn,),
        in_specs=[
            pl.BlockSpec((n_mem, D_MODEL), lambda j: (0, 0)),
            pl.BlockSpec((1, D_MODEL), lambda j: (0, 0)),
            pl.BlockSpec((D_MODEL, tn), lambda j: (0, j)),
        ],
        out_specs=pl.BlockSpec((n_mem, tn), lambda j: (0, j)),
        out_shape=jax.ShapeDtypeStruct((n_mem, 2 * MEM_WIDTH), BF16),
        compiler_params=_cparams(("parallel",)),
        name="mem_kv",
    )(mem, g, w_kv)


def _mem_att_kernel(q_ref, z_ref, kv_ref, o_ref):
    scale = MEM_HEAD_DIM ** -0.5
    for hd in range(MEM_HEADS):
        sl = slice(hd * MEM_HEAD_DIM, (hd + 1) * MEM_HEAD_DIM)
        k = kv_ref[:, sl]
        v = kv_ref[:, MEM_WIDTH + hd * MEM_HEAD_DIM:MEM_WIDTH + (hd + 1) * MEM_HEAD_DIM]
        s = _dot_nt(q_ref[:, sl], k) * scale
        m = jnp.max(s, axis=-1, keepdims=True)
        p = jnp.exp(s - m)
        l = jnp.sum(p, axis=-1, keepdims=True)
        o = _dot(p.astype(BF16), v) / l
        o_ref[:, sl] = (o * _silu(z_ref[:, sl].astype(F32))).astype(BF16)


def _mem_att(h, kv):
    seq = h.shape[0]
    tm = min(1024, seq)
    n_mem = kv.shape[0]
    return pl.pallas_call(
        _mem_att_kernel,
        grid=(seq // tm,),
        in_specs=[
            pl.BlockSpec((tm, MEM_WIDTH), lambda i: (i, OFF_MQ // MEM_WIDTH)),
            pl.BlockSpec((tm, MEM_WIDTH), lambda i: (i, OFF_ZM // MEM_WIDTH)),
            pl.BlockSpec((n_mem, 2 * MEM_WIDTH), lambda i: (0, 0)),
        ],
        out_specs=pl.BlockSpec((tm, MEM_WIDTH), lambda i: (i, 0)),
        out_shape=jax.ShapeDtypeStruct((seq, MEM_WIDTH), BF16),
        compiler_params=_cparams(("parallel",)),
        name="mem_att",
    )(h, h, kv)


def _merge_kernel(ya_ref, yb_ref, yc_ref, ym_ref, ga_ref, gb_ref, gc_ref, gm_ref,
                  wa_ref, wb_ref, wc_ref, wm_ref, wo_ref, x_ref, fg_ref, o_ref, acc_ref,
                  *, final_norm, nj):
    j = pl.program_id(1)

    @pl.when(j == 0)
    def _():
        acc_ref[...] = jnp.zeros(acc_ref.shape, F32)

    merged = jax.nn.sigmoid(ga_ref[...].astype(F32)) * _dot(ya_ref[...], wa_ref[...])
    merged = merged + jax.nn.sigmoid(gb_ref[...].astype(F32)) * _dot(yb_ref[...], wb_ref[...])
    merged = merged + jax.nn.sigmoid(gc_ref[...].astype(F32)) * _dot(yc_ref[...], wc_ref[...])
    merged = merged + jax.nn.sigmoid(gm_ref[...].astype(F32)) * _dot(ym_ref[...], wm_ref[...])
    acc_ref[...] += _dot(merged.astype(BF16), wo_ref[...])

    @pl.when(j == nj - 1)
    def _():
        y = x_ref[...] + acc_ref[...]
        if final_norm:
            ms = jnp.mean(y * y, axis=-1, keepdims=True)
            y = y * lax.rsqrt(ms + EPS) * fg_ref[...]
        o_ref[...] = y


def _merge(ys, h, wbs, w_out, x, final_g, final_norm):
    seq = x.shape[0]
    tm = min(512, seq)
    tn = 512
    nj = D_MODEL // tn
    widths = (SSM_WIDTH, DN_WIDTH, ATT_WIDTH, MEM_WIDTH)
    in_specs = [pl.BlockSpec((tm, wd), lambda i, j: (i, 0)) for wd in widths]
    for b in range(N_BRANCH):
        base = (OFF_GATE + b * D_MODEL) // tn
        in_specs.append(pl.BlockSpec((tm, tn), lambda i, j, base=base: (i, base + j)))
    in_specs += [pl.BlockSpec((wd, tn), lambda i, j: (0, j)) for wd in widths]
    in_specs += [
        pl.BlockSpec((tn, D_MODEL), lambda i, j: (j, 0)),
        pl.BlockSpec((tm, D_MODEL), lambda i, j: (i, 0)),
        pl.BlockSpec((1, D_MODEL), lambda i, j: (0, 0)),
    ]
    return pl.pallas_call(
        functools.partial(_merge_kernel, final_norm=final_norm, nj=nj),
        grid=(seq // tm, nj),
        in_specs=in_specs,
        out_specs=pl.BlockSpec((tm, D_MODEL), lambda i, j: (i, 0)),
        out_shape=jax.ShapeDtypeStruct((seq, D_MODEL), F32),
        scratch_shapes=[pltpu.VMEM((tm, D_MODEL), F32)],
        compiler_params=_cparams(("parallel", "arbitrary")),
        name="merge_out",
    )(*ys, h, h, h, h, *wbs, w_out, x, final_g)


def _deinterleave_heads(w, nheads):
    lead = w.shape[:-1]
    w = w.reshape(lead + (nheads, ATT_HEAD_DIM // 2, 2))
    w = jnp.swapaxes(w, -1, -2)
    return w.reshape(lead + (nheads * ATT_HEAD_DIM,))


def _rope_tables(seq):
    rows = seq // GRID_W
    row = jnp.repeat(jnp.arange(rows), GRID_W).astype(F32)
    col = jnp.tile(jnp.arange(GRID_W), rows).astype(F32)
    axis_dim = ATT_HEAD_DIM // 2
    freqs = ROPE_THETA ** (-jnp.arange(0, axis_dim, 2, dtype=F32) / axis_dim)
    ang = jnp.concatenate([row[:, None] * freqs, col[:, None] * freqs], axis=-1)
    cos = jnp.cos(ang)
    sin = jnp.sin(ang)
    return jnp.concatenate([cos, cos], axis=-1), jnp.concatenate([-sin, sin], axis=-1)


def _pair_rows(p):
    return p.reshape(2, SSM_PAIRS, 1, 2 * SSM_STATE).transpose(1, 0, 2, 3)


def _layer_params(layer, w_in, ssm_a_re, ssm_a_im, ssm_log_step, ssm_b_re, ssm_b_im, ssm_c_re,
                  ssm_c_im, ssm_d, ssm_w_glu, ssm_b_glu, dn_conv, dn_a_log, dn_dt_bias,
                  dn_norm_g, attn_q_norm, attn_k_norm, w_branch, w_out):
    w = w_in[layer]
    o_da = 1536 + 3 * DN_WIDTH
    o_zb = o_da + 4 * DN_HEADS
    o_aq = o_zb + DN_WIDTH
    o_ak = o_aq + ATT_WIDTH
    o_av = o_ak + ATT_KV_WIDTH
    o_zc = o_av + ATT_KV_WIDTH
    o_mq = o_zc + ATT_WIDTH
    o_zm = o_mq + MEM_WIDTH
    o_gate = o_zm + MEM_WIDTH
    w_main = jnp.concatenate([
        w[:, :o_da],
        w[:, o_zb:o_aq],
        w[:, o_mq:o_gate],
        _deinterleave_heads(w[:, o_ak:o_av], ATT_KV_HEADS),
        w[:, o_av:o_zc],
        _deinterleave_heads(w[:, o_aq:o_ak], ATT_HEADS),
        w[:, o_zc:o_mq],
        w[:, o_gate:],
    ], axis=1).astype(BF16)
    w_ab = jnp.pad(w[:, o_da:o_zb], ((0, 0), (0, AB_PAD - 4 * DN_HEADS))).astype(BF16)

    ls_n = jnp.broadcast_to(ssm_log_step[layer][:, :, None], ssm_a_re[layer].shape)
    flat = lambda p: p.reshape(2, 1, SSM_GROUPS * SSM_STATE)

    def b_tiles(b):
        bt = b.transpose(0, 1, 3, 2).reshape(2, SSM_PAIRS, 2, SSM_GROUP, SSM_STATE)
        bt = bt.transpose(1, 0, 3, 2, 4).reshape(SSM_PAIRS, 2, SSM_GROUP, 2 * SSM_STATE)
        return jnp.tile(bt, (1, 1, SSM_T, 1))

    def c_tiles(c):
        ct = c.transpose(0, 1, 3, 2).reshape(2, SSM_PAIRS, 2 * SSM_STATE, SSM_GROUP)
        ct = ct.transpose(1, 0, 2, 3)
        return jnp.tile(ct, (1, 1, 1, SSM_T))

    conv = dn_conv[layer].T.reshape(DN_CONV, 3, DN_WIDTH).transpose(1, 0, 2)
    conv = jnp.pad(conv, ((0, 0), (0, 8 - DN_CONV), (0, 0)))

    def lane_vec(p):
        return jnp.pad(p.reshape(1, -1), ((0, 0), (0, AB_PAD - p.size)))

    wb = w_branch[layer].astype(BF16)
    offs = np.cumsum((0, SSM_WIDTH, DN_WIDTH, ATT_WIDTH, MEM_WIDTH))
    return {
        "w_main": w_main, "w_ab": w_ab,
        "ssm_rows": [_pair_rows(ssm_a_re[layer]), _pair_rows(ssm_a_im[layer]), _pair_rows(ls_n)],
        "ssm_b": [b_tiles(ssm_b_re[layer]), b_tiles(ssm_b_im[layer])],
        "ssm_c": [c_tiles(ssm_c_re[layer]), c_tiles(ssm_c_im[layer])],
        "ssm_flat": [flat(ssm_a_re[layer]), flat(ssm_a_im[layer]), flat(ls_n)],
        "ssm_d": ssm_d[layer].reshape(1, SSM_WIDTH),
        "ssm_w_glu": ssm_w_glu[layer],
        "ssm_b_glu": ssm_b_glu[layer].reshape(1, SSM_WIDTH),
        "dn_conv": conv,
        "dn_alog": lane_vec(dn_a_log[layer]),
        "dn_dtb": lane_vec(dn_dt_bias[layer]),
        "dn_norm_g": dn_norm_g[layer].reshape(1, DN_HEAD_DIM),
        "att_gq": _deinterleave_heads(attn_q_norm[layer].reshape(1, ATT_HEAD_DIM), 1),
        "att_gk": _deinterleave_heads(attn_k_norm[layer].reshape(1, ATT_HEAD_DIM), 1),
        "w_branch": [wb[int(offs[b]):int(offs[b + 1])] for b in range(N_BRANCH)],
        "w_out": w_out[layer].astype(BF16),
    }


def kernel(x, mem, norm_g, w_in, ssm_a_re, ssm_a_im, ssm_log_step, ssm_b_re, ssm_b_im, ssm_c_re, ssm_c_im, ssm_d, ssm_w_glu, ssm_b_glu, dn_conv, dn_a_log, dn_dt_bias, dn_norm_g, attn_q_norm, attn_k_norm, mem_norm_g, w_mem_kv, w_branch, w_out, final_norm_g):
    bsz, seq, _ = x.shape
    depth = w_in.shape[0]
    cos2, sin2 = _rope_tables(seq)
    final_g = final_norm_g.reshape(1, D_MODEL)
    outs = []
    for b in range(bsz):
        xb = x[b]
        for layer in range(depth):
            lp = _layer_params(layer, w_in, ssm_a_re, ssm_a_im, ssm_log_step, ssm_b_re, ssm_b_im,
                               ssm_c_re, ssm_c_im, ssm_d, ssm_w_glu, ssm_b_glu, dn_conv, dn_a_log,
                               dn_dt_bias, dn_norm_g, attn_q_norm, attn_k_norm, w_branch, w_out)
            h, ab = _inproj(xb, norm_g[layer].reshape(1, D_MODEL), lp["w_main"], lp["w_ab"])
            y_a = _ssm_branch(h, lp)
            y_b = _dn_branch(h, ab, lp)
            qt, kr, vt = _att_prep(h, cos2, sin2, lp["att_gq"], lp["att_gk"])
            y_c = _att(qt, kr, vt, h)
            kv = _mem_kv(mem[b], mem_norm_g[layer].reshape(1, D_MODEL), w_mem_kv[layer])
            y_m = _mem_att(h, kv)
            xb = _merge((y_a, y_b, y_c, y_m), h, lp["w_branch"], lp["w_out"], xb, final_g,
                        final_norm=(layer == depth - 1))
        outs.append(xb)
    return outs[0][None] if bsz == 1 else jnp.stack(outs, axis=0)
```

```python
import functools
import math

import numpy as np
import jax
import jax.numpy as jnp
from jax import lax
from jax.experimental import pallas as pl
from jax.experimental.pallas import tpu as pltpu

F32 = jnp.float32
BF16 = jnp.bfloat16

D_MODEL = 2048
GRID_W = 64
EPS = 1e-6

SSM_GROUP = 16
SSM_STATE = 64
SSM_GROUPS = 48
SSM_WIDTH = SSM_GROUPS * SSM_GROUP
SSM_T = 16
SSM_PAIRS = SSM_GROUPS // 2
SSM_CW = SSM_T * SSM_GROUP

DN_HEADS = 6
DN_HEAD_DIM = 128
DN_WIDTH = DN_HEADS * DN_HEAD_DIM
DN_CONV = 5
DN_CHUNK = 64
DN_GROUP = 256
DN_HB = 6

ATT_HEADS = 8
ATT_KV_HEADS = 2
ATT_HEAD_DIM = 128
ATT_WIDTH = ATT_HEADS * ATT_HEAD_DIM
ATT_KV_WIDTH = ATT_KV_HEADS * ATT_HEAD_DIM
ROPE_THETA = 10000.0

MEM_HEADS = 4
MEM_HEAD_DIM = 128
MEM_WIDTH = MEM_HEADS * MEM_HEAD_DIM

N_BRANCH = 4

OFF_UA = 0
OFF_ZA = 768
OFF_DQ = 1536
OFF_DK = 2304
OFF_DV = 3072
OFF_ZB = 3840
OFF_AQ = 4608
OFF_AK = 5632
OFF_AV = 5888
OFF_ZC = 6144
OFF_MQ = 7168
OFF_ZM = 7680
OFF_GATE = 8192
IN_AB = 3840
IN_AB_END = IN_AB + 4 * DN_HEADS
H_WIDTH = OFF_GATE + N_BRANCH * D_MODEL
AB_PAD = 128

VMEM_LIMIT = 56 * 1024 * 1024


def _cparams(sem):
    return pltpu.CompilerParams(dimension_semantics=sem, vmem_limit_bytes=VMEM_LIMIT)


def _silu(x):
    return x * jax.nn.sigmoid(x)


def _dot(a, b):
    return jnp.dot(a, b, preferred_element_type=F32)


def _dot_nt(a, b):
    return lax.dot_general(a, b, (((1,), (1,)), ((), ())), preferred_element_type=F32)


def _dot_tn(a, b):
    return lax.dot_general(a, b, (((0,), (0,)), ((), ())), preferred_element_type=F32)


def _split3(x):
    x1 = x.astype(BF16)
    r1 = x - x1.astype(F32)
    x2 = r1.astype(BF16)
    x3 = (r1 - x2.astype(F32)).astype(BF16)
    return x1, x2, x3


def _dot_exact_lhs(a_bf16, x):
    x1, x2, x3 = _split3(x)
    return _dot(a_bf16, x1) + _dot(a_bf16, x2) + _dot(a_bf16, x3)


def _dot_exact_rhs(x, b_bf16):
    x1, x2, x3 = _split3(x)
    return _dot(x1, b_bf16) + _dot(x2, b_bf16) + _dot(x3, b_bf16)


def _dot_f32(a, b):
    a1, a2, a3 = _split3(a)
    b1, b2, b3 = _split3(b)
    return (_dot(a1, b1) + (_dot(a1, b2) + _dot(a2, b1))
            + (_dot(a1, b3) + _dot(a2, b2) + _dot(a3, b1)))


def _inproj_kernel(x_ref, g_ref, w_ref, wab_ref, h_ref, ab_ref, xn_ref):
    @pl.when(pl.program_id(1) == 0)
    def _():
        x = x_ref[...]
        ms = jnp.mean(x * x, axis=-1, keepdims=True)
        xn_ref[...] = (x * lax.rsqrt(ms + EPS) * g_ref[...]).astype(BF16)
        ab_ref[...] = _dot(xn_ref[...], wab_ref[...])

    h_ref[...] = _dot(xn_ref[...], w_ref[...]).astype(BF16)


def _inproj(x, g, w_main, w_ab, layer):
    seq = x.shape[0]
    tm = min(1024, seq)
    tn = 2048
    return pl.pallas_call(
        _inproj_kernel,
        grid=(seq // tm, H_WIDTH // tn),
        in_specs=[
            pl.BlockSpec((tm, D_MODEL), lambda i, j: (i, 0)),
            pl.BlockSpec((1, D_MODEL), lambda i, j: (0, 0)),
            pl.BlockSpec((None, D_MODEL, tn), lambda i, j: (layer, 0, j)),
            pl.BlockSpec((None, D_MODEL, AB_PAD), lambda i, j: (layer, 0, 0)),
        ],
        out_specs=[
            pl.BlockSpec((tm, tn), lambda i, j: (i, j)),
            pl.BlockSpec((tm, AB_PAD), lambda i, j: (i, 0)),
        ],
        out_shape=[
            jax.ShapeDtypeStruct((seq, H_WIDTH), BF16),
            jax.ShapeDtypeStruct((seq, AB_PAD), F32),
        ],
        scratch_shapes=[pltpu.VMEM((tm, D_MODEL), BF16)],
        compiler_params=_cparams(("parallel", "arbitrary")),
        name="inproj",
    )(x, g, w_main, w_ab)


def _cpow_table(base_re, base_im, n):
    re = [jnp.ones_like(base_re)]
    im = [jnp.zeros_like(base_im)]
    for _ in range(n):
        re_n = re[-1] * base_re - im[-1] * base_im
        im_n = re[-1] * base_im + im[-1] * base_re
        re.append(re_n)
        im.append(im_n)
    return re, im


def _ssm_prep_kernel(are_r, aim_r, ls_r, bre_ref, bim_ref, cre_ref, cim_ref,
                     win_ref, wout_ref, m_ref):
    t = SSM_T
    cw = SSM_CW
    lane128 = lax.broadcasted_iota(jnp.int32, (1, 128), 1)
    col_t = lax.broadcasted_iota(jnp.int32, (1, cw), 1) // SSM_GROUP
    sub128 = lax.broadcasted_iota(jnp.int32, (128, 1), 0)
    lane_cw = lax.broadcasted_iota(jnp.int32, (SSM_GROUP, cw), 1)
    tile_p = jnp.where(lane_cw % SSM_GROUP == lax.broadcasted_iota(jnp.int32, (SSM_GROUP, cw), 0),
                       1.0, 0.0).astype(BF16)

    m_ref[...] = jnp.zeros(m_ref.shape, m_ref.dtype)

    krows = [[None, None], [None, None]]
    for d in range(2):
        step = jnp.exp(ls_r[d])
        a_re = are_r[d]
        a_im = aim_r[d]
        mag = jnp.exp(a_re * step)
        lam_re = mag * jnp.cos(a_im * step)
        lam_im = mag * jnp.sin(a_im * step)
        den = a_re * a_re + a_im * a_im
        nr = lam_re - 1.0
        ni = lam_im
        coef_re = (nr * a_re + ni * a_im) / den
        coef_im = (ni * a_re - nr * a_im) / den
        b_re = bre_ref[d]
        b_im = bim_ref[d]
        bbt_re = coef_re * b_re - coef_im * b_im
        bbt_im = coef_re * b_im + coef_im * b_re
        bb_re = jnp.concatenate([bbt_re] * t, axis=0)
        bb_im = jnp.concatenate([bbt_im] * t, axis=0)
        pr_re, pr_im = _cpow_table(lam_re, lam_im, t)
        e_in = [(t - 1 - s) if d == 0 else s for s in range(t)]
        p_re = jnp.concatenate([jnp.broadcast_to(pr_re[e], (SSM_GROUP, 128)) for e in e_in], axis=0)
        p_im = jnp.concatenate([jnp.broadcast_to(pr_im[e], (SSM_GROUP, 128)) for e in e_in], axis=0)
        w_re = p_re * bb_re - p_im * bb_im
        w_im = p_re * bb_im + p_im * bb_re
        for par in range(2):
            keep = (lane128 < 64) if par == 0 else (lane128 >= 64)
            win_ref[par * cw:(par + 1) * cw, (2 * d) * 128:(2 * d + 1) * 128] = (
                jnp.where(keep, w_re, 0.0).astype(BF16))
            win_ref[par * cw:(par + 1) * cw, (2 * d + 1) * 128:(2 * d + 2) * 128] = (
                jnp.where(keep, w_im, 0.0).astype(BF16))

        tab_re = jnp.zeros((128, 128), F32)
        tab_im = jnp.zeros((128, 128), F32)
        for e in range(t + 1):
            tab_re = jnp.where(sub128 == e, pr_re[e], tab_re)
            tab_im = jnp.where(sub128 == e, pr_im[e], tab_im)
        tab_re = tab_re.T
        tab_im = tab_im.T
        c_re = _dot_exact_rhs(cre_ref[d], tile_p)
        c_im = _dot_exact_rhs(cim_ref[d], tile_p)

        def c_lam(expo_row):
            sel = jnp.where(sub128 == expo_row, 1.0, 0.0).astype(BF16)
            q_re = _dot_exact_rhs(tab_re, sel)
            q_im = _dot_exact_rhs(tab_im, sel)
            return c_re * q_re - c_im * q_im, -(c_re * q_im + c_im * q_re)

        o_re, o_im = c_lam((col_t + 1) if d == 0 else (t - col_t))
        for par in range(2):
            keep = (sub128 < 64) if par == 0 else (sub128 >= 64)
            wout_ref[(2 * d) * 128:(2 * d + 1) * 128, par * cw:(par + 1) * cw] = (
                jnp.where(keep, o_re, 0.0).astype(BF16))
            wout_ref[(2 * d + 1) * 128:(2 * d + 2) * 128, par * cw:(par + 1) * cw] = (
                jnp.where(keep, o_im, 0.0).astype(BF16))

        r_re, r_im = c_lam(col_t if d == 0 else (t - 1 - col_t))
        for par in range(2):
            keep = (lane128 < 64) if par == 0 else (lane128 >= 64)
            krows[d][par] = (_dot_f32(jnp.where(keep, bbt_re, 0.0), r_re)
                             + _dot_f32(jnp.where(keep, bbt_im, 0.0), r_im))

    for par in range(2):
        kf = krows[0][par]
        kb = krows[1][par]
        for s in range(t):
            sh_f = SSM_GROUP * s
            blk = jnp.where(lane_cw >= sh_f, pltpu.roll(kf, sh_f, axis=1) if sh_f else kf, 0.0)
            sh_b = SSM_GROUP * (t - 1 - s)
            rolled_b = pltpu.roll(kb, cw - sh_b, axis=1) if sh_b else kb
            blk = blk + jnp.where(lane_cw < cw - sh_b, rolled_b, 0.0)
            m_ref[par * cw + s * SSM_GROUP:par * cw + (s + 1) * SSM_GROUP,
                  par * cw:(par + 1) * cw] = blk.astype(BF16)


def _ssm_prep(rows, b_t, c_t):
    pw = 2 * SSM_CW
    row_spec = pl.BlockSpec((None, 2, 1, 128), lambda g: (g, 0, 0, 0))
    b_spec = pl.BlockSpec((None, 2, SSM_GROUP, 128), lambda g: (g, 0, 0, 0))
    c_spec = pl.BlockSpec((None, 2, 128, SSM_GROUP), lambda g: (g, 0, 0, 0))
    w_spec = pl.BlockSpec((None, pw, pw), lambda g: (g, 0, 0))
    return pl.pallas_call(
        _ssm_prep_kernel,
        grid=(SSM_PAIRS,),
        in_specs=[row_spec] * 3 + [b_spec] * 2 + [c_spec] * 2,
        out_specs=[w_spec] * 3,
        out_shape=[jax.ShapeDtypeStruct((SSM_PAIRS, pw, pw), BF16)] * 3,
        compiler_params=_cparams(("parallel",)),
        name="ssm_prep",
    )(*rows, *b_t, *c_t)


SSM_BG = 8
SSM_BP = SSM_BG // 2
SSM_BW = SSM_BG * SSM_CW


def _ssm_place(t0):
    r = lax.broadcasted_iota(jnp.int32, (256, SSM_BW), 0)
    c = lax.broadcasted_iota(jnp.int32, (256, SSM_BW), 1)
    j = r % 128
    target = (j // SSM_GROUP) * SSM_CW + (t0 + r // 128) * SSM_GROUP + j % SSM_GROUP
    return jnp.where(c == target, 1.0, 0.0).astype(BF16)


def _ssm_in_kernel(u_ref, win_ref, u2_ref, h_ref, uf_ref):
    nchunk = u2_ref.shape[0]
    uf_ref[...] = u_ref[...].astype(F32)
    acc = jnp.zeros((nchunk, SSM_BW), F32)
    for t0 in range(0, SSM_T, 2):
        lhs = jnp.concatenate([uf_ref[pl.ds(t0, nchunk, stride=SSM_T), :],
                               uf_ref[pl.ds(t0 + 1, nchunk, stride=SSM_T), :]], axis=1)
        acc = acc + _dot(lhs.astype(BF16), _ssm_place(t0))
    u2_ref[...] = acc.astype(BF16)
    pw = 2 * SSM_CW
    for j in range(SSM_BP):
        h = _dot(u2_ref[:, j * pw:(j + 1) * pw], win_ref[j])
        for k in range(4):
            h_ref[k, :, j * 128:(j + 1) * 128] = h[:, k * 128:(k + 1) * 128]


def _ssm_in(h, win):
    seq = h.shape[0]
    nchunk = seq // SSM_T
    pw = 2 * SSM_CW
    nblk = SSM_GROUPS // SSM_BG
    return pl.pallas_call(
        _ssm_in_kernel,
        grid=(nblk,),
        in_specs=[
            pl.BlockSpec((seq, 128), lambda b: (0, OFF_UA // 128 + b)),
            pl.BlockSpec((SSM_BP, pw, pw), lambda b: (b, 0, 0)),
        ],
        out_specs=[
            pl.BlockSpec((nchunk, SSM_BW), lambda b: (0, b)),
            pl.BlockSpec((4, nchunk, SSM_BP * 128), lambda b: (0, 0, b)),
        ],
        out_shape=[
            jax.ShapeDtypeStruct((nchunk, SSM_GROUPS * SSM_CW), BF16),
            jax.ShapeDtypeStruct((4, nchunk, SSM_PAIRS * 128), F32),
        ],
        scratch_shapes=[pltpu.VMEM((seq, 128), F32)],
        compiler_params=_cparams(("parallel",)),
        name="ssm_in",
    )(h, win)


def _ssm_scan_kernel(are_ref, aim_ref, ls_ref, h_ref, p_ref, *, nchunk):
    width = h_ref.shape[2]

    def lam_pow_t(d):
        step = jnp.exp(ls_ref[d]) * float(SSM_T)
        mag = jnp.exp(are_ref[d] * step)
        ang = aim_ref[d] * step
        return mag * jnp.cos(ang), mag * jnp.sin(ang)

    lfr, lfi = lam_pow_t(0)
    lbr, lbi = lam_pow_t(1)

    def body(c, carry):
        fr, fi, br, bi = carry
        cb = nchunk - 1 - c
        p_ref[0, pl.ds(c, 1), :] = fr
        p_ref[1, pl.ds(c, 1), :] = fi
        p_ref[2, pl.ds(cb, 1), :] = br
        p_ref[3, pl.ds(cb, 1), :] = bi
        hfr = h_ref[0, pl.ds(c, 1), :]
        hfi = h_ref[1, pl.ds(c, 1), :]
        hbr = h_ref[2, pl.ds(cb, 1), :]
        hbi = h_ref[3, pl.ds(cb, 1), :]
        nfr = lfr * fr - lfi * fi + hfr
        nfi = lfr * fi + lfi * fr + hfi
        nbr = lbr * br - lbi * bi + hbr
        nbi = lbr * bi + lbi * br + hbi
        return nfr, nfi, nbr, nbi

    z = jnp.zeros((1, width), F32)
    lax.fori_loop(0, nchunk, body, (z, z, z, z))


def _ssm_scan(flat_params, h):
    nchunk = h.shape[1]
    width = h.shape[2]
    wt = 768
    return pl.pallas_call(
        functools.partial(_ssm_scan_kernel, nchunk=nchunk),
        grid=(width // wt,),
        in_specs=[pl.BlockSpec((2, 1, wt), lambda j: (0, 0, j))] * 3 + [
            pl.BlockSpec((4, nchunk, wt), lambda j: (0, 0, j)),
        ],
        out_specs=pl.BlockSpec((4, nchunk, wt), lambda j: (0, 0, j)),
        out_shape=jax.ShapeDtypeStruct(h.shape, F32),
        compiler_params=_cparams(("parallel",)),
        name="ssm_scan",
    )(*flat_params, h)


def _ssm_out_kernel(u2_ref, p_ref, m_ref, wout_ref, y_ref):
    nchunk = u2_ref.shape[0]
    pw = 2 * SSM_CW
    hi = []
    lo = []
    for j in range(SSM_BP):
        acc = _dot(u2_ref[:, j * pw:(j + 1) * pw], m_ref[j])
        for k in range(4):
            acc = acc + _dot(p_ref[k, :, j * 128:(j + 1) * 128].astype(BF16),
                             wout_ref[j, k * 128:(k + 1) * 128, :])
        acc_hi = acc.astype(BF16)
        hi.append(acc_hi)
        lo.append((acc - acc_hi.astype(F32)).astype(BF16))
    y_hi = jnp.concatenate(hi, axis=1)
    y_lo = jnp.concatenate(lo, axis=1)
    for t0 in range(0, SSM_T, 2):
        place = _ssm_place(t0)
        yt = _dot_nt(y_hi, place) + _dot_nt(y_lo, place)
        y_ref[pl.ds(t0, nchunk, stride=SSM_T), :] = yt[:, :128]
        y_ref[pl.ds(t0 + 1, nchunk, stride=SSM_T), :] = yt[:, 128:]


def _ssm_out(u2, p, m, wout):
    nchunk = u2.shape[0]
    seq = nchunk * SSM_T
    pw = 2 * SSM_CW
    nblk = SSM_GROUPS // SSM_BG
    return pl.pallas_call(
        _ssm_out_kernel,
        grid=(nblk,),
        in_specs=[
            pl.BlockSpec((nchunk, SSM_BW), lambda b: (0, b)),
            pl.BlockSpec((4, nchunk, SSM_BP * 128), lambda b: (0, 0, b)),
            pl.BlockSpec((SSM_BP, pw, pw), lambda b: (b, 0, 0)),
            pl.BlockSpec((SSM_BP, pw, pw), lambda b: (b, 0, 0)),
        ],
        out_specs=pl.BlockSpec((seq, 128), lambda b: (0, b)),
        out_shape=jax.ShapeDtypeStruct((seq, SSM_WIDTH), F32),
        compiler_params=_cparams(("parallel",)),
        name="ssm_out",
    )(u2, p, m, wout)


def _ssm_epi_kernel(y_ref, u_ref, z_ref, d_ref, wg_ref, bg_ref, o_ref):
    y = y_ref[...] + d_ref[...] * u_ref[...].astype(F32)
    y = jax.nn.gelu(y)
    glu = _dot(y.astype(BF16), wg_ref[...].astype(BF16)) + bg_ref[...]
    y = y * jax.nn.sigmoid(glu)
    o_ref[...] = (y * _silu(z_ref[...].astype(F32))).astype(BF16)


def _ssm_epi(y, h, d, w_glu, b_glu):
    seq = y.shape[0]
    tm = min(512, seq)
    w = SSM_WIDTH
    return pl.pallas_call(
        _ssm_epi_kernel,
        grid=(seq // tm,),
        in_specs=[
            pl.BlockSpec((tm, w), lambda i: (i, 0)),
            pl.BlockSpec((tm, w), lambda i: (i, OFF_UA // w)),
            pl.BlockSpec((tm, w), lambda i: (i, OFF_ZA // w)),
            pl.BlockSpec((1, w), lambda i: (0, 0)),
            pl.BlockSpec((w, w), lambda i: (0, 0)),
            pl.BlockSpec((1, w), lambda i: (0, 0)),
        ],
        out_specs=pl.BlockSpec((tm, w), lambda i: (i, 0)),
        out_shape=jax.ShapeDtypeStruct((seq, w), BF16),
        compiler_params=_cparams(("parallel",)),
        name="ssm_epi",
    )(y, h, h, d, w_glu, b_glu)


def _ssm_branch(h, lp):
    win, wout, m = _ssm_prep(lp["ssm_rows"], lp["ssm_b"], lp["ssm_c"])
    u2, hs = _ssm_in(h, win)
    p = _ssm_scan(lp["ssm_flat"], hs)
    y = _ssm_out(u2, p, m, wout)
    return _ssm_epi(y, h, lp["ssm_d"], lp["ssm_w_glu"], lp["ssm_b_glu"])


def _dn_prep_kernel(qc, qp, qn, kc, kp, kn, vc, vp, vn, ab_ref, cw_ref, alog_ref, dtb_ref,
                    qo_ref, ko_ref, vo_ref, sc_ref, sct_ref, ext_ref, *, tm, nblk):
    i = pl.program_id(0)
    halo = 16
    pad = DN_CONV // 2

    def conv_silu(cur, prev, nxt, part):
        ext_ref[0:halo, :] = jnp.where(i > 0, prev[...].astype(F32), 0.0)
        ext_ref[halo:halo + tm, :] = cur[...].astype(F32)
        ext_ref[halo + tm:halo + tm + halo, :] = jnp.where(i < nblk - 1, nxt[...].astype(F32), 0.0)
        acc = jnp.zeros((tm, DN_WIDTH), F32)
        for j in range(DN_CONV):
            acc = acc + cw_ref[part, j:j + 1, :] * ext_ref[pl.ds(halo - pad + j, tm), :]
        return _silu(acc)

    def l2n(x):
        outs = []
        for hd in range(DN_HEADS):
            xh = x[:, hd * DN_HEAD_DIM:(hd + 1) * DN_HEAD_DIM]
            outs.append(xh * lax.rsqrt(jnp.sum(xh * xh, axis=-1, keepdims=True) + EPS))
        return jnp.concatenate(outs, axis=1)

    qo_ref[...] = (l2n(conv_silu(qc, qp, qn, 0)) * (DN_HEAD_DIM ** -0.5)).astype(BF16)
    ko_ref[...] = l2n(conv_silu(kc, kp, kn, 1)).astype(BF16)
    vo_ref[...] = conv_silu(vc, vp, vn, 2).astype(BF16)

    ab = ab_ref[...]
    g_all = -jnp.exp(alog_ref[...]) * jax.nn.softplus(ab + dtb_ref[...])
    beta_all = jax.nn.sigmoid(ab)
    r = lax.broadcasted_iota(jnp.int32, (tm, tm), 0)
    c = lax.broadcasted_iota(jnp.int32, (tm, tm), 1)
    same = (r // DN_CHUNK) == (c // DN_CHUNK)
    tri_f = jnp.where(same & (c <= r), 1.0, 0.0).astype(BF16)
    tri_b = jnp.where(same & (c >= r), 1.0, 0.0).astype(BF16)
    blk = jnp.where(same, 1.0, 0.0).astype(BF16)
    gcf = _dot_exact_lhs(tri_f, g_all)
    gcb = _dot_exact_lhs(tri_b, g_all)
    gtot = _dot_exact_lhs(blk, g_all)
    lane = lax.broadcasted_iota(jnp.int32, (tm, AB_PAD), 1)
    sc = jnp.where(lane < 6, gcf,
                   jnp.where(lane < 12, gcb,
                             jnp.where(lane < 24, beta_all,
                                       jnp.where(lane < 36, pltpu.roll(gtot, 24, axis=1), 0.0))))
    sc_ref[...] = sc
    sct_ref[...] = sc.T


def _dn_prep(h, ab, conv_w, alog, dtb):
    seq = h.shape[0]
    tm = min(256, seq)
    nblk = seq // tm
    w = DN_WIDTH
    hb = tm // 16
    nh = seq // 16

    def cur(ci):
        return pl.BlockSpec((tm, w), lambda i: (i, ci))

    def prev(ci):
        return pl.BlockSpec((16, w), lambda i: (jnp.maximum(i * hb - 1, 0), ci))

    def nxt(ci):
        return pl.BlockSpec((16, w), lambda i: (jnp.minimum((i + 1) * hb, nh - 1), ci))

    in_specs = []
    for off in (OFF_DQ, OFF_DK, OFF_DV):
        ci = off // w
        in_specs += [cur(ci), prev(ci), nxt(ci)]
    in_specs += [
        pl.BlockSpec((tm, AB_PAD), lambda i: (i, 0)),
        pl.BlockSpec((3, 8, w), lambda i: (0, 0, 0)),
        pl.BlockSpec((1, AB_PAD), lambda i: (0, 0)),
        pl.BlockSpec((1, AB_PAD), lambda i: (0, 0)),
    ]
    return pl.pallas_call(
        functools.partial(_dn_prep_kernel, tm=tm, nblk=nblk),
        grid=(nblk,),
        in_specs=in_specs,
        out_specs=[pl.BlockSpec((tm, w), lambda i: (i, 0))] * 3
        + [pl.BlockSpec((tm, AB_PAD), lambda i: (i, 0)), pl.BlockSpec((AB_PAD, tm), lambda i: (0, i))],
        out_shape=[jax.ShapeDtypeStruct((seq, w), BF16)] * 3
        + [jax.ShapeDtypeStruct((seq, AB_PAD), F32), jax.ShapeDtypeStruct((AB_PAD, seq), F32)],
        scratch_shapes=[pltpu.VMEM((tm + 32, w), F32)],
        compiler_params=_cparams(("parallel",)),
        name="dn_prep",
    )(h, h, h, h, h, h, h, h, h, ab, conv_w, alog, dtb)


def _dn_main_kernel(qf_ref, kf_ref, vf_ref, scf_ref, sctf_ref,
                    qb_ref, kb_ref, vb_ref, scb_ref, sctb_ref, of_ref, ob_ref, s_ref):
    gsz = DN_GROUP
    nck = gsz // DN_CHUNK

    @pl.when(pl.program_id(1) == 0)
    def _():
        s_ref[...] = jnp.zeros(s_ref.shape, F32)

    refs = ((qf_ref, kf_ref, vf_ref, scf_ref, sctf_ref, of_ref),
            (qb_ref, kb_ref, vb_ref, scb_ref, sctb_ref, ob_ref))
    r = lax.broadcasted_iota(jnp.int32, (gsz, gsz), 0)
    c = lax.broadcasted_iota(jnp.int32, (gsz, gsz), 1)
    same = (r // DN_CHUNK) == (c // DN_CHUNK)
    incl = (same & (r >= c), same & (r <= c))
    strict = (same & (r > c), same & (r < c))

    chains = []
    for d in range(2):
        q_ref, k_ref, v_ref, sc_ref, sct_ref, o_ref = refs[d]
        sc = sc_ref[...]
        sct = sct_ref[...]
        lane = lax.broadcasted_iota(jnp.int32, sc.shape, 1)
        sub = lax.broadcasted_iota(jnp.int32, sct.shape, 0)
        for hl in range(DN_HB):
            cidx = d * DN_HEADS + pl.program_id(0) * DN_HB + hl
            hs = slice(hl * DN_HEAD_DIM, (hl + 1) * DN_HEAD_DIM)

            def col(ci, sc=sc, lane=lane):
                return jnp.sum(jnp.where(lane == ci, sc, 0.0), axis=1, keepdims=True)

            ch = {"d": d, "hl": hl, "hs": hs, "o_ref": o_ref}
            ch["q"] = q_ref[:, hs]
            ch["k"] = k_ref[:, hs]
            ch["v"] = v_ref[:, hs]
            ch["gc_col"] = col(cidx)
            ch["gc_row"] = jnp.sum(jnp.where(sub == cidx, sct, 0.0), axis=0, keepdims=True)
            ch["beta_col"] = col(12 + cidx)
            ch["gtot_col"] = col(24 + cidx)
            chains.append(ch)

    for ch in chains:
        d = ch["d"]
        ch["decay"] = jnp.where(incl[d], jnp.exp(jnp.where(incl[d], ch["gc_col"] - ch["gc_row"], 0.0)), 0.0)
        kf = ch["k"].astype(F32)
        kb = kf * ch["beta_col"]
        ch["kb16"] = kb.astype(BF16)
        ch["x"] = jnp.concatenate([ch["v"].astype(F32) * ch["beta_col"], kb * jnp.exp(ch["gc_col"])], axis=1)
        ch["qd16"] = (ch["q"].astype(F32) * jnp.exp(ch["gc_col"])).astype(BF16)
        ch["kd16"] = (kf * jnp.exp(ch["gtot_col"] - ch["gc_col"])).astype(BF16)
    for ch in chains:
        ch["p"] = jnp.where(strict[ch["d"]], _dot_nt(ch["kb16"], ch["k"]) * ch["decay"], 0.0).astype(BF16)
    for ch in chains:
        ch["intra"] = (_dot_nt(ch["q"], ch["k"]) * ch["decay"]).astype(BF16)
    for ch in chains:
        ch["x"] = ch["x"] - _dot(ch["p"], ch["x"].astype(BF16))
    npow = 2
    while npow < DN_CHUNK:
        for ch in chains:
            ch["p"] = _dot(ch["p"], ch["p"]).astype(BF16)
        for ch in chains:
            ch["x"] = ch["x"] + _dot(ch["p"], ch["x"].astype(BF16))
        npow *= 2
    for ch in chains:
        ch["u"] = ch["x"][:, :DN_HEAD_DIM]
        ch["w16"] = ch["x"][:, DN_HEAD_DIM:].astype(BF16)
        ch["s"] = s_ref[ch["d"], ch["hl"]]
        ch["vnew"] = [None] * nck
        ch["oq"] = [None] * nck

    for step in range(nck):
        for ch in chains:
            j = step if ch["d"] == 0 else nck - 1 - step
            lo, hi = j * DN_CHUNK, (j + 1) * DN_CHUNK
            s16 = ch["s"].astype(BF16)
            ws = _dot(jnp.concatenate([ch["w16"][lo:hi], ch["qd16"][lo:hi]], axis=0), s16)
            vn = ch["u"][lo:hi] - ws[:DN_CHUNK]
            ch["oq"][j] = ws[DN_CHUNK:]
            ch["vnew"][j] = vn
            ch["s"] = (ch["s"] * jnp.exp(ch["gtot_col"][lo:lo + 1, :])
                       + _dot_tn(ch["kd16"][lo:hi], vn.astype(BF16)))
    for ch in chains:
        s_ref[ch["d"], ch["hl"]] = ch["s"]
        vn_all = jnp.concatenate(ch["vnew"], axis=0).astype(BF16)
        ch["o_ref"][:, ch["hs"]] = jnp.concatenate(ch["oq"], axis=0) + _dot(ch["intra"], vn_all)


def _dn_main(qn, kn, vc, sc, sct):
    seq = qn.shape[0]
    gsz = DN_GROUP
    ng = seq // gsz
    hw = DN_HB * DN_HEAD_DIM
    sct_rows = sct.shape[0]
    fwd = lambda h, i: (i, h)
    bwd = lambda h, i: (ng - 1 - i, h)
    in_specs = [
        pl.BlockSpec((gsz, hw), fwd), pl.BlockSpec((gsz, hw), fwd), pl.BlockSpec((gsz, hw), fwd),
        pl.BlockSpec((gsz, AB_PAD), lambda h, i: (i, 0)),
        pl.BlockSpec((sct_rows, gsz), lambda h, i: (0, i)),
        pl.BlockSpec((gsz, hw), bwd), pl.BlockSpec((gsz, hw), bwd), pl.BlockSpec((gsz, hw), bwd),
        pl.BlockSpec((gsz, AB_PAD), lambda h, i: (ng - 1 - i, 0)),
        pl.BlockSpec((sct_rows, gsz), lambda h, i: (0, ng - 1 - i)),
    ]
    return pl.pallas_call(
        _dn_main_kernel,
        grid=(DN_HEADS // DN_HB, ng),
        in_specs=in_specs,
        out_specs=[pl.BlockSpec((gsz, hw), fwd), pl.BlockSpec((gsz, hw), bwd)],
        out_shape=[jax.ShapeDtypeStruct((seq, DN_WIDTH), F32)] * 2,
        scratch_shapes=[pltpu.VMEM((2, DN_HB, DN_HEAD_DIM, DN_HEAD_DIM), F32)],
        compiler_params=_cparams(("parallel", "arbitrary")),
        name="dn_main",
    )(qn, kn, vc, sc, sct, qn, kn, vc, sc, sct)


def _dn_epi_kernel(of_ref, ob_ref, z_ref, g_ref, o_ref):
    o = of_ref[...] + ob_ref[...]
    g = g_ref[...]
    outs = []
    for hd in range(DN_HEADS):
        oh = o[:, hd * DN_HEAD_DIM:(hd + 1) * DN_HEAD_DIM]
        ms = jnp.mean(oh * oh, axis=-1, keepdims=True)
        outs.append(oh * lax.rsqrt(ms + EPS) * g)
    y = jnp.concatenate(outs, axis=1)
    o_ref[...] = (y * _silu(z_ref[...].astype(F32))).astype(BF16)


def _dn_epi(o_f, o_b, h, norm_g):
    seq = o_f.shape[0]
    tm = min(512, seq)
    w = DN_WIDTH
    return pl.pallas_call(
        _dn_epi_kernel,
        grid=(seq // tm,),
        in_specs=[
            pl.BlockSpec((tm, w), lambda i: (i, 0)),
            pl.BlockSpec((tm, w), lambda i: (i, 0)),
            pl.BlockSpec((tm, w), lambda i: (i, OFF_ZB // w)),
            pl.BlockSpec((1, DN_HEAD_DIM), lambda i: (0, 0)),
        ],
        out_specs=pl.BlockSpec((tm, w), lambda i: (i, 0)),
        out_shape=jax.ShapeDtypeStruct((seq, w), BF16),
        compiler_params=_cparams(("parallel",)),
        name="dn_epi",
    )(o_f, o_b, h, norm_g)


def _dn_branch(h, ab, lp):
    qn, kn, vc, sc, sct = _dn_prep(h, ab, lp["dn_conv"], lp["dn_alog"], lp["dn_dtb"])
    o_f, o_b = _dn_main(qn, kn, vc, sc, sct)
    return _dn_epi(o_f, o_b, h, lp["dn_norm_g"])


def _att_prep_kernel(qlo_ref, qhi_ref, k_ref, v_ref, cos_ref, sin_ref, gq_ref, gk_ref,
                     qt_ref, ko_ref, vt_ref):
    cos2 = cos_ref[...]
    sin2 = sin_ref[...]
    even = lax.broadcasted_iota(jnp.int32, cos2.shape, 1) % 2 == 0

    def norm_rope(x, g):
        ms = jnp.mean(x * x, axis=-1, keepdims=True)
        xn = x * lax.rsqrt(ms + EPS) * g
        partner = jnp.where(even, pltpu.roll(xn, ATT_HEAD_DIM - 1, axis=1), pltpu.roll(xn, 1, axis=1))
        return xn * cos2 + partner * sin2

    gq = gq_ref[...]
    gk = gk_ref[...]
    scale = ATT_HEAD_DIM ** -0.5 * math.log2(math.e)
    half = ATT_HEADS // 2
    for hd in range(ATT_HEADS):
        src = qlo_ref if hd < half else qhi_ref
        ssl = slice((hd % half) * ATT_HEAD_DIM, (hd % half + 1) * ATT_HEAD_DIM)
        sl = slice(hd * ATT_HEAD_DIM, (hd + 1) * ATT_HEAD_DIM)
        qt_ref[sl, :] = (norm_rope(src[:, ssl].astype(F32), gq) * scale).T.astype(BF16)
    for hd in range(ATT_KV_HEADS):
        sl = slice(hd * ATT_HEAD_DIM, (hd + 1) * ATT_HEAD_DIM)
        ko_ref[:, sl] = norm_rope(k_ref[:, sl].astype(F32), gk).astype(BF16)
        vt_ref[sl, :] = v_ref[:, sl].astype(F32).T.astype(BF16)


def _att_prep(h, cos2, sin2, gq, gk):
    seq = h.shape[0]
    tm = min(512, seq)
    return pl.pallas_call(
        _att_prep_kernel,
        grid=(seq // tm,),
        in_specs=[
            pl.BlockSpec((tm, ATT_WIDTH // 2), lambda i: (i, OFF_AQ // (ATT_WIDTH // 2))),
            pl.BlockSpec((tm, ATT_WIDTH // 2), lambda i: (i, OFF_AQ // (ATT_WIDTH // 2) + 1)),
            pl.BlockSpec((tm, ATT_KV_WIDTH), lambda i: (i, OFF_AK // ATT_KV_WIDTH)),
            pl.BlockSpec((tm, ATT_KV_WIDTH), lambda i: (i, OFF_AV // ATT_KV_WIDTH)),
            pl.BlockSpec((tm, ATT_HEAD_DIM), lambda i: (i, 0)),
            pl.BlockSpec((tm, ATT_HEAD_DIM), lambda i: (i, 0)),
            pl.BlockSpec((1, ATT_HEAD_DIM), lambda i: (0, 0)),
            pl.BlockSpec((1, ATT_HEAD_DIM), lambda i: (0, 0)),
        ],
        out_specs=[
            pl.BlockSpec((ATT_WIDTH, tm), lambda i: (0, i)),
            pl.BlockSpec((tm, ATT_KV_WIDTH), lambda i: (i, 0)),
            pl.BlockSpec((ATT_KV_WIDTH, tm), lambda i: (0, i)),
        ],
        out_shape=[
            jax.ShapeDtypeStruct((ATT_WIDTH, seq), BF16),
            jax.ShapeDtypeStruct((seq, ATT_KV_WIDTH), BF16),
            jax.ShapeDtypeStruct((ATT_KV_WIDTH, seq), BF16),
        ],
        compiler_params=_cparams(("parallel",)),
        name="att_prep",
    )(h, h, h, h, cos2, sin2, gq, gk)


def _att_kernel(qt_ref, k_ref, vt_ref, z_ref, o_ref, m_ref, l_ref, acc_ref, *, kc, nkc):
    grp = ATT_HEADS // ATT_KV_HEADS
    m_ref[...] = jnp.full(m_ref.shape, -1e30, F32)
    l_ref[...] = jnp.zeros(l_ref.shape, F32)
    acc_ref[...] = jnp.zeros(acc_ref.shape, F32)

    def chunk(c, carry):
        k0 = pl.multiple_of(c * kc, kc)
        k_c = k_ref[pl.ds(k0, kc), :]
        vt_c = vt_ref[:, pl.ds(k0, kc)]
        s = [_dot(k_c, qt_ref[hh * ATT_HEAD_DIM:(hh + 1) * ATT_HEAD_DIM, :])
             for hh in range(grp)]
        p = [None] * grp
        alpha = [None] * grp
        for hh in range(grp):
            m_old = m_ref[hh]
            m_new = jnp.maximum(m_old, jnp.max(s[hh], axis=0, keepdims=True))
            alpha[hh] = jnp.exp2(m_old - m_new)
            ph = jnp.exp2(s[hh] - m_new)
            l_ref[hh] = alpha[hh] * l_ref[hh] + jnp.sum(ph, axis=0, keepdims=True)
            m_ref[hh] = m_new
            p[hh] = ph.astype(BF16)
        for hh in range(grp):
            acc_ref[hh] = alpha[hh] * acc_ref[hh] + _dot(vt_c, p[hh])
        return carry

    lax.fori_loop(0, nkc, chunk, 0)
    for hh in range(grp):
        sl = slice(hh * ATT_HEAD_DIM, (hh + 1) * ATT_HEAD_DIM)
        o = (acc_ref[hh] / l_ref[hh]).T
        o_ref[:, sl] = (o * _silu(z_ref[:, sl].astype(F32))).astype(BF16)


def _att(qt, kr, vt, h):
    seq = kr.shape[0]
    tq = min(256, seq)
    kc = min(2048, seq)
    grp = ATT_HEADS // ATT_KV_HEADS
    gw = ATT_WIDTH // ATT_KV_HEADS
    return pl.pallas_call(
        functools.partial(_att_kernel, kc=kc, nkc=seq // kc),
        grid=(ATT_KV_HEADS, seq // tq),
        in_specs=[
            pl.BlockSpec((gw, tq), lambda g, i: (g, i)),
            pl.BlockSpec((seq, ATT_HEAD_DIM), lambda g, i: (0, g)),
            pl.BlockSpec((ATT_HEAD_DIM, seq), lambda g, i: (g, 0)),
            pl.BlockSpec((tq, gw), lambda g, i: (i, OFF_ZC // gw + g)),
        ],
        out_specs=pl.BlockSpec((tq, gw), lambda g, i: (i, g)),
        out_shape=jax.ShapeDtypeStruct((seq, ATT_WIDTH), BF16),
        scratch_shapes=[
            pltpu.VMEM((grp, 1, tq), F32),
            pltpu.VMEM((grp, 1, tq), F32),
            pltpu.VMEM((grp, ATT_HEAD_DIM, tq), F32),
        ],
        compiler_params=_cparams(("parallel", "parallel")),
        name="grid_att",
    )(qt, kr, vt, h)


def _mem_kv_kernel(mem_ref, g_ref, w_ref, o_ref):
    x = mem_ref[...]
    ms = jnp.mean(x * x, axis=-1, keepdims=True)
    xn = (x * lax.rsqrt(ms + EPS) * g_ref[...]).astype(BF16)
    o_ref[...] = _dot(xn, w_ref[...].astype(BF16)).astype(BF16)


def _mem_kv(mem, g, w_kv):
    n_mem = mem.shape[0]
    tn = 512
    return pl.pallas_call(
        _mem_kv_kernel,
        grid=(2 * MEM_WIDTH // tn,),
        in_specs=[
            pl.BlockSpec((n_mem, D_MODEL), lambda j: (0, 0)),
            pl.BlockSpec((1, D_MODEL), lambda j: (0, 0)),
            pl.BlockSpec((D_MODEL, tn), lambda j: (0, j)),
        ],
        out_specs=pl.BlockSpec((n_mem, tn), lambda j: (0, j)),
        out_shape=jax.ShapeDtypeStruct((n_mem, 2 * MEM_WIDTH), BF16),
        compiler_params=_cparams(("parallel",)),
        name="mem_kv",
    )(mem, g, w_kv)


def _mem_att_kernel(q_ref, z_ref, kv_ref, o_ref):
    scale = MEM_HEAD_DIM ** -0.5
    for hd in range(MEM_HEADS):
        sl = slice(hd * MEM_HEAD_DIM, (hd + 1) * MEM_HEAD_DIM)
        k = kv_ref[:, sl]
        v = kv_ref[:, MEM_WIDTH + hd * MEM_HEAD_DIM:MEM_WIDTH + (hd + 1) * MEM_HEAD_DIM]
        s = _dot_nt(q_ref[:, sl], k) * scale
        m = jnp.max(s, axis=-1, keepdims=True)
        p = jnp.exp(s - m)
        l = jnp.sum(p, axis=-1, keepdims=True)
        o = _dot(p.astype(BF16), v) / l
        o_ref[:, sl] = (o * _silu(z_ref[:, sl].astype(F32))).astype(BF16)


def _mem_att(h, kv):
    seq = h.shape[0]
    tm = min(1024, seq)
    n_mem = kv.shape[0]
    return pl.pallas_call(
        _mem_att_kernel,
        grid=(seq // tm,),
        in_specs=[
            pl.BlockSpec((tm, MEM_WIDTH), lambda i: (i, OFF_MQ // MEM_WIDTH)),
            pl.BlockSpec((tm, MEM_WIDTH), lambda i: (i, OFF_ZM // MEM_WIDTH)),
            pl.BlockSpec((n_mem, 2 * MEM_WIDTH), lambda i: (0, 0)),
        ],
        out_specs=pl.BlockSpec((tm, MEM_WIDTH), lambda i: (i, 0)),
        out_shape=jax.ShapeDtypeStruct((seq, MEM_WIDTH), BF16),
        compiler_params=_cparams(("parallel",)),
        name="mem_att",
    )(h, h, kv)


def _merge_kernel(ya_ref, yb_ref, yc_ref, ym_ref, ga_ref, gb_ref, gc_ref, gm_ref,
                  wa_ref, wb_ref, wc0_ref, wc1_ref, wm_ref, wo_ref, x_ref, fg_ref, o_ref, acc_ref,
                  *, final_norm, nj):
    j = pl.program_id(1)

    @pl.when(j == 0)
    def _():
        acc_ref[...] = jnp.zeros(acc_ref.shape, F32)

    half = ATT_WIDTH // 2
    yc_w = _dot(yc_ref[:, :half], wc0_ref[...]) + _dot(yc_ref[:, half:], wc1_ref[...])
    merged = jax.nn.sigmoid(ga_ref[...].astype(F32)) * _dot(ya_ref[...], wa_ref[...])
    merged = merged + jax.nn.sigmoid(gb_ref[...].astype(F32)) * _dot(yb_ref[...], wb_ref[...])
    merged = merged + jax.nn.sigmoid(gc_ref[...].astype(F32)) * yc_w
    merged = merged + jax.nn.sigmoid(gm_ref[...].astype(F32)) * _dot(ym_ref[...], wm_ref[...])
    acc_ref[...] += _dot(merged.astype(BF16), wo_ref[...])

    @pl.when(j == nj - 1)
    def _():
        y = x_ref[...] + acc_ref[...]
        if final_norm:
            ms = jnp.mean(y * y, axis=-1, keepdims=True)
            y = y * lax.rsqrt(ms + EPS) * fg_ref[...]
        o_ref[...] = y


def _merge(ys, h, w_branch, w_out, x, final_g, layer, final_norm):
    seq = x.shape[0]
    tm = min(512, seq)
    tn = 512
    nj = D_MODEL // tn
    widths = (SSM_WIDTH, DN_WIDTH, ATT_WIDTH, MEM_WIDTH)
    half = ATT_WIDTH // 2
    assert SSM_WIDTH == DN_WIDTH and (SSM_WIDTH + DN_WIDTH) % half == 0 and MEM_WIDTH == half
    c_blk = (SSM_WIDTH + DN_WIDTH) // half
    in_specs = [pl.BlockSpec((tm, wd), lambda i, j: (i, 0)) for wd in widths]
    for b in range(N_BRANCH):
        base = (OFF_GATE + b * D_MODEL) // tn
        in_specs.append(pl.BlockSpec((tm, tn), lambda i, j, base=base: (i, base + j)))
    in_specs += [
        pl.BlockSpec((None, SSM_WIDTH, tn), lambda i, j: (layer, 0, j)),
        pl.BlockSpec((None, DN_WIDTH, tn), lambda i, j: (layer, 1, j)),
        pl.BlockSpec((None, half, tn), lambda i, j: (layer, c_blk, j)),
        pl.BlockSpec((None, half, tn), lambda i, j: (layer, c_blk + 1, j)),
        pl.BlockSpec((None, half, tn), lambda i, j: (layer, c_blk + 2, j)),
        pl.BlockSpec((None, tn, D_MODEL), lambda i, j: (layer, j, 0)),
        pl.BlockSpec((tm, D_MODEL), lambda i, j: (i, 0)),
        pl.BlockSpec((1, D_MODEL), lambda i, j: (0, 0)),
    ]
    return pl.pallas_call(
        functools.partial(_merge_kernel, final_norm=final_norm, nj=nj),
        grid=(seq // tm, nj),
        in_specs=in_specs,
        out_specs=pl.BlockSpec((tm, D_MODEL), lambda i, j: (i, 0)),
        out_shape=jax.ShapeDtypeStruct((seq, D_MODEL), F32),
        scratch_shapes=[pltpu.VMEM((tm, D_MODEL), F32)],
        compiler_params=_cparams(("parallel", "arbitrary")),
        name="merge_out",
    )(*ys, h, h, h, h, w_branch, w_branch, w_branch, w_branch, w_branch, w_out, x, final_g)


def _rope_tables(seq):
    rows = seq // GRID_W
    row = jnp.repeat(jnp.arange(rows), GRID_W).astype(F32)
    col = jnp.tile(jnp.arange(GRID_W), rows).astype(F32)
    axis_dim = ATT_HEAD_DIM // 2
    freqs = ROPE_THETA ** (-jnp.arange(0, axis_dim, 2, dtype=F32) / axis_dim)
    ang = jnp.concatenate([row[:, None] * freqs, col[:, None] * freqs], axis=-1)
    cos = jnp.cos(ang)
    sin = jnp.sin(ang)
    return (jnp.repeat(cos, 2, axis=-1),
            jnp.stack([-sin, sin], axis=-1).reshape(seq, ATT_HEAD_DIM))


def _pair_rows(p):
    return p.reshape(2, SSM_PAIRS, 1, 2 * SSM_STATE).transpose(1, 0, 2, 3)


def _stacked_weights(w_in, w_branch, w_out):
    w_main = jnp.concatenate([w_in[:, :, :IN_AB], w_in[:, :, IN_AB_END:]], axis=2).astype(BF16)
    w_ab = jnp.pad(w_in[:, :, IN_AB:IN_AB_END],
                   ((0, 0), (0, 0), (0, AB_PAD - (IN_AB_END - IN_AB)))).astype(BF16)
    return w_main, w_ab, w_branch.astype(BF16), w_out.astype(BF16)


def _layer_params(layer, ssm_a_re, ssm_a_im, ssm_log_step, ssm_b_re, ssm_b_im, ssm_c_re,
                  ssm_c_im, ssm_d, ssm_w_glu, ssm_b_glu, dn_conv, dn_a_log, dn_dt_bias,
                  dn_norm_g, attn_q_norm, attn_k_norm):
    ls_n = jnp.broadcast_to(ssm_log_step[layer][:, :, None], ssm_a_re[layer].shape)
    flat = lambda p: p.reshape(2, 1, SSM_GROUPS * SSM_STATE)

    def b_pairs(b):
        bt = b.transpose(0, 1, 3, 2).reshape(2, SSM_PAIRS, 2, SSM_GROUP, SSM_STATE)
        return bt.transpose(1, 0, 3, 2, 4).reshape(SSM_PAIRS, 2, SSM_GROUP, 2 * SSM_STATE)

    def c_pairs(c):
        ct = c.transpose(0, 1, 3, 2).reshape(2, SSM_PAIRS, 2 * SSM_STATE, SSM_GROUP)
        return ct.transpose(1, 0, 2, 3)

    conv = dn_conv[layer].T.reshape(DN_CONV, 3, DN_WIDTH).transpose(1, 0, 2)
    conv = jnp.pad(conv, ((0, 0), (0, 8 - DN_CONV), (0, 0)))

    def lane_vec(p):
        return jnp.pad(p.reshape(1, -1), ((0, 0), (0, AB_PAD - p.size)))

    return {
        "ssm_rows": [_pair_rows(ssm_a_re[layer]), _pair_rows(ssm_a_im[layer]), _pair_rows(ls_n)],
        "ssm_b": [b_pairs(ssm_b_re[layer]), b_pairs(ssm_b_im[layer])],
        "ssm_c": [c_pairs(ssm_c_re[layer]), c_pairs(ssm_c_im[layer])],
        "ssm_flat": [flat(ssm_a_re[layer]), flat(ssm_a_im[layer]), flat(ls_n)],
        "ssm_d": ssm_d[layer].reshape(1, SSM_WIDTH),
        "ssm_w_glu": ssm_w_glu[layer],
        "ssm_b_glu": ssm_b_glu[layer].reshape(1, SSM_WIDTH),
        "dn_conv": conv,
        "dn_alog": lane_vec(dn_a_log[layer]),
        "dn_dtb": lane_vec(dn_dt_bias[layer]),
        "dn_norm_g": dn_norm_g[layer].reshape(1, DN_HEAD_DIM),
        "att_gq": attn_q_norm[layer].reshape(1, ATT_HEAD_DIM),
        "att_gk": attn_k_norm[layer].reshape(1, ATT_HEAD_DIM),
    }


def kernel(x, mem, norm_g, w_in, ssm_a_re, ssm_a_im, ssm_log_step, ssm_b_re, ssm_b_im, ssm_c_re, ssm_c_im, ssm_d, ssm_w_glu, ssm_b_glu, dn_conv, dn_a_log, dn_dt_bias, dn_norm_g, attn_q_norm, attn_k_norm, mem_norm_g, w_mem_kv, w_branch, w_out, final_norm_g):
    bsz, seq, _ = x.shape
    depth = w_in.shape[0]
    cos2, sin2 = _rope_tables(seq)
    final_g = final_norm_g.reshape(1, D_MODEL)
    w_main, w_ab, w_br, w_o = _stacked_weights(w_in, w_branch, w_out)
    outs = []
    for b in range(bsz):
        xb = x[b]
        for layer in range(depth):
            lp = _layer_params(layer, ssm_a_re, ssm_a_im, ssm_log_step, ssm_b_re, ssm_b_im,
                               ssm_c_re, ssm_c_im, ssm_d, ssm_w_glu, ssm_b_glu, dn_conv, dn_a_log,
                               dn_dt_bias, dn_norm_g, attn_q_norm, attn_k_norm)
            h, ab = _inproj(xb, norm_g[layer].reshape(1, D_MODEL), w_main, w_ab, layer)
            y_a = _ssm_branch(h, lp)
            y_b = _dn_branch(h, ab, lp)
            qt, kr, vt = _att_prep(h, cos2, sin2, lp["att_gq"], lp["att_gk"])
            y_c = _att(qt, kr, vt, h)
            kv = _mem_kv(mem[b], mem_norm_g[layer].reshape(1, D_MODEL), w_mem_kv[layer])
            y_m = _mem_att(h, kv)
            xb = _merge((y_a, y_b, y_c, y_m), h, w_br, w_o, xb, final_g, layer,
                        final_norm=(layer == depth - 1))
        outs.append(xb)
    return outs[0][None] if bsz == 1 else jnp.stack(outs, axis=0)
```

```python
import functools
import math

import numpy as np
import jax
import jax.numpy as jnp
from jax import lax
from jax.experimental import pallas as pl
from jax.experimental.pallas import tpu as pltpu

F32 = jnp.float32
BF16 = jnp.bfloat16

D_MODEL = 2048
GRID_W = 64
EPS = 1e-6

SSM_GROUP = 16
SSM_STATE = 64
SSM_GROUPS = 48
SSM_WIDTH = SSM_GROUPS * SSM_GROUP
SSM_T = 16
SSM_PAIRS = SSM_GROUPS // 2
SSM_CW = SSM_T * SSM_GROUP

DN_HEADS = 6
DN_HEAD_DIM = 128
DN_WIDTH = DN_HEADS * DN_HEAD_DIM
DN_CONV = 5
DN_CHUNK = 64
DN_GROUP = 256
DN_HB = 6

ATT_HEADS = 8
ATT_KV_HEADS = 2
ATT_HEAD_DIM = 128
ATT_WIDTH = ATT_HEADS * ATT_HEAD_DIM
ATT_KV_WIDTH = ATT_KV_HEADS * ATT_HEAD_DIM
ROPE_THETA = 10000.0

MEM_HEADS = 4
MEM_HEAD_DIM = 128
MEM_WIDTH = MEM_HEADS * MEM_HEAD_DIM

N_BRANCH = 4

OFF_UA = 0
OFF_ZA = 768
OFF_DQ = 1536
OFF_DK = 2304
OFF_DV = 3072
OFF_ZB = 3840
OFF_AQ = 4608
OFF_AK = 5632
OFF_AV = 5888
OFF_ZC = 6144
OFF_MQ = 7168
OFF_ZM = 7680
OFF_GATE = 8192
IN_AB = 3840
IN_AB_END = IN_AB + 4 * DN_HEADS
H_WIDTH = OFF_GATE + N_BRANCH * D_MODEL
AB_PAD = 128

VMEM_LIMIT = 56 * 1024 * 1024


def _cparams(sem):
    return pltpu.CompilerParams(dimension_semantics=sem, vmem_limit_bytes=VMEM_LIMIT)


def _silu(x):
    return x * jax.nn.sigmoid(x)


def _dot(a, b):
    return jnp.dot(a, b, preferred_element_type=F32)


def _dot_nt(a, b):
    return lax.dot_general(a, b, (((1,), (1,)), ((), ())), preferred_element_type=F32)


def _dot_tn(a, b):
    return lax.dot_general(a, b, (((0,), (0,)), ((), ())), preferred_element_type=F32)


def _split3(x):
    x1 = x.astype(BF16)
    r1 = x - x1.astype(F32)
    x2 = r1.astype(BF16)
    x3 = (r1 - x2.astype(F32)).astype(BF16)
    return x1, x2, x3


def _dot_exact_lhs(a_bf16, x):
    x1, x2, x3 = _split3(x)
    return _dot(a_bf16, x1) + _dot(a_bf16, x2) + _dot(a_bf16, x3)


def _dot_exact_rhs(x, b_bf16):
    x1, x2, x3 = _split3(x)
    return _dot(x1, b_bf16) + _dot(x2, b_bf16) + _dot(x3, b_bf16)


def _dot_f32(a, b):
    a1, a2, a3 = _split3(a)
    b1, b2, b3 = _split3(b)
    return (_dot(a1, b1) + (_dot(a1, b2) + _dot(a2, b1))
            + (_dot(a1, b3) + _dot(a2, b2) + _dot(a3, b1)))


def _inproj_kernel(x_ref, g_ref, w_ref, wab_ref, h_ref, ab_ref, xn_ref):
    @pl.when(pl.program_id(1) == 0)
    def _():
        x = x_ref[...]
        ms = jnp.mean(x * x, axis=-1, keepdims=True)
        xn_ref[...] = (x * lax.rsqrt(ms + EPS) * g_ref[...]).astype(BF16)
        ab_ref[...] = _dot(xn_ref[...], wab_ref[...])

    h_ref[...] = _dot(xn_ref[...], w_ref[...]).astype(BF16)


def _inproj(x, g, w_main, w_ab, layer):
    seq = x.shape[0]
    tm = min(1024, seq)
    tn = 2048
    return pl.pallas_call(
        _inproj_kernel,
        grid=(seq // tm, H_WIDTH // tn),
        in_specs=[
            pl.BlockSpec((tm, D_MODEL), lambda i, j: (i, 0)),
            pl.BlockSpec((1, D_MODEL), lambda i, j: (0, 0)),
            pl.BlockSpec((None, D_MODEL, tn), lambda i, j: (layer, 0, j)),
            pl.BlockSpec((None, D_MODEL, AB_PAD), lambda i, j: (layer, 0, 0)),
        ],
        out_specs=[
            pl.BlockSpec((tm, tn), lambda i, j: (i, j)),
            pl.BlockSpec((tm, AB_PAD), lambda i, j: (i, 0)),
        ],
        out_shape=[
            jax.ShapeDtypeStruct((seq, H_WIDTH), BF16),
            jax.ShapeDtypeStruct((seq, AB_PAD), F32),
        ],
        scratch_shapes=[pltpu.VMEM((tm, D_MODEL), BF16)],
        compiler_params=_cparams(("parallel", "arbitrary")),
        name="inproj",
    )(x, g, w_main, w_ab)


def _cpow_table(base_re, base_im, n):
    re = [jnp.ones_like(base_re)]
    im = [jnp.zeros_like(base_im)]
    for _ in range(n):
        re_n = re[-1] * base_re - im[-1] * base_im
        im_n = re[-1] * base_im + im[-1] * base_re
        re.append(re_n)
        im.append(im_n)
    return re, im


def _ssm_prep_kernel(are_r, aim_r, ls_r, bre_ref, bim_ref, cre_ref, cim_ref,
                     win_ref, wout_ref, m_ref):
    t = SSM_T
    cw = SSM_CW
    lane128 = lax.broadcasted_iota(jnp.int32, (1, 128), 1)
    col_t = lax.broadcasted_iota(jnp.int32, (1, cw), 1) // SSM_GROUP
    sub128 = lax.broadcasted_iota(jnp.int32, (128, 1), 0)
    lane_cw = lax.broadcasted_iota(jnp.int32, (SSM_GROUP, cw), 1)
    tile_p = jnp.where(lane_cw % SSM_GROUP == lax.broadcasted_iota(jnp.int32, (SSM_GROUP, cw), 0),
                       1.0, 0.0).astype(BF16)

    m_ref[...] = jnp.zeros(m_ref.shape, m_ref.dtype)

    krows = [[None, None], [None, None]]
    for d in range(2):
        step = jnp.exp(ls_r[d])
        a_re = are_r[d]
        a_im = aim_r[d]
        mag = jnp.exp(a_re * step)
        lam_re = mag * jnp.cos(a_im * step)
        lam_im = mag * jnp.sin(a_im * step)
        den = a_re * a_re + a_im * a_im
        nr = lam_re - 1.0
        ni = lam_im
        coef_re = (nr * a_re + ni * a_im) / den
        coef_im = (ni * a_re - nr * a_im) / den
        b_re = bre_ref[d]
        b_im = bim_ref[d]
        bbt_re = coef_re * b_re - coef_im * b_im
        bbt_im = coef_re * b_im + coef_im * b_re
        bb_re = jnp.concatenate([bbt_re] * t, axis=0)
        bb_im = jnp.concatenate([bbt_im] * t, axis=0)
        pr_re, pr_im = _cpow_table(lam_re, lam_im, t)
        e_in = [(t - 1 - s) if d == 0 else s for s in range(t)]
        p_re = jnp.concatenate([jnp.broadcast_to(pr_re[e], (SSM_GROUP, 128)) for e in e_in], axis=0)
        p_im = jnp.concatenate([jnp.broadcast_to(pr_im[e], (SSM_GROUP, 128)) for e in e_in], axis=0)
        w_re = p_re * bb_re - p_im * bb_im
        w_im = p_re * bb_im + p_im * bb_re
        for par in range(2):
            keep = (lane128 < 64) if par == 0 else (lane128 >= 64)
            win_ref[par * cw:(par + 1) * cw, (2 * d) * 128:(2 * d + 1) * 128] = (
                jnp.where(keep, w_re, 0.0).astype(BF16))
            win_ref[par * cw:(par + 1) * cw, (2 * d + 1) * 128:(2 * d + 2) * 128] = (
                jnp.where(keep, w_im, 0.0).astype(BF16))

        tab_re = jnp.zeros((128, 128), F32)
        tab_im = jnp.zeros((128, 128), F32)
        for e in range(t + 1):
            tab_re = jnp.where(sub128 == e, pr_re[e], tab_re)
            tab_im = jnp.where(sub128 == e, pr_im[e], tab_im)
        tab_re = tab_re.T
        tab_im = tab_im.T
        c_re = _dot_exact_rhs(cre_ref[d], tile_p)
        c_im = _dot_exact_rhs(cim_ref[d], tile_p)

        def c_lam(expo_row):
            sel = jnp.where(sub128 == expo_row, 1.0, 0.0).astype(BF16)
            q_re = _dot_exact_rhs(tab_re, sel)
            q_im = _dot_exact_rhs(tab_im, sel)
            return c_re * q_re - c_im * q_im, -(c_re * q_im + c_im * q_re)

        o_re, o_im = c_lam((col_t + 1) if d == 0 else (t - col_t))
        for par in range(2):
            keep = (sub128 < 64) if par == 0 else (sub128 >= 64)
            wout_ref[(2 * d) * 128:(2 * d + 1) * 128, par * cw:(par + 1) * cw] = (
                jnp.where(keep, o_re, 0.0).astype(BF16))
            wout_ref[(2 * d + 1) * 128:(2 * d + 2) * 128, par * cw:(par + 1) * cw] = (
                jnp.where(keep, o_im, 0.0).astype(BF16))

        r_re, r_im = c_lam(col_t if d == 0 else (t - 1 - col_t))
        for par in range(2):
            keep = (lane128 < 64) if par == 0 else (lane128 >= 64)
            krows[d][par] = (_dot_f32(jnp.where(keep, bbt_re, 0.0), r_re)
                             + _dot_f32(jnp.where(keep, bbt_im, 0.0), r_im))

    for par in range(2):
        kf = krows[0][par]
        kb = krows[1][par]
        for s in range(t):
            sh_f = SSM_GROUP * s
            blk = jnp.where(lane_cw >= sh_f, pltpu.roll(kf, sh_f, axis=1) if sh_f else kf, 0.0)
            sh_b = SSM_GROUP * (t - 1 - s)
            rolled_b = pltpu.roll(kb, cw - sh_b, axis=1) if sh_b else kb
            blk = blk + jnp.where(lane_cw < cw - sh_b, rolled_b, 0.0)
            m_ref[par * cw + s * SSM_GROUP:par * cw + (s + 1) * SSM_GROUP,
                  par * cw:(par + 1) * cw] = blk.astype(BF16)


def _ssm_prep(rows, b_t, c_t):
    pw = 2 * SSM_CW
    row_spec = pl.BlockSpec((None, 2, 1, 128), lambda g: (g, 0, 0, 0))
    b_spec = pl.BlockSpec((None, 2, SSM_GROUP, 128), lambda g: (g, 0, 0, 0))
    c_spec = pl.BlockSpec((None, 2, 128, SSM_GROUP), lambda g: (g, 0, 0, 0))
    w_spec = pl.BlockSpec((None, pw, pw), lambda g: (g, 0, 0))
    return pl.pallas_call(
        _ssm_prep_kernel,
        grid=(SSM_PAIRS,),
        in_specs=[row_spec] * 3 + [b_spec] * 2 + [c_spec] * 2,
        out_specs=[w_spec] * 3,
        out_shape=[jax.ShapeDtypeStruct((SSM_PAIRS, pw, pw), BF16)] * 3,
        compiler_params=_cparams(("parallel",)),
        name="ssm_prep",
    )(*rows, *b_t, *c_t)


SSM_BG = 8
SSM_BP = SSM_BG // 2
SSM_BW = SSM_BG * SSM_CW


def _ssm_place(t0):
    r = lax.broadcasted_iota(jnp.int32, (256, SSM_BW), 0)
    c = lax.broadcasted_iota(jnp.int32, (256, SSM_BW), 1)
    j = r % 128
    target = (j // SSM_GROUP) * SSM_CW + (t0 + r // 128) * SSM_GROUP + j % SSM_GROUP
    return jnp.where(c == target, 1.0, 0.0).astype(BF16)


def _ssm_in_kernel(u_ref, win_ref, u2_ref, h_ref, uf_ref):
    nchunk = u2_ref.shape[0]
    uf_ref[...] = u_ref[...].astype(F32)
    acc = jnp.zeros((nchunk, SSM_BW), F32)
    for t0 in range(0, SSM_T, 2):
        lhs = jnp.concatenate([uf_ref[pl.ds(t0, nchunk, stride=SSM_T), :],
                               uf_ref[pl.ds(t0 + 1, nchunk, stride=SSM_T), :]], axis=1)
        acc = acc + _dot(lhs.astype(BF16), _ssm_place(t0))
    u2_ref[...] = acc.astype(BF16)
    pw = 2 * SSM_CW
    for j in range(SSM_BP):
        h = _dot(u2_ref[:, j * pw:(j + 1) * pw], win_ref[j])
        for k in range(4):
            h_ref[k, :, j * 128:(j + 1) * 128] = h[:, k * 128:(k + 1) * 128]


def _ssm_in(h, win):
    seq = h.shape[0]
    nchunk = seq // SSM_T
    pw = 2 * SSM_CW
    nblk = SSM_GROUPS // SSM_BG
    return pl.pallas_call(
        _ssm_in_kernel,
        grid=(nblk,),
        in_specs=[
            pl.BlockSpec((seq, 128), lambda b: (0, OFF_UA // 128 + b)),
            pl.BlockSpec((SSM_BP, pw, pw), lambda b: (b, 0, 0)),
        ],
        out_specs=[
            pl.BlockSpec((nchunk, SSM_BW), lambda b: (0, b)),
            pl.BlockSpec((4, nchunk, SSM_BP * 128), lambda b: (0, 0, b)),
        ],
        out_shape=[
            jax.ShapeDtypeStruct((nchunk, SSM_GROUPS * SSM_CW), BF16),
            jax.ShapeDtypeStruct((4, nchunk, SSM_PAIRS * 128), F32),
        ],
        scratch_shapes=[pltpu.VMEM((seq, 128), F32)],
        compiler_params=_cparams(("parallel",)),
        name="ssm_in",
    )(h, win)


def _ssm_scan_kernel(are_ref, aim_ref, ls_ref, h_ref, p_ref, *, nchunk):
    width = h_ref.shape[2]

    def lam_pow_t(d):
        step = jnp.exp(ls_ref[d]) * float(SSM_T)
        mag = jnp.exp(are_ref[d] * step)
        ang = aim_ref[d] * step
        return mag * jnp.cos(ang), mag * jnp.sin(ang)

    lfr, lfi = lam_pow_t(0)
    lbr, lbi = lam_pow_t(1)

    def body(c, carry):
        fr, fi, br, bi = carry
        cb = nchunk - 1 - c
        p_ref[0, pl.ds(c, 1), :] = fr
        p_ref[1, pl.ds(c, 1), :] = fi
        p_ref[2, pl.ds(cb, 1), :] = br
        p_ref[3, pl.ds(cb, 1), :] = bi
        hfr = h_ref[0, pl.ds(c, 1), :]
        hfi = h_ref[1, pl.ds(c, 1), :]
        hbr = h_ref[2, pl.ds(cb, 1), :]
        hbi = h_ref[3, pl.ds(cb, 1), :]
        nfr = lfr * fr - lfi * fi + hfr
        nfi = lfr * fi + lfi * fr + hfi
        nbr = lbr * br - lbi * bi + hbr
        nbi = lbr * bi + lbi * br + hbi
        return nfr, nfi, nbr, nbi

    z = jnp.zeros((1, width), F32)
    lax.fori_loop(0, nchunk, body, (z, z, z, z))


def _ssm_scan(flat_params, h):
    nchunk = h.shape[1]
    width = h.shape[2]
    wt = 768
    return pl.pallas_call(
        functools.partial(_ssm_scan_kernel, nchunk=nchunk),
        grid=(width // wt,),
        in_specs=[pl.BlockSpec((2, 1, wt), lambda j: (0, 0, j))] * 3 + [
            pl.BlockSpec((4, nchunk, wt), lambda j: (0, 0, j)),
        ],
        out_specs=pl.BlockSpec((4, nchunk, wt), lambda j: (0, 0, j)),
        out_shape=jax.ShapeDtypeStruct(h.shape, F32),
        compiler_params=_cparams(("parallel",)),
        name="ssm_scan",
    )(*flat_params, h)


def _ssm_out_kernel(u2_ref, p_ref, m_ref, wout_ref, y_ref):
    nchunk = u2_ref.shape[0]
    pw = 2 * SSM_CW
    hi = []
    lo = []
    for j in range(SSM_BP):
        acc = _dot(u2_ref[:, j * pw:(j + 1) * pw], m_ref[j])
        for k in range(4):
            acc = acc + _dot(p_ref[k, :, j * 128:(j + 1) * 128].astype(BF16),
                             wout_ref[j, k * 128:(k + 1) * 128, :])
        acc_hi = acc.astype(BF16)
        hi.append(acc_hi)
        lo.append((acc - acc_hi.astype(F32)).astype(BF16))
    y_hi = jnp.concatenate(hi, axis=1)
    y_lo = jnp.concatenate(lo, axis=1)
    for t0 in range(0, SSM_T, 2):
        place = _ssm_place(t0)
        yt = _dot_nt(y_hi, place) + _dot_nt(y_lo, place)
        y_ref[pl.ds(t0, nchunk, stride=SSM_T), :] = yt[:, :128]
        y_ref[pl.ds(t0 + 1, nchunk, stride=SSM_T), :] = yt[:, 128:]


def _ssm_out(u2, p, m, wout):
    nchunk = u2.shape[0]
    seq = nchunk * SSM_T
    pw = 2 * SSM_CW
    nblk = SSM_GROUPS // SSM_BG
    return pl.pallas_call(
        _ssm_out_kernel,
        grid=(nblk,),
        in_specs=[
            pl.BlockSpec((nchunk, SSM_BW), lambda b: (0, b)),
            pl.BlockSpec((4, nchunk, SSM_BP * 128), lambda b: (0, 0, b)),
            pl.BlockSpec((SSM_BP, pw, pw), lambda b: (b, 0, 0)),
            pl.BlockSpec((SSM_BP, pw, pw), lambda b: (b, 0, 0)),
        ],
        out_specs=pl.BlockSpec((seq, 128), lambda b: (0, b)),
        out_shape=jax.ShapeDtypeStruct((seq, SSM_WIDTH), F32),
        compiler_params=_cparams(("parallel",)),
        name="ssm_out",
    )(u2, p, m, wout)


def _ssm_epi_kernel(y_ref, u_ref, z_ref, d_ref, wg_ref, bg_ref, o_ref):
    y = y_ref[...] + d_ref[...] * u_ref[...].astype(F32)
    y = jax.nn.gelu(y)
    glu = _dot(y.astype(BF16), wg_ref[...].astype(BF16)) + bg_ref[...]
    y = y * jax.nn.sigmoid(glu)
    o_ref[...] = (y * _silu(z_ref[...].astype(F32))).astype(BF16)


def _ssm_epi(y, h, d, w_glu, b_glu):
    seq = y.shape[0]
    tm = min(512, seq)
    w = SSM_WIDTH
    return pl.pallas_call(
        _ssm_epi_kernel,
        grid=(seq // tm,),
        in_specs=[
            pl.BlockSpec((tm, w), lambda i: (i, 0)),
            pl.BlockSpec((tm, w), lambda i: (i, OFF_UA // w)),
            pl.BlockSpec((tm, w), lambda i: (i, OFF_ZA // w)),
            pl.BlockSpec((1, w), lambda i: (0, 0)),
            pl.BlockSpec((w, w), lambda i: (0, 0)),
            pl.BlockSpec((1, w), lambda i: (0, 0)),
        ],
        out_specs=pl.BlockSpec((tm, w), lambda i: (i, 0)),
        out_shape=jax.ShapeDtypeStruct((seq, w), BF16),
        compiler_params=_cparams(("parallel",)),
        name="ssm_epi",
    )(y, h, h, d, w_glu, b_glu)


def _ssm_branch(h, lp):
    win, wout, m = _ssm_prep(lp["ssm_rows"], lp["ssm_b"], lp["ssm_c"])
    u2, hs = _ssm_in(h, win)
    p = _ssm_scan(lp["ssm_flat"], hs)
    y = _ssm_out(u2, p, m, wout)
    return _ssm_epi(y, h, lp["ssm_d"], lp["ssm_w_glu"], lp["ssm_b_glu"])


def _dn_prep_kernel(qc, qp, qn, kc, kp, kn, vc, vp, vn, ab_ref, cw_ref, alog_ref, dtb_ref,
                    qo_ref, ko_ref, vo_ref, sc_ref, sct_ref, ext_ref, *, tm, nblk):
    i = pl.program_id(0)
    halo = 16
    pad = DN_CONV // 2

    def conv_silu(cur, prev, nxt, part):
        ext_ref[0:halo, :] = jnp.where(i > 0, prev[...].astype(F32), 0.0)
        ext_ref[halo:halo + tm, :] = cur[...].astype(F32)
        ext_ref[halo + tm:halo + tm + halo, :] = jnp.where(i < nblk - 1, nxt[...].astype(F32), 0.0)
        acc = jnp.zeros((tm, DN_WIDTH), F32)
        for j in range(DN_CONV):
            acc = acc + cw_ref[part, j:j + 1, :] * ext_ref[pl.ds(halo - pad + j, tm), :]
        return _silu(acc)

    def l2n(x):
        outs = []
        for hd in range(DN_HEADS):
            xh = x[:, hd * DN_HEAD_DIM:(hd + 1) * DN_HEAD_DIM]
            outs.append(xh * lax.rsqrt(jnp.sum(xh * xh, axis=-1, keepdims=True) + EPS))
        return jnp.concatenate(outs, axis=1)

    qo_ref[...] = (l2n(conv_silu(qc, qp, qn, 0)) * (DN_HEAD_DIM ** -0.5)).astype(BF16)
    ko_ref[...] = l2n(conv_silu(kc, kp, kn, 1)).astype(BF16)
    vo_ref[...] = conv_silu(vc, vp, vn, 2).astype(BF16)

    ab = ab_ref[...]
    g_all = -jnp.exp(alog_ref[...]) * jax.nn.softplus(ab + dtb_ref[...])
    beta_all = jax.nn.sigmoid(ab)
    r = lax.broadcasted_iota(jnp.int32, (tm, tm), 0)
    c = lax.broadcasted_iota(jnp.int32, (tm, tm), 1)
    same = (r // DN_CHUNK) == (c // DN_CHUNK)
    tri_f = jnp.where(same & (c <= r), 1.0, 0.0).astype(BF16)
    tri_b = jnp.where(same & (c >= r), 1.0, 0.0).astype(BF16)
    blk = jnp.where(same, 1.0, 0.0).astype(BF16)
    gcf = _dot_exact_lhs(tri_f, g_all)
    gcb = _dot_exact_lhs(tri_b, g_all)
    gtot = _dot_exact_lhs(blk, g_all)
    lane = lax.broadcasted_iota(jnp.int32, (tm, AB_PAD), 1)
    sc = jnp.where(lane < 6, gcf,
                   jnp.where(lane < 12, gcb,
                             jnp.where(lane < 24, beta_all,
                                       jnp.where(lane < 36, pltpu.roll(gtot, 24, axis=1), 0.0))))
    sc_ref[...] = sc
    sct_ref[...] = sc.T


def _dn_prep(h, ab, conv_w, alog, dtb):
    seq = h.shape[0]
    tm = min(256, seq)
    nblk = seq // tm
    w = DN_WIDTH
    hb = tm // 16
    nh = seq // 16

    def cur(ci):
        return pl.BlockSpec((tm, w), lambda i: (i, ci))

    def prev(ci):
        return pl.BlockSpec((16, w), lambda i: (jnp.maximum(i * hb - 1, 0), ci))

    def nxt(ci):
        return pl.BlockSpec((16, w), lambda i: (jnp.minimum((i + 1) * hb, nh - 1), ci))

    in_specs = []
    for off in (OFF_DQ, OFF_DK, OFF_DV):
        ci = off // w
        in_specs += [cur(ci), prev(ci), nxt(ci)]
    in_specs += [
        pl.BlockSpec((tm, AB_PAD), lambda i: (i, 0)),
        pl.BlockSpec((3, 8, w), lambda i: (0, 0, 0)),
        pl.BlockSpec((1, AB_PAD), lambda i: (0, 0)),
        pl.BlockSpec((1, AB_PAD), lambda i: (0, 0)),
    ]
    return pl.pallas_call(
        functools.partial(_dn_prep_kernel, tm=tm, nblk=nblk),
        grid=(nblk,),
        in_specs=in_specs,
        out_specs=[pl.BlockSpec((tm, w), lambda i: (i, 0))] * 3
        + [pl.BlockSpec((tm, AB_PAD), lambda i: (i, 0)), pl.BlockSpec((AB_PAD, tm), lambda i: (0, i))],
        out_shape=[jax.ShapeDtypeStruct((seq, w), BF16)] * 3
        + [jax.ShapeDtypeStruct((seq, AB_PAD), F32), jax.ShapeDtypeStruct((AB_PAD, seq), F32)],
        scratch_shapes=[pltpu.VMEM((tm + 32, w), F32)],
        compiler_params=_cparams(("parallel",)),
        name="dn_prep",
    )(h, h, h, h, h, h, h, h, h, ab, conv_w, alog, dtb)


def _dn_main_kernel(qf_ref, kf_ref, vf_ref, scf_ref, sctf_ref,
                    qb_ref, kb_ref, vb_ref, scb_ref, sctb_ref, of_ref, ob_ref, s_ref):
    gsz = DN_GROUP
    nck = gsz // DN_CHUNK

    @pl.when(pl.program_id(1) == 0)
    def _():
        s_ref[...] = jnp.zeros(s_ref.shape, F32)

    refs = ((qf_ref, kf_ref, vf_ref, scf_ref, sctf_ref, of_ref),
            (qb_ref, kb_ref, vb_ref, scb_ref, sctb_ref, ob_ref))
    r = lax.broadcasted_iota(jnp.int32, (gsz, gsz), 0)
    c = lax.broadcasted_iota(jnp.int32, (gsz, gsz), 1)
    same = (r // DN_CHUNK) == (c // DN_CHUNK)
    incl = (same & (r >= c), same & (r <= c))
    strict = (same & (r > c), same & (r < c))

    chains = []
    for d in range(2):
        q_ref, k_ref, v_ref, sc_ref, sct_ref, o_ref = refs[d]
        sc = sc_ref[...]
        sct = sct_ref[...]
        lane = lax.broadcasted_iota(jnp.int32, sc.shape, 1)
        sub = lax.broadcasted_iota(jnp.int32, sct.shape, 0)
        for hl in range(DN_HB):
            cidx = d * DN_HEADS + pl.program_id(0) * DN_HB + hl
            hs = slice(hl * DN_HEAD_DIM, (hl + 1) * DN_HEAD_DIM)

            def col(ci, sc=sc, lane=lane):
                return jnp.sum(jnp.where(lane == ci, sc, 0.0), axis=1, keepdims=True)

            ch = {"d": d, "hl": hl, "hs": hs, "o_ref": o_ref}
            ch["q"] = q_ref[:, hs]
            ch["k"] = k_ref[:, hs]
            ch["v"] = v_ref[:, hs]
            ch["gc_col"] = col(cidx)
            ch["gc_row"] = jnp.sum(jnp.where(sub == cidx, sct, 0.0), axis=0, keepdims=True)
            ch["beta_col"] = col(12 + cidx)
            ch["gtot_col"] = col(24 + cidx)
            chains.append(ch)

    for ch in chains:
        ch["kf"] = ch["k"].astype(F32)
        ch["kb"] = ch["kf"] * ch["beta_col"]
        ch["gram"] = _dot_nt(ch["kb"].astype(BF16), ch["k"])
    for ch in chains:
        ch["qk"] = _dot_nt(ch["q"], ch["k"])
    for ch in chains:
        d = ch["d"]
        ch["decay"] = jnp.where(incl[d], jnp.exp(jnp.where(incl[d], ch["gc_col"] - ch["gc_row"], 0.0)), 0.0)
        ch["x"] = jnp.concatenate([ch["v"].astype(F32) * ch["beta_col"],
                                   ch["kb"] * jnp.exp(ch["gc_col"])], axis=1)
        ch["qd16"] = (ch["q"].astype(F32) * jnp.exp(ch["gc_col"])).astype(BF16)
        ch["kd16"] = (ch["kf"] * jnp.exp(ch["gtot_col"] - ch["gc_col"])).astype(BF16)
    for ch in chains:
        ch["p"] = jnp.where(strict[ch["d"]], ch["gram"] * ch["decay"], 0.0).astype(BF16)
        ch["intra"] = (ch["qk"] * ch["decay"]).astype(BF16)
    for ch in chains:
        ch["x"] = ch["x"] - _dot(ch["p"], ch["x"].astype(BF16))
    npow = 2
    while npow < DN_CHUNK:
        for ch in chains:
            ch["p"] = _dot(ch["p"], ch["p"]).astype(BF16)
        for ch in chains:
            ch["x"] = ch["x"] + _dot(ch["p"], ch["x"].astype(BF16))
        npow *= 2
    for ch in chains:
        ch["u"] = ch["x"][:, :DN_HEAD_DIM]
        ch["w16"] = ch["x"][:, DN_HEAD_DIM:].astype(BF16)
        ch["s"] = s_ref[ch["d"], ch["hl"]]
        ch["vnew"] = [None] * nck
        ch["oq"] = [None] * nck

    for step in range(nck):
        for ch in chains:
            j = step if ch["d"] == 0 else nck - 1 - step
            lo, hi = j * DN_CHUNK, (j + 1) * DN_CHUNK
            s16 = ch["s"].astype(BF16)
            ws = _dot(jnp.concatenate([ch["w16"][lo:hi], ch["qd16"][lo:hi]], axis=0), s16)
            vn = ch["u"][lo:hi] - ws[:DN_CHUNK]
            ch["oq"][j] = ws[DN_CHUNK:]
            ch["vnew"][j] = vn
            ch["s"] = (ch["s"] * jnp.exp(ch["gtot_col"][lo:lo + 1, :])
                       + _dot_tn(ch["kd16"][lo:hi], vn.astype(BF16)))
    for ch in chains:
        s_ref[ch["d"], ch["hl"]] = ch["s"]
        vn_all = jnp.concatenate(ch["vnew"], axis=0).astype(BF16)
        ch["o_ref"][:, ch["hs"]] = jnp.concatenate(ch["oq"], axis=0) + _dot(ch["intra"], vn_all)


def _dn_main(qn, kn, vc, sc, sct):
    seq = qn.shape[0]
    gsz = DN_GROUP
    ng = seq // gsz
    hw = DN_HB * DN_HEAD_DIM
    sct_rows = sct.shape[0]
    fwd = lambda h, i: (i, h)
    bwd = lambda h, i: (ng - 1 - i, h)
    in_specs = [
        pl.BlockSpec((gsz, hw), fwd), pl.BlockSpec((gsz, hw), fwd), pl.BlockSpec((gsz, hw), fwd),
        pl.BlockSpec((gsz, AB_PAD), lambda h, i: (i, 0)),
        pl.BlockSpec((sct_rows, gsz), lambda h, i: (0, i)),
        pl.BlockSpec((gsz, hw), bwd), pl.BlockSpec((gsz, hw), bwd), pl.BlockSpec((gsz, hw), bwd),
        pl.BlockSpec((gsz, AB_PAD), lambda h, i: (ng - 1 - i, 0)),
        pl.BlockSpec((sct_rows, gsz), lambda h, i: (0, ng - 1 - i)),
    ]
    return pl.pallas_call(
        _dn_main_kernel,
        grid=(DN_HEADS // DN_HB, ng),
        in_specs=in_specs,
        out_specs=[pl.BlockSpec((gsz, hw), fwd), pl.BlockSpec((gsz, hw), bwd)],
        out_shape=[jax.ShapeDtypeStruct((seq, DN_WIDTH), F32)] * 2,
        scratch_shapes=[pltpu.VMEM((2, DN_HB, DN_HEAD_DIM, DN_HEAD_DIM), F32)],
        compiler_params=_cparams(("parallel", "arbitrary")),
        name="dn_main",
    )(qn, kn, vc, sc, sct, qn, kn, vc, sc, sct)


def _dn_epi_kernel(of_ref, ob_ref, z_ref, g_ref, o_ref):
    o = of_ref[...] + ob_ref[...]
    g = g_ref[...]
    outs = []
    for hd in range(DN_HEADS):
        oh = o[:, hd * DN_HEAD_DIM:(hd + 1) * DN_HEAD_DIM]
        ms = jnp.mean(oh * oh, axis=-1, keepdims=True)
        outs.append(oh * lax.rsqrt(ms + EPS) * g)
    y = jnp.concatenate(outs, axis=1)
    o_ref[...] = (y * _silu(z_ref[...].astype(F32))).astype(BF16)


def _dn_epi(o_f, o_b, h, norm_g):
    seq = o_f.shape[0]
    tm = min(512, seq)
    w = DN_WIDTH
    return pl.pallas_call(
        _dn_epi_kernel,
        grid=(seq // tm,),
        in_specs=[
            pl.BlockSpec((tm, w), lambda i: (i, 0)),
            pl.BlockSpec((tm, w), lambda i: (i, 0)),
            pl.BlockSpec((tm, w), lambda i: (i, OFF_ZB // w)),
            pl.BlockSpec((1, DN_HEAD_DIM), lambda i: (0, 0)),
        ],
        out_specs=pl.BlockSpec((tm, w), lambda i: (i, 0)),
        out_shape=jax.ShapeDtypeStruct((seq, w), BF16),
        compiler_params=_cparams(("parallel",)),
        name="dn_epi",
    )(o_f, o_b, h, norm_g)


def _dn_branch(h, ab, lp):
    qn, kn, vc, sc, sct = _dn_prep(h, ab, lp["dn_conv"], lp["dn_alog"], lp["dn_dtb"])
    o_f, o_b = _dn_main(qn, kn, vc, sc, sct)
    return _dn_epi(o_f, o_b, h, lp["dn_norm_g"])


def _att_prep_kernel(qlo_ref, qhi_ref, k_ref, v_ref, cos_ref, sin_ref, gq_ref, gk_ref,
                     qt_ref, ko_ref, vt_ref):
    cos2 = cos_ref[...]
    sin2 = sin_ref[...]
    even = lax.broadcasted_iota(jnp.int32, cos2.shape, 1) % 2 == 0

    def norm_rope(x, g):
        ms = jnp.mean(x * x, axis=-1, keepdims=True)
        xn = x * lax.rsqrt(ms + EPS) * g
        partner = jnp.where(even, pltpu.roll(xn, ATT_HEAD_DIM - 1, axis=1), pltpu.roll(xn, 1, axis=1))
        return xn * cos2 + partner * sin2

    gq = gq_ref[...]
    gk = gk_ref[...]
    scale = ATT_HEAD_DIM ** -0.5 * math.log2(math.e)
    half = ATT_HEADS // 2
    for hd in range(ATT_HEADS):
        src = qlo_ref if hd < half else qhi_ref
        ssl = slice((hd % half) * ATT_HEAD_DIM, (hd % half + 1) * ATT_HEAD_DIM)
        sl = slice(hd * ATT_HEAD_DIM, (hd + 1) * ATT_HEAD_DIM)
        qt_ref[sl, :] = (norm_rope(src[:, ssl].astype(F32), gq) * scale).T.astype(BF16)
    for hd in range(ATT_KV_HEADS):
        sl = slice(hd * ATT_HEAD_DIM, (hd + 1) * ATT_HEAD_DIM)
        ko_ref[:, sl] = norm_rope(k_ref[:, sl].astype(F32), gk).astype(BF16)
        vt_ref[sl, :] = v_ref[:, sl].astype(F32).T.astype(BF16)


def _att_prep(h, cos2, sin2, gq, gk):
    seq = h.shape[0]
    tm = min(512, seq)
    return pl.pallas_call(
        _att_prep_kernel,
        grid=(seq // tm,),
        in_specs=[
            pl.BlockSpec((tm, ATT_WIDTH // 2), lambda i: (i, OFF_AQ // (ATT_WIDTH // 2))),
            pl.BlockSpec((tm, ATT_WIDTH // 2), lambda i: (i, OFF_AQ // (ATT_WIDTH // 2) + 1)),
            pl.BlockSpec((tm, ATT_KV_WIDTH), lambda i: (i, OFF_AK // ATT_KV_WIDTH)),
            pl.BlockSpec((tm, ATT_KV_WIDTH), lambda i: (i, OFF_AV // ATT_KV_WIDTH)),
            pl.BlockSpec((tm, ATT_HEAD_DIM), lambda i: (i, 0)),
            pl.BlockSpec((tm, ATT_HEAD_DIM), lambda i: (i, 0)),
            pl.BlockSpec((1, ATT_HEAD_DIM), lambda i: (0, 0)),
            pl.BlockSpec((1, ATT_HEAD_DIM), lambda i: (0, 0)),
        ],
        out_specs=[
            pl.BlockSpec((ATT_WIDTH, tm), lambda i: (0, i)),
            pl.BlockSpec((tm, ATT_KV_WIDTH), lambda i: (i, 0)),
            pl.BlockSpec((ATT_KV_WIDTH, tm), lambda i: (0, i)),
        ],
        out_shape=[
            jax.ShapeDtypeStruct((ATT_WIDTH, seq), BF16),
            jax.ShapeDtypeStruct((seq, ATT_KV_WIDTH), BF16),
            jax.ShapeDtypeStruct((ATT_KV_WIDTH, seq), BF16),
        ],
        compiler_params=_cparams(("parallel",)),
        name="att_prep",
    )(h, h, h, h, cos2, sin2, gq, gk)


def _att_kernel(qt_ref, k_ref, vt_ref, z_ref, o_ref, m_ref, l_ref, acc_ref, *, kc, nkc):
    grp = ATT_HEADS // ATT_KV_HEADS
    m_ref[...] = jnp.full(m_ref.shape, -1e30, F32)
    l_ref[...] = jnp.zeros(l_ref.shape, F32)
    acc_ref[...] = jnp.zeros(acc_ref.shape, F32)

    def chunk(c, carry):
        k0 = pl.multiple_of(c * kc, kc)
        k_c = k_ref[pl.ds(k0, kc), :]
        vt_c = vt_ref[:, pl.ds(k0, kc)]
        s = [_dot(k_c, qt_ref[hh * ATT_HEAD_DIM:(hh + 1) * ATT_HEAD_DIM, :])
             for hh in range(grp)]
        p = [None] * grp
        alpha = [None] * grp
        for hh in range(grp):
            m_old = m_ref[hh]
            m_new = jnp.maximum(m_old, jnp.max(s[hh], axis=0, keepdims=True))
            alpha[hh] = jnp.exp2(m_old - m_new)
            ph = jnp.exp2(s[hh] - m_new)
            l_ref[hh] = alpha[hh] * l_ref[hh] + jnp.sum(ph, axis=0, keepdims=True)
            m_ref[hh] = m_new
            p[hh] = ph.astype(BF16)
        for hh in range(grp):
            acc_ref[hh] = alpha[hh] * acc_ref[hh] + _dot(vt_c, p[hh])
        return carry

    lax.fori_loop(0, nkc, chunk, 0)
    for hh in range(grp):
        sl = slice(hh * ATT_HEAD_DIM, (hh + 1) * ATT_HEAD_DIM)
        o = (acc_ref[hh] / l_ref[hh]).T
        o_ref[:, sl] = (o * _silu(z_ref[:, sl].astype(F32))).astype(BF16)


def _att(qt, kr, vt, h):
    seq = kr.shape[0]
    tq = min(256, seq)
    kc = min(4096, seq)
    grp = ATT_HEADS // ATT_KV_HEADS
    gw = ATT_WIDTH // ATT_KV_HEADS
    return pl.pallas_call(
        functools.partial(_att_kernel, kc=kc, nkc=seq // kc),
        grid=(ATT_KV_HEADS, seq // tq),
        in_specs=[
            pl.BlockSpec((gw, tq), lambda g, i: (g, i)),
            pl.BlockSpec((seq, ATT_HEAD_DIM), lambda g, i: (0, g)),
            pl.BlockSpec((ATT_HEAD_DIM, seq), lambda g, i: (g, 0)),
            pl.BlockSpec((tq, gw), lambda g, i: (i, OFF_ZC // gw + g)),
        ],
        out_specs=pl.BlockSpec((tq, gw), lambda g, i: (i, g)),
        out_shape=jax.ShapeDtypeStruct((seq, ATT_WIDTH), BF16),
        scratch_shapes=[
            pltpu.VMEM((grp, 1, tq), F32),
            pltpu.VMEM((grp, 1, tq), F32),
            pltpu.VMEM((grp, ATT_HEAD_DIM, tq), F32),
        ],
        compiler_params=_cparams(("parallel", "parallel")),
        name="grid_att",
    )(qt, kr, vt, h)


def _mem_kv_kernel(mem_ref, g_ref, w_ref, o_ref):
    x = mem_ref[...]
    ms = jnp.mean(x * x, axis=-1, keepdims=True)
    xn = (x * lax.rsqrt(ms + EPS) * g_ref[...]).astype(BF16)
    o_ref[...] = _dot(xn, w_ref[...].astype(BF16)).astype(BF16)


def _mem_kv(mem, g, w_kv):
    n_mem = mem.shape[0]
    tn = 512
    return pl.pallas_call(
        _mem_kv_kernel,
        grid=(2 * MEM_WIDTH // tn,),
        in_specs=[
            pl.BlockSpec((n_mem, D_MODEL), lambda j: (0, 0)),
            pl.BlockSpec((1, D_MODEL), lambda j: (0, 0)),
            pl.BlockSpec((D_MODEL, tn), lambda j: (0, j)),
        ],
        out_specs=pl.BlockSpec((n_mem, tn), lambda j: (0, j)),
        out_shape=jax.ShapeDtypeStruct((n_mem, 2 * MEM_WIDTH), BF16),
        compiler_params=_cparams(("parallel",)),
        name="mem_kv",
    )(mem, g, w_kv)


def _mem_att_kernel(q_ref, z_ref, kv_ref, o_ref):
    scale = MEM_HEAD_DIM ** -0.5
    for hd in range(MEM_HEADS):
        sl = slice(hd * MEM_HEAD_DIM, (hd + 1) * MEM_HEAD_DIM)
        k = kv_ref[:, sl]
        v = kv_ref[:, MEM_WIDTH + hd * MEM_HEAD_DIM:MEM_WIDTH + (hd + 1) * MEM_HEAD_DIM]
        s = _dot_nt(q_ref[:, sl], k) * scale
        m = jnp.max(s, axis=-1, keepdims=True)
        p = jnp.exp(s - m)
        l = jnp.sum(p, axis=-1, keepdims=True)
        o = _dot(p.astype(BF16), v) / l
        o_ref[:, sl] = (o * _silu(z_ref[:, sl].astype(F32))).astype(BF16)


def _mem_att(h, kv):
    seq = h.shape[0]
    tm = min(1024, seq)
    n_mem = kv.shape[0]
    return pl.pallas_call(
        _mem_att_kernel,
        grid=(seq // tm,),
        in_specs=[
            pl.BlockSpec((tm, MEM_WIDTH), lambda i: (i, OFF_MQ // MEM_WIDTH)),
            pl.BlockSpec((tm, MEM_WIDTH), lambda i: (i, OFF_ZM // MEM_WIDTH)),
            pl.BlockSpec((n_mem, 2 * MEM_WIDTH), lambda i: (0, 0)),
        ],
        out_specs=pl.BlockSpec((tm, MEM_WIDTH), lambda i: (i, 0)),
        out_shape=jax.ShapeDtypeStruct((seq, MEM_WIDTH), BF16),
        compiler_params=_cparams(("parallel",)),
        name="mem_att",
    )(h, h, kv)


def _merge_kernel(ya_ref, yb_ref, yc_ref, ym_ref, ga_ref, gb_ref, gc_ref, gm_ref,
                  wa_ref, wb_ref, wc0_ref, wc1_ref, wm_ref, o_ref):
    half = ATT_WIDTH // 2
    ya_w = _dot(ya_ref[...], wa_ref[...])
    yb_w = _dot(yb_ref[...], wb_ref[...])
    yc_w = _dot(yc_ref[:, :half], wc0_ref[...]) + _dot(yc_ref[:, half:], wc1_ref[...])
    ym_w = _dot(ym_ref[...], wm_ref[...])
    merged = jax.nn.sigmoid(ga_ref[...].astype(F32)) * ya_w
    merged = merged + jax.nn.sigmoid(gb_ref[...].astype(F32)) * yb_w
    merged = merged + jax.nn.sigmoid(gc_ref[...].astype(F32)) * yc_w
    merged = merged + jax.nn.sigmoid(gm_ref[...].astype(F32)) * ym_w
    o_ref[...] = merged.astype(BF16)


def _outproj_kernel(m_ref, wo_ref, x_ref, fg_ref, o_ref, *, final_norm):
    y = x_ref[...] + _dot(m_ref[...], wo_ref[...])
    if final_norm:
        ms = jnp.mean(y * y, axis=-1, keepdims=True)
        y = y * lax.rsqrt(ms + EPS) * fg_ref[...]
    o_ref[...] = y


def _outproj(merged, w_out, x, final_g, layer, final_norm):
    seq = x.shape[0]
    tm = min(512, seq)
    return pl.pallas_call(
        functools.partial(_outproj_kernel, final_norm=final_norm),
        grid=(seq // tm,),
        in_specs=[
            pl.BlockSpec((tm, D_MODEL), lambda i: (i, 0)),
            pl.BlockSpec((None, D_MODEL, D_MODEL), lambda i: (layer, 0, 0)),
            pl.BlockSpec((tm, D_MODEL), lambda i: (i, 0)),
            pl.BlockSpec((1, D_MODEL), lambda i: (0, 0)),
        ],
        out_specs=pl.BlockSpec((tm, D_MODEL), lambda i: (i, 0)),
        out_shape=jax.ShapeDtypeStruct((seq, D_MODEL), F32),
        compiler_params=_cparams(("parallel",)),
        name="out_proj",
    )(merged, w_out, x, final_g)


def _merge(ys, h, w_branch, layer):
    seq = h.shape[0]
    tm = min(1024, seq)
    tn = 512
    widths = (SSM_WIDTH, DN_WIDTH, ATT_WIDTH, MEM_WIDTH)
    half = ATT_WIDTH // 2
    assert SSM_WIDTH == DN_WIDTH and (SSM_WIDTH + DN_WIDTH) % half == 0 and MEM_WIDTH == half
    c_blk = (SSM_WIDTH + DN_WIDTH) // half
    in_specs = [pl.BlockSpec((tm, wd), lambda i, j: (i, 0)) for wd in widths]
    for b in range(N_BRANCH):
        base = (OFF_GATE + b * D_MODEL) // tn
        in_specs.append(pl.BlockSpec((tm, tn), lambda i, j, base=base: (i, base + j)))
    in_specs += [
        pl.BlockSpec((None, SSM_WIDTH, tn), lambda i, j: (layer, 0, j)),
        pl.BlockSpec((None, DN_WIDTH, tn), lambda i, j: (layer, 1, j)),
        pl.BlockSpec((None, half, tn), lambda i, j: (layer, c_blk, j)),
        pl.BlockSpec((None, half, tn), lambda i, j: (layer, c_blk + 1, j)),
        pl.BlockSpec((None, half, tn), lambda i, j: (layer, c_blk + 2, j)),
    ]
    return pl.pallas_call(
        _merge_kernel,
        grid=(seq // tm, D_MODEL // tn),
        in_specs=in_specs,
        out_specs=pl.BlockSpec((tm, tn), lambda i, j: (i, j)),
        out_shape=jax.ShapeDtypeStruct((seq, D_MODEL), BF16),
        compiler_params=_cparams(("parallel", "parallel")),
        name="gate_merge",
    )(*ys, h, h, h, h, w_branch, w_branch, w_branch, w_branch, w_branch)


def _rope_tables(seq):
    rows = seq // GRID_W
    row = jnp.repeat(jnp.arange(rows), GRID_W).astype(F32)
    col = jnp.tile(jnp.arange(GRID_W), rows).astype(F32)
    axis_dim = ATT_HEAD_DIM // 2
    freqs = ROPE_THETA ** (-jnp.arange(0, axis_dim, 2, dtype=F32) / axis_dim)
    ang = jnp.concatenate([row[:, None] * freqs, col[:, None] * freqs], axis=-1)
    cos = jnp.cos(ang)
    sin = jnp.sin(ang)
    return (jnp.repeat(cos, 2, axis=-1),
            jnp.stack([-sin, sin], axis=-1).reshape(seq, ATT_HEAD_DIM))


def _pair_rows(p):
    return p.reshape(2, SSM_PAIRS, 1, 2 * SSM_STATE).transpose(1, 0, 2, 3)


def _stacked_weights(w_in, w_branch, w_out):
    w_main = jnp.concatenate([w_in[:, :, :IN_AB], w_in[:, :, IN_AB_END:]], axis=2).astype(BF16)
    w_ab = jnp.pad(w_in[:, :, IN_AB:IN_AB_END],
                   ((0, 0), (0, 0), (0, AB_PAD - (IN_AB_END - IN_AB)))).astype(BF16)
    return w_main, w_ab, w_branch.astype(BF16), w_out.astype(BF16)


def _layer_params(layer, ssm_a_re, ssm_a_im, ssm_log_step, ssm_b_re, ssm_b_im, ssm_c_re,
                  ssm_c_im, ssm_d, ssm_w_glu, ssm_b_glu, dn_conv, dn_a_log, dn_dt_bias,
                  dn_norm_g, attn_q_norm, attn_k_norm):
    ls_n = jnp.broadcast_to(ssm_log_step[layer][:, :, None], ssm_a_re[layer].shape)
    flat = lambda p: p.reshape(2, 1, SSM_GROUPS * SSM_STATE)

    def b_pairs(b):
        bt = b.transpose(0, 1, 3, 2).reshape(2, SSM_PAIRS, 2, SSM_GROUP, SSM_STATE)
        return bt.transpose(1, 0, 3, 2, 4).reshape(SSM_PAIRS, 2, SSM_GROUP, 2 * SSM_STATE)

    def c_pairs(c):
        ct = c.transpose(0, 1, 3, 2).reshape(2, SSM_PAIRS, 2 * SSM_STATE, SSM_GROUP)
        return ct.transpose(1, 0, 2, 3)

    conv = dn_conv[layer].T.reshape(DN_CONV, 3, DN_WIDTH).transpose(1, 0, 2)
    conv = jnp.pad(conv, ((0, 0), (0, 8 - DN_CONV), (0, 0)))

    def lane_vec(p):
        return jnp.pad(p.reshape(1, -1), ((0, 0), (0, AB_PAD - p.size)))

    return {
        "ssm_rows": [_pair_rows(ssm_a_re[layer]), _pair_rows(ssm_a_im[layer]), _pair_rows(ls_n)],
        "ssm_b": [b_pairs(ssm_b_re[layer]), b_pairs(ssm_b_im[layer])],
        "ssm_c": [c_pairs(ssm_c_re[layer]), c_pairs(ssm_c_im[layer])],
        "ssm_flat": [flat(ssm_a_re[layer]), flat(ssm_a_im[layer]), flat(ls_n)],
        "ssm_d": ssm_d[layer].reshape(1, SSM_WIDTH),
        "ssm_w_glu": ssm_w_glu[layer],
        "ssm_b_glu": ssm_b_glu[layer].reshape(1, SSM_WIDTH),
        "dn_conv": conv,
        "dn_alog": lane_vec(dn_a_log[layer]),
        "dn_dtb": lane_vec(dn_dt_bias[layer]),
        "dn_norm_g": dn_norm_g[layer].reshape(1, DN_HEAD_DIM),
        "att_gq": attn_q_norm[layer].reshape(1, ATT_HEAD_DIM),
        "att_gk": attn_k_norm[layer].reshape(1, ATT_HEAD_DIM),
    }


def kernel(x, mem, norm_g, w_in, ssm_a_re, ssm_a_im, ssm_log_step, ssm_b_re, ssm_b_im, ssm_c_re, ssm_c_im, ssm_d, ssm_w_glu, ssm_b_glu, dn_conv, dn_a_log, dn_dt_bias, dn_norm_g, attn_q_norm, attn_k_norm, mem_norm_g, w_mem_kv, w_branch, w_out, final_norm_g):
    bsz, seq, _ = x.shape
    depth = w_in.shape[0]
    cos2, sin2 = _rope_tables(seq)
    final_g = final_norm_g.reshape(1, D_MODEL)
    w_main, w_ab, w_br, w_o = _stacked_weights(w_in, w_branch, w_out)
    outs = []
    for b in range(bsz):
        xb = x[b]
        for layer in range(depth):
            lp = _layer_params(layer, ssm_a_re, ssm_a_im, ssm_log_step, ssm_b_re, ssm_b_im,
                               ssm_c_re, ssm_c_im, ssm_d, ssm_w_glu, ssm_b_glu, dn_conv, dn_a_log,
                               dn_dt_bias, dn_norm_g, attn_q_norm, attn_k_norm)
            h, ab = _inproj(xb, norm_g[layer].reshape(1, D_MODEL), w_main, w_ab, layer)
            y_a = _ssm_branch(h, lp)
            y_b = _dn_branch(h, ab, lp)
            qt, kr, vt = _att_prep(h, cos2, sin2, lp["att_gq"], lp["att_gk"])
            y_c = _att(qt, kr, vt, h)
            kv = _mem_kv(mem[b], mem_norm_g[layer].reshape(1, D_MODEL), w_mem_kv[layer])
            y_m = _mem_att(h, kv)
            merged = _merge((y_a, y_b, y_c, y_m), h, w_br, layer)
            xb = _outproj(merged, w_o, xb, final_g, layer, final_norm=(layer == depth - 1))
        outs.append(xb)
    return outs[0][None] if bsz == 1 else jnp.stack(outs, axis=0)
```

```python
import functools
import math

import numpy as np
import jax
import jax.numpy as jnp
from jax import lax
from jax.experimental import pallas as pl
from jax.experimental.pallas import tpu as pltpu

F32 = jnp.float32
BF16 = jnp.bfloat16

D_MODEL = 2048
GRID_W = 64
EPS = 1e-6

SSM_GROUP = 16
SSM_STATE = 64
SSM_GROUPS = 48
SSM_WIDTH = SSM_GROUPS * SSM_GROUP
SSM_T = 16
SSM_PAIRS = SSM_GROUPS // 2
SSM_CW = SSM_T * SSM_GROUP

DN_HEADS = 6
DN_HEAD_DIM = 128
DN_WIDTH = DN_HEADS * DN_HEAD_DIM
DN_CONV = 5
DN_CHUNK = 64
DN_GROUP = 256
DN_HB = 6

ATT_HEADS = 8
ATT_KV_HEADS = 2
ATT_HEAD_DIM = 128
ATT_WIDTH = ATT_HEADS * ATT_HEAD_DIM
ATT_KV_WIDTH = ATT_KV_HEADS * ATT_HEAD_DIM
ROPE_THETA = 10000.0

MEM_HEADS = 4
MEM_HEAD_DIM = 128
MEM_WIDTH = MEM_HEADS * MEM_HEAD_DIM

N_BRANCH = 4

OFF_UA = 0
OFF_ZA = 768
OFF_DQ = 1536
OFF_DK = 2304
OFF_DV = 3072
OFF_ZB = 3840
OFF_AQ = 4608
OFF_AK = 5632
OFF_AV = 5888
OFF_ZC = 6144
OFF_MQ = 7168
OFF_ZM = 7680
OFF_GATE = 8192
IN_AB = 3840
IN_AB_END = IN_AB + 4 * DN_HEADS
H_WIDTH = OFF_GATE + N_BRANCH * D_MODEL
AB_PAD = 128

VMEM_LIMIT = 56 * 1024 * 1024


def _cparams(sem):
    return pltpu.CompilerParams(dimension_semantics=sem, vmem_limit_bytes=VMEM_LIMIT)


def _silu(x):
    return x * jax.nn.sigmoid(x)


def _dot(a, b):
    return jnp.dot(a, b, preferred_element_type=F32)


def _dot_nt(a, b):
    return lax.dot_general(a, b, (((1,), (1,)), ((), ())), preferred_element_type=F32)


def _dot_tn(a, b):
    return lax.dot_general(a, b, (((0,), (0,)), ((), ())), preferred_element_type=F32)


def _split3(x):
    x1 = x.astype(BF16)
    r1 = x - x1.astype(F32)
    x2 = r1.astype(BF16)
    x3 = (r1 - x2.astype(F32)).astype(BF16)
    return x1, x2, x3


def _dot_exact_lhs(a_bf16, x):
    x1, x2, x3 = _split3(x)
    return _dot(a_bf16, x1) + _dot(a_bf16, x2) + _dot(a_bf16, x3)


def _dot_exact_rhs(x, b_bf16):
    x1, x2, x3 = _split3(x)
    return _dot(x1, b_bf16) + _dot(x2, b_bf16) + _dot(x3, b_bf16)


def _dot_f32(a, b):
    a1, a2, a3 = _split3(a)
    b1, b2, b3 = _split3(b)
    return (_dot(a1, b1) + (_dot(a1, b2) + _dot(a2, b1))
            + (_dot(a1, b3) + _dot(a2, b2) + _dot(a3, b1)))


def _inproj_kernel(x_ref, g_ref, w_ref, wab_ref, h_ref, ab_ref, xn_ref):
    @pl.when(pl.program_id(1) == 0)
    def _():
        x = x_ref[...]
        ms = jnp.mean(x * x, axis=-1, keepdims=True)
        xn_ref[...] = (x * lax.rsqrt(ms + EPS) * g_ref[...]).astype(BF16)
        ab_ref[...] = _dot(xn_ref[...], wab_ref[...])

    h_ref[...] = _dot(xn_ref[...], w_ref[...]).astype(BF16)


def _inproj(x, g, w_main, w_ab, layer):
    seq = x.shape[0]
    tm = min(1024, seq)
    tn = 2048
    return pl.pallas_call(
        _inproj_kernel,
        grid=(seq // tm, H_WIDTH // tn),
        in_specs=[
            pl.BlockSpec((tm, D_MODEL), lambda i, j: (i, 0)),
            pl.BlockSpec((1, D_MODEL), lambda i, j: (0, 0)),
            pl.BlockSpec((None, D_MODEL, tn), lambda i, j: (layer, 0, j)),
            pl.BlockSpec((None, D_MODEL, AB_PAD), lambda i, j: (layer, 0, 0)),
        ],
        out_specs=[
            pl.BlockSpec((tm, tn), lambda i, j: (i, j)),
            pl.BlockSpec((tm, AB_PAD), lambda i, j: (i, 0)),
        ],
        out_shape=[
            jax.ShapeDtypeStruct((seq, H_WIDTH), BF16),
            jax.ShapeDtypeStruct((seq, AB_PAD), F32),
        ],
        scratch_shapes=[pltpu.VMEM((tm, D_MODEL), BF16)],
        compiler_params=_cparams(("parallel", "arbitrary")),
        name="inproj",
    )(x, g, w_main, w_ab)


def _cpow_table(base_re, base_im, n):
    re = [jnp.ones_like(base_re)]
    im = [jnp.zeros_like(base_im)]
    for _ in range(n):
        re_n = re[-1] * base_re - im[-1] * base_im
        im_n = re[-1] * base_im + im[-1] * base_re
        re.append(re_n)
        im.append(im_n)
    return re, im


def _ssm_prep_kernel(are_r, aim_r, ls_r, bre_ref, bim_ref, cre_ref, cim_ref,
                     win_ref, wout_ref, m_ref):
    t = SSM_T
    cw = SSM_CW
    lane128 = lax.broadcasted_iota(jnp.int32, (1, 128), 1)
    col_t = lax.broadcasted_iota(jnp.int32, (1, cw), 1) // SSM_GROUP
    sub128 = lax.broadcasted_iota(jnp.int32, (128, 1), 0)
    lane_cw = lax.broadcasted_iota(jnp.int32, (SSM_GROUP, cw), 1)
    tile_p = jnp.where(lane_cw % SSM_GROUP == lax.broadcasted_iota(jnp.int32, (SSM_GROUP, cw), 0),
                       1.0, 0.0).astype(BF16)

    m_ref[...] = jnp.zeros(m_ref.shape, m_ref.dtype)

    krows = [[None, None], [None, None]]
    for d in range(2):
        step = jnp.exp(ls_r[d])
        a_re = are_r[d]
        a_im = aim_r[d]
        mag = jnp.exp(a_re * step)
        lam_re = mag * jnp.cos(a_im * step)
        lam_im = mag * jnp.sin(a_im * step)
        den = a_re * a_re + a_im * a_im
        nr = lam_re - 1.0
        ni = lam_im
        coef_re = (nr * a_re + ni * a_im) / den
        coef_im = (ni * a_re - nr * a_im) / den
        b_re = bre_ref[d]
        b_im = bim_ref[d]
        bbt_re = coef_re * b_re - coef_im * b_im
        bbt_im = coef_re * b_im + coef_im * b_re
        bb_re = jnp.concatenate([bbt_re] * t, axis=0)
        bb_im = jnp.concatenate([bbt_im] * t, axis=0)
        pr_re, pr_im = _cpow_table(lam_re, lam_im, t)
        e_in = [(t - 1 - s) if d == 0 else s for s in range(t)]
        p_re = jnp.concatenate([jnp.broadcast_to(pr_re[e], (SSM_GROUP, 128)) for e in e_in], axis=0)
        p_im = jnp.concatenate([jnp.broadcast_to(pr_im[e], (SSM_GROUP, 128)) for e in e_in], axis=0)
        w_re = p_re * bb_re - p_im * bb_im
        w_im = p_re * bb_im + p_im * bb_re
        for par in range(2):
            keep = (lane128 < 64) if par == 0 else (lane128 >= 64)
            win_ref[par * cw:(par + 1) * cw, (2 * d) * 128:(2 * d + 1) * 128] = (
                jnp.where(keep, w_re, 0.0).astype(BF16))
            win_ref[par * cw:(par + 1) * cw, (2 * d + 1) * 128:(2 * d + 2) * 128] = (
                jnp.where(keep, w_im, 0.0).astype(BF16))

        tab_re = jnp.zeros((128, 128), F32)
        tab_im = jnp.zeros((128, 128), F32)
        for e in range(t + 1):
            tab_re = jnp.where(sub128 == e, pr_re[e], tab_re)
            tab_im = jnp.where(sub128 == e, pr_im[e], tab_im)
        tab_re = tab_re.T
        tab_im = tab_im.T
        c_re = _dot_exact_rhs(cre_ref[d], tile_p)
        c_im = _dot_exact_rhs(cim_ref[d], tile_p)

        def c_lam(expo_row):
            sel = jnp.where(sub128 == expo_row, 1.0, 0.0).astype(BF16)
            q_re = _dot_exact_rhs(tab_re, sel)
            q_im = _dot_exact_rhs(tab_im, sel)
            return c_re * q_re - c_im * q_im, -(c_re * q_im + c_im * q_re)

        o_re, o_im = c_lam((col_t + 1) if d == 0 else (t - col_t))
        for par in range(2):
            keep = (sub128 < 64) if par == 0 else (sub128 >= 64)
            wout_ref[(2 * d) * 128:(2 * d + 1) * 128, par * cw:(par + 1) * cw] = (
                jnp.where(keep, o_re, 0.0).astype(BF16))
            wout_ref[(2 * d + 1) * 128:(2 * d + 2) * 128, par * cw:(par + 1) * cw] = (
                jnp.where(keep, o_im, 0.0).astype(BF16))

        r_re, r_im = c_lam(col_t if d == 0 else (t - 1 - col_t))
        for par in range(2):
            keep = (lane128 < 64) if par == 0 else (lane128 >= 64)
            krows[d][par] = (_dot_f32(jnp.where(keep, bbt_re, 0.0), r_re)
                             + _dot_f32(jnp.where(keep, bbt_im, 0.0), r_im))

    for par in range(2):
        kf = krows[0][par]
        kb = krows[1][par]
        for s in range(t):
            sh_f = SSM_GROUP * s
            blk = jnp.where(lane_cw >= sh_f, pltpu.roll(kf, sh_f, axis=1) if sh_f else kf, 0.0)
            sh_b = SSM_GROUP * (t - 1 - s)
            rolled_b = pltpu.roll(kb, cw - sh_b, axis=1) if sh_b else kb
            blk = blk + jnp.where(lane_cw < cw - sh_b, rolled_b, 0.0)
            m_ref[par * cw + s * SSM_GROUP:par * cw + (s + 1) * SSM_GROUP,
                  par * cw:(par + 1) * cw] = blk.astype(BF16)


def _ssm_prep(rows, b_t, c_t):
    pw = 2 * SSM_CW
    row_spec = pl.BlockSpec((None, 2, 1, 128), lambda g: (g, 0, 0, 0))
    b_spec = pl.BlockSpec((None, 2, SSM_GROUP, 128), lambda g: (g, 0, 0, 0))
    c_spec = pl.BlockSpec((None, 2, 128, SSM_GROUP), lambda g: (g, 0, 0, 0))
    w_spec = pl.BlockSpec((None, pw, pw), lambda g: (g, 0, 0))
    return pl.pallas_call(
        _ssm_prep_kernel,
        grid=(SSM_PAIRS,),
        in_specs=[row_spec] * 3 + [b_spec] * 2 + [c_spec] * 2,
        out_specs=[w_spec] * 3,
        out_shape=[jax.ShapeDtypeStruct((SSM_PAIRS, pw, pw), BF16)] * 3,
        compiler_params=_cparams(("parallel",)),
        name="ssm_prep",
    )(*rows, *b_t, *c_t)


SSM_BG = 8
SSM_BP = SSM_BG // 2
SSM_BW = SSM_BG * SSM_CW


def _ssm_place(t0):
    r = lax.broadcasted_iota(jnp.int32, (256, SSM_BW), 0)
    c = lax.broadcasted_iota(jnp.int32, (256, SSM_BW), 1)
    j = r % 128
    target = (j // SSM_GROUP) * SSM_CW + (t0 + r // 128) * SSM_GROUP + j % SSM_GROUP
    return jnp.where(c == target, 1.0, 0.0).astype(BF16)


def _ssm_in_kernel(u_ref, win_ref, u2_ref, h_ref, uf_ref):
    nchunk = u2_ref.shape[0]
    uf_ref[...] = u_ref[...].astype(F32)
    acc = jnp.zeros((nchunk, SSM_BW), F32)
    for t0 in range(0, SSM_T, 2):
        lhs = jnp.concatenate([uf_ref[pl.ds(t0, nchunk, stride=SSM_T), :],
                               uf_ref[pl.ds(t0 + 1, nchunk, stride=SSM_T), :]], axis=1)
        acc = acc + _dot(lhs.astype(BF16), _ssm_place(t0))
    u2_ref[...] = acc.astype(BF16)
    pw = 2 * SSM_CW
    for j in range(SSM_BP):
        h = _dot(u2_ref[:, j * pw:(j + 1) * pw], win_ref[j])
        for k in range(4):
            h_ref[k, :, j * 128:(j + 1) * 128] = h[:, k * 128:(k + 1) * 128]


def _ssm_in(h, win):
    seq = h.shape[0]
    nchunk = seq // SSM_T
    pw = 2 * SSM_CW
    nblk = SSM_GROUPS // SSM_BG
    return pl.pallas_call(
        _ssm_in_kernel,
        grid=(nblk,),
        in_specs=[
            pl.BlockSpec((seq, 128), lambda b: (0, OFF_UA // 128 + b)),
            pl.BlockSpec((SSM_BP, pw, pw), lambda b: (b, 0, 0)),
        ],
        out_specs=[
            pl.BlockSpec((nchunk, SSM_BW), lambda b: (0, b)),
            pl.BlockSpec((4, nchunk, SSM_BP * 128), lambda b: (0, 0, b)),
        ],
        out_shape=[
            jax.ShapeDtypeStruct((nchunk, SSM_GROUPS * SSM_CW), BF16),
            jax.ShapeDtypeStruct((4, nchunk, SSM_PAIRS * 128), F32),
        ],
        scratch_shapes=[pltpu.VMEM((seq, 128), F32)],
        compiler_params=_cparams(("parallel",)),
        name="ssm_in",
    )(h, win)


def _ssm_scan_kernel(are_ref, aim_ref, ls_ref, h_ref, p_ref, *, nchunk):
    width = h_ref.shape[2]

    def lam_pow_t(d):
        step = jnp.exp(ls_ref[d]) * float(SSM_T)
        mag = jnp.exp(are_ref[d] * step)
        ang = aim_ref[d] * step
        return mag * jnp.cos(ang), mag * jnp.sin(ang)

    lfr, lfi = lam_pow_t(0)
    lbr, lbi = lam_pow_t(1)

    def body(c, carry):
        fr, fi, br, bi = carry
        cb = nchunk - 1 - c
        p_ref[0, pl.ds(c, 1), :] = fr
        p_ref[1, pl.ds(c, 1), :] = fi
        p_ref[2, pl.ds(cb, 1), :] = br
        p_ref[3, pl.ds(cb, 1), :] = bi
        hfr = h_ref[0, pl.ds(c, 1), :]
        hfi = h_ref[1, pl.ds(c, 1), :]
        hbr = h_ref[2, pl.ds(cb, 1), :]
        hbi = h_ref[3, pl.ds(cb, 1), :]
        nfr = lfr * fr - lfi * fi + hfr
        nfi = lfr * fi + lfi * fr + hfi
        nbr = lbr * br - lbi * bi + hbr
        nbi = lbr * bi + lbi * br + hbi
        return nfr, nfi, nbr, nbi

    z = jnp.zeros((1, width), F32)
    lax.fori_loop(0, nchunk, body, (z, z, z, z))


def _ssm_scan(flat_params, h):
    nchunk = h.shape[1]
    width = h.shape[2]
    wt = 768
    return pl.pallas_call(
        functools.partial(_ssm_scan_kernel, nchunk=nchunk),
        grid=(width // wt,),
        in_specs=[pl.BlockSpec((2, 1, wt), lambda j: (0, 0, j))] * 3 + [
            pl.BlockSpec((4, nchunk, wt), lambda j: (0, 0, j)),
        ],
        out_specs=pl.BlockSpec((4, nchunk, wt), lambda j: (0, 0, j)),
        out_shape=jax.ShapeDtypeStruct(h.shape, F32),
        compiler_params=_cparams(("parallel",)),
        name="ssm_scan",
    )(*flat_params, h)


def _ssm_out_kernel(u2_ref, p_ref, m_ref, wout_ref, y_ref):
    nchunk = u2_ref.shape[0]
    pw = 2 * SSM_CW
    parts = []
    for j in range(SSM_BP):
        acc = _dot(u2_ref[:, j * pw:(j + 1) * pw], m_ref[j])
        for k in range(4):
            acc = acc + _dot(p_ref[k, :, j * 128:(j + 1) * 128].astype(BF16),
                             wout_ref[j, k * 128:(k + 1) * 128, :])
        parts.append(acc.astype(BF16))
    y16 = jnp.concatenate(parts, axis=1)
    for t0 in range(0, SSM_T, 2):
        yt = _dot_nt(y16, _ssm_place(t0))
        y_ref[pl.ds(t0, nchunk, stride=SSM_T), :] = yt[:, :128]
        y_ref[pl.ds(t0 + 1, nchunk, stride=SSM_T), :] = yt[:, 128:]


def _ssm_out(u2, p, m, wout):
    nchunk = u2.shape[0]
    seq = nchunk * SSM_T
    pw = 2 * SSM_CW
    nblk = SSM_GROUPS // SSM_BG
    return pl.pallas_call(
        _ssm_out_kernel,
        grid=(nblk,),
        in_specs=[
            pl.BlockSpec((nchunk, SSM_BW), lambda b: (0, b)),
            pl.BlockSpec((4, nchunk, SSM_BP * 128), lambda b: (0, 0, b)),
            pl.BlockSpec((SSM_BP, pw, pw), lambda b: (b, 0, 0)),
            pl.BlockSpec((SSM_BP, pw, pw), lambda b: (b, 0, 0)),
        ],
        out_specs=pl.BlockSpec((seq, 128), lambda b: (0, b)),
        out_shape=jax.ShapeDtypeStruct((seq, SSM_WIDTH), F32),
        compiler_params=_cparams(("parallel",)),
        name="ssm_out",
    )(u2, p, m, wout)


def _ssm_epi_kernel(y_ref, u_ref, z_ref, d_ref, wg_ref, bg_ref, o_ref):
    y = y_ref[...] + d_ref[...] * u_ref[...].astype(F32)
    y = jax.nn.gelu(y)
    glu = _dot(y.astype(BF16), wg_ref[...].astype(BF16)) + bg_ref[...]
    y = y * jax.nn.sigmoid(glu)
    o_ref[...] = (y * _silu(z_ref[...].astype(F32))).astype(BF16)


def _ssm_epi(y, h, d, w_glu, b_glu):
    seq = y.shape[0]
    tm = min(512, seq)
    w = SSM_WIDTH
    return pl.pallas_call(
        _ssm_epi_kernel,
        grid=(seq // tm,),
        in_specs=[
            pl.BlockSpec((tm, w), lambda i: (i, 0)),
            pl.BlockSpec((tm, w), lambda i: (i, OFF_UA // w)),
            pl.BlockSpec((tm, w), lambda i: (i, OFF_ZA // w)),
            pl.BlockSpec((1, w), lambda i: (0, 0)),
            pl.BlockSpec((w, w), lambda i: (0, 0)),
            pl.BlockSpec((1, w), lambda i: (0, 0)),
        ],
        out_specs=pl.BlockSpec((tm, w), lambda i: (i, 0)),
        out_shape=jax.ShapeDtypeStruct((seq, w), BF16),
        compiler_params=_cparams(("parallel",)),
        name="ssm_epi",
    )(y, h, h, d, w_glu, b_glu)


def _ssm_branch(h, lp):
    win, wout, m = _ssm_prep(lp["ssm_rows"], lp["ssm_b"], lp["ssm_c"])
    u2, hs = _ssm_in(h, win)
    p = _ssm_scan(lp["ssm_flat"], hs)
    y = _ssm_out(u2, p, m, wout)
    return _ssm_epi(y, h, lp["ssm_d"], lp["ssm_w_glu"], lp["ssm_b_glu"])


def _dn_prep_kernel(qc, qp, qn, kc, kp, kn, vc, vp, vn, ab_ref, cw_ref, alog_ref, dtb_ref,
                    qo_ref, ko_ref, vo_ref, sc_ref, sct_ref, ext_ref, *, tm, nblk):
    i = pl.program_id(0)
    halo = 16
    pad = DN_CONV // 2

    def conv_silu(cur, prev, nxt, part):
        ext_ref[0:halo, :] = jnp.where(i > 0, prev[...].astype(F32), 0.0)
        ext_ref[halo:halo + tm, :] = cur[...].astype(F32)
        ext_ref[halo + tm:halo + tm + halo, :] = jnp.where(i < nblk - 1, nxt[...].astype(F32), 0.0)
        acc = jnp.zeros((tm, DN_WIDTH), F32)
        for j in range(DN_CONV):
            acc = acc + cw_ref[part, j:j + 1, :] * ext_ref[pl.ds(halo - pad + j, tm), :]
        return _silu(acc)

    def l2n(x):
        outs = []
        for hd in range(DN_HEADS):
            xh = x[:, hd * DN_HEAD_DIM:(hd + 1) * DN_HEAD_DIM]
            outs.append(xh * lax.rsqrt(jnp.sum(xh * xh, axis=-1, keepdims=True) + EPS))
        return jnp.concatenate(outs, axis=1)

    qo_ref[...] = (l2n(conv_silu(qc, qp, qn, 0)) * (DN_HEAD_DIM ** -0.5)).astype(BF16)
    ko_ref[...] = l2n(conv_silu(kc, kp, kn, 1)).astype(BF16)
    vo_ref[...] = conv_silu(vc, vp, vn, 2).astype(BF16)

    ab = ab_ref[...]
    g_all = -jnp.exp(alog_ref[...]) * jax.nn.softplus(ab + dtb_ref[...])
    beta_all = jax.nn.sigmoid(ab)
    r = lax.broadcasted_iota(jnp.int32, (tm, tm), 0)
    c = lax.broadcasted_iota(jnp.int32, (tm, tm), 1)
    same = (r // DN_CHUNK) == (c // DN_CHUNK)
    tri_f = jnp.where(same & (c <= r), 1.0, 0.0).astype(BF16)
    tri_b = jnp.where(same & (c >= r), 1.0, 0.0).astype(BF16)
    blk = jnp.where(same, 1.0, 0.0).astype(BF16)
    gcf = _dot_exact_lhs(tri_f, g_all)
    gcb = _dot_exact_lhs(tri_b, g_all)
    gtot = _dot_exact_lhs(blk, g_all)
    lane = lax.broadcasted_iota(jnp.int32, (tm, AB_PAD), 1)
    sc = jnp.where(lane < 6, gcf,
                   jnp.where(lane < 12, gcb,
                             jnp.where(lane < 24, beta_all,
                                       jnp.where(lane < 36, pltpu.roll(gtot, 24, axis=1), 0.0))))
    sc_ref[...] = sc
    sct_ref[...] = sc.T


def _dn_prep(h, ab, conv_w, alog, dtb):
    seq = h.shape[0]
    tm = min(256, seq)
    nblk = seq // tm
    w = DN_WIDTH
    hb = tm // 16
    nh = seq // 16

    def cur(ci):
        return pl.BlockSpec((tm, w), lambda i: (i, ci))

    def prev(ci):
        return pl.BlockSpec((16, w), lambda i: (jnp.maximum(i * hb - 1, 0), ci))

    def nxt(ci):
        return pl.BlockSpec((16, w), lambda i: (jnp.minimum((i + 1) * hb, nh - 1), ci))

    in_specs = []
    for off in (OFF_DQ, OFF_DK, OFF_DV):
        ci = off // w
        in_specs += [cur(ci), prev(ci), nxt(ci)]
    in_specs += [
        pl.BlockSpec((tm, AB_PAD), lambda i: (i, 0)),
        pl.BlockSpec((3, 8, w), lambda i: (0, 0, 0)),
        pl.BlockSpec((1, AB_PAD), lambda i: (0, 0)),
        pl.BlockSpec((1, AB_PAD), lambda i: (0, 0)),
    ]
    return pl.pallas_call(
        functools.partial(_dn_prep_kernel, tm=tm, nblk=nblk),
        grid=(nblk,),
        in_specs=in_specs,
        out_specs=[pl.BlockSpec((tm, w), lambda i: (i, 0))] * 3
        + [pl.BlockSpec((tm, AB_PAD), lambda i: (i, 0)), pl.BlockSpec((AB_PAD, tm), lambda i: (0, i))],
        out_shape=[jax.ShapeDtypeStruct((seq, w), BF16)] * 3
        + [jax.ShapeDtypeStruct((seq, AB_PAD), F32), jax.ShapeDtypeStruct((AB_PAD, seq), F32)],
        scratch_shapes=[pltpu.VMEM((tm + 32, w), F32)],
        compiler_params=_cparams(("parallel",)),
        name="dn_prep",
    )(h, h, h, h, h, h, h, h, h, ab, conv_w, alog, dtb)


def _dn_main_kernel(qf_ref, kf_ref, vf_ref, scf_ref, sctf_ref,
                    qb_ref, kb_ref, vb_ref, scb_ref, sctb_ref, of_ref, ob_ref, s_ref):
    gsz = DN_GROUP
    nck = gsz // DN_CHUNK

    @pl.when(pl.program_id(1) == 0)
    def _():
        s_ref[...] = jnp.zeros(s_ref.shape, F32)

    refs = ((qf_ref, kf_ref, vf_ref, scf_ref, sctf_ref, of_ref),
            (qb_ref, kb_ref, vb_ref, scb_ref, sctb_ref, ob_ref))
    r = lax.broadcasted_iota(jnp.int32, (gsz, gsz), 0)
    c = lax.broadcasted_iota(jnp.int32, (gsz, gsz), 1)
    same = (r // DN_CHUNK) == (c // DN_CHUNK)
    incl = (same & (r >= c), same & (r <= c))
    strict = (same & (r > c), same & (r < c))

    chains = []
    for d in range(2):
        q_ref, k_ref, v_ref, sc_ref, sct_ref, o_ref = refs[d]
        sc = sc_ref[...]
        sct = sct_ref[...]
        lane = lax.broadcasted_iota(jnp.int32, sc.shape, 1)
        sub = lax.broadcasted_iota(jnp.int32, sct.shape, 0)
        for hl in range(DN_HB):
            cidx = d * DN_HEADS + pl.program_id(0) * DN_HB + hl
            hs = slice(hl * DN_HEAD_DIM, (hl + 1) * DN_HEAD_DIM)

            def col(ci, sc=sc, lane=lane):
                return jnp.sum(jnp.where(lane == ci, sc, 0.0), axis=1, keepdims=True)

            ch = {"d": d, "hl": hl, "hs": hs, "o_ref": o_ref}
            ch["q"] = q_ref[:, hs]
            ch["k"] = k_ref[:, hs]
            ch["v"] = v_ref[:, hs]
            ch["gc_col"] = col(cidx)
            ch["gc_row"] = jnp.sum(jnp.where(sub == cidx, sct, 0.0), axis=0, keepdims=True)
            ch["beta_col"] = col(12 + cidx)
            ch["gtot_col"] = col(24 + cidx)
            chains.append(ch)

    for ch in chains:
        ch["kf"] = ch["k"].astype(F32)
        ch["kb"] = ch["kf"] * ch["beta_col"]
        ch["gram"] = _dot_nt(ch["kb"].astype(BF16), ch["k"])
    for ch in chains:
        ch["qk"] = _dot_nt(ch["q"], ch["k"])
    for ch in chains:
        d = ch["d"]
        ch["decay"] = jnp.where(incl[d], jnp.exp(jnp.where(incl[d], ch["gc_col"] - ch["gc_row"], 0.0)), 0.0)
        ch["x"] = jnp.concatenate([ch["v"].astype(F32) * ch["beta_col"],
                                   ch["kb"] * jnp.exp(ch["gc_col"])], axis=1)
        ch["qd16"] = (ch["q"].astype(F32) * jnp.exp(ch["gc_col"])).astype(BF16)
        ch["kd16"] = (ch["kf"] * jnp.exp(ch["gtot_col"] - ch["gc_col"])).astype(BF16)
    for ch in chains:
        ch["p"] = jnp.where(strict[ch["d"]], ch["gram"] * ch["decay"], 0.0).astype(BF16)
        ch["intra"] = (ch["qk"] * ch["decay"]).astype(BF16)
    for ch in chains:
        ch["x"] = ch["x"] - _dot(ch["p"], ch["x"].astype(BF16))
    npow = 2
    while npow < DN_CHUNK:
        for ch in chains:
            ch["p"] = _dot(ch["p"], ch["p"]).astype(BF16)
        for ch in chains:
            ch["x"] = ch["x"] + _dot(ch["p"], ch["x"].astype(BF16))
        npow *= 2
    for ch in chains:
        ch["u"] = ch["x"][:, :DN_HEAD_DIM]
        ch["w16"] = ch["x"][:, DN_HEAD_DIM:].astype(BF16)
        ch["s"] = s_ref[ch["d"], ch["hl"]]
        ch["vnew"] = [None] * nck
        ch["oq"] = [None] * nck

    for step in range(nck):
        for ch in chains:
            j = step if ch["d"] == 0 else nck - 1 - step
            lo, hi = j * DN_CHUNK, (j + 1) * DN_CHUNK
            s16 = ch["s"].astype(BF16)
            ws = _dot(jnp.concatenate([ch["w16"][lo:hi], ch["qd16"][lo:hi]], axis=0), s16)
            vn = ch["u"][lo:hi] - ws[:DN_CHUNK]
            ch["oq"][j] = ws[DN_CHUNK:]
            ch["vnew"][j] = vn
            ch["s"] = (ch["s"] * jnp.exp(ch["gtot_col"][lo:lo + 1, :])
                       + _dot_tn(ch["kd16"][lo:hi], vn.astype(BF16)))
    for ch in chains:
        s_ref[ch["d"], ch["hl"]] = ch["s"]
        vn_all = jnp.concatenate(ch["vnew"], axis=0).astype(BF16)
        ch["o_ref"][:, ch["hs"]] = jnp.concatenate(ch["oq"], axis=0) + _dot(ch["intra"], vn_all)


def _dn_main(qn, kn, vc, sc, sct):
    seq = qn.shape[0]
    gsz = DN_GROUP
    ng = seq // gsz
    hw = DN_HB * DN_HEAD_DIM
    sct_rows = sct.shape[0]
    fwd = lambda h, i: (i, h)
    bwd = lambda h, i: (ng - 1 - i, h)
    in_specs = [
        pl.BlockSpec((gsz, hw), fwd), pl.BlockSpec((gsz, hw), fwd), pl.BlockSpec((gsz, hw), fwd),
        pl.BlockSpec((gsz, AB_PAD), lambda h, i: (i, 0)),
        pl.BlockSpec((sct_rows, gsz), lambda h, i: (0, i)),
        pl.BlockSpec((gsz, hw), bwd), pl.BlockSpec((gsz, hw), bwd), pl.BlockSpec((gsz, hw), bwd),
        pl.BlockSpec((gsz, AB_PAD), lambda h, i: (ng - 1 - i, 0)),
        pl.BlockSpec((sct_rows, gsz), lambda h, i: (0, ng - 1 - i)),
    ]
    return pl.pallas_call(
        _dn_main_kernel,
        grid=(DN_HEADS // DN_HB, ng),
        in_specs=in_specs,
        out_specs=[pl.BlockSpec((gsz, hw), fwd), pl.BlockSpec((gsz, hw), bwd)],
        out_shape=[jax.ShapeDtypeStruct((seq, DN_WIDTH), F32)] * 2,
        scratch_shapes=[pltpu.VMEM((2, DN_HB, DN_HEAD_DIM, DN_HEAD_DIM), F32)],
        compiler_params=_cparams(("parallel", "arbitrary")),
        name="dn_main",
    )(qn, kn, vc, sc, sct, qn, kn, vc, sc, sct)


def _dn_epi_kernel(of_ref, ob_ref, z_ref, g_ref, o_ref):
    o = of_ref[...] + ob_ref[...]
    g = g_ref[...]
    outs = []
    for hd in range(DN_HEADS):
        oh = o[:, hd * DN_HEAD_DIM:(hd + 1) * DN_HEAD_DIM]
        ms = jnp.mean(oh * oh, axis=-1, keepdims=True)
        outs.append(oh * lax.rsqrt(ms + EPS) * g)
    y = jnp.concatenate(outs, axis=1)
    o_ref[...] = (y * _silu(z_ref[...].astype(F32))).astype(BF16)


def _dn_epi(o_f, o_b, h, norm_g):
    seq = o_f.shape[0]
    tm = min(512, seq)
    w = DN_WIDTH
    return pl.pallas_call(
        _dn_epi_kernel,
        grid=(seq // tm,),
        in_specs=[
            pl.BlockSpec((tm, w), lambda i: (i, 0)),
            pl.BlockSpec((tm, w), lambda i: (i, 0)),
            pl.BlockSpec((tm, w), lambda i: (i, OFF_ZB // w)),
            pl.BlockSpec((1, DN_HEAD_DIM), lambda i: (0, 0)),
        ],
        out_specs=pl.BlockSpec((tm, w), lambda i: (i, 0)),
        out_shape=jax.ShapeDtypeStruct((seq, w), BF16),
        compiler_params=_cparams(("parallel",)),
        name="dn_epi",
    )(o_f, o_b, h, norm_g)


def _dn_branch(h, ab, lp):
    qn, kn, vc, sc, sct = _dn_prep(h, ab, lp["dn_conv"], lp["dn_alog"], lp["dn_dtb"])
    o_f, o_b = _dn_main(qn, kn, vc, sc, sct)
    return _dn_epi(o_f, o_b, h, lp["dn_norm_g"])


def _att_prep_kernel(qlo_ref, qhi_ref, k_ref, v_ref, cos_ref, sin_ref, gq_ref, gk_ref,
                     qt_ref, ko_ref, vt_ref):
    cos2 = cos_ref[...]
    sin2 = sin_ref[...]
    even = lax.broadcasted_iota(jnp.int32, cos2.shape, 1) % 2 == 0

    def norm_rope(x, g):
        ms = jnp.mean(x * x, axis=-1, keepdims=True)
        xn = x * lax.rsqrt(ms + EPS) * g
        partner = jnp.where(even, pltpu.roll(xn, ATT_HEAD_DIM - 1, axis=1), pltpu.roll(xn, 1, axis=1))
        return xn * cos2 + partner * sin2

    gq = gq_ref[...]
    gk = gk_ref[...]
    scale = ATT_HEAD_DIM ** -0.5 * math.log2(math.e)
    half = ATT_HEADS // 2
    for hd in range(ATT_HEADS):
        src = qlo_ref if hd < half else qhi_ref
        ssl = slice((hd % half) * ATT_HEAD_DIM, (hd % half + 1) * ATT_HEAD_DIM)
        sl = slice(hd * ATT_HEAD_DIM, (hd + 1) * ATT_HEAD_DIM)
        qt_ref[sl, :] = (norm_rope(src[:, ssl].astype(F32), gq) * scale).T.astype(BF16)
    for hd in range(ATT_KV_HEADS):
        sl = slice(hd * ATT_HEAD_DIM, (hd + 1) * ATT_HEAD_DIM)
        ko_ref[:, sl] = norm_rope(k_ref[:, sl].astype(F32), gk).astype(BF16)
        vt_ref[sl, :] = v_ref[:, sl].astype(F32).T.astype(BF16)


def _att_prep(h, cos2, sin2, gq, gk):
    seq = h.shape[0]
    tm = min(512, seq)
    return pl.pallas_call(
        _att_prep_kernel,
        grid=(seq // tm,),
        in_specs=[
            pl.BlockSpec((tm, ATT_WIDTH // 2), lambda i: (i, OFF_AQ // (ATT_WIDTH // 2))),
            pl.BlockSpec((tm, ATT_WIDTH // 2), lambda i: (i, OFF_AQ // (ATT_WIDTH // 2) + 1)),
            pl.BlockSpec((tm, ATT_KV_WIDTH), lambda i: (i, OFF_AK // ATT_KV_WIDTH)),
            pl.BlockSpec((tm, ATT_KV_WIDTH), lambda i: (i, OFF_AV // ATT_KV_WIDTH)),
            pl.BlockSpec((tm, ATT_HEAD_DIM), lambda i: (i, 0)),
            pl.BlockSpec((tm, ATT_HEAD_DIM), lambda i: (i, 0)),
            pl.BlockSpec((1, ATT_HEAD_DIM), lambda i: (0, 0)),
            pl.BlockSpec((1, ATT_HEAD_DIM), lambda i: (0, 0)),
        ],
        out_specs=[
            pl.BlockSpec((ATT_WIDTH, tm), lambda i: (0, i)),
            pl.BlockSpec((tm, ATT_KV_WIDTH), lambda i: (i, 0)),
            pl.BlockSpec((ATT_KV_WIDTH, tm), lambda i: (0, i)),
        ],
        out_shape=[
            jax.ShapeDtypeStruct((ATT_WIDTH, seq), BF16),
            jax.ShapeDtypeStruct((seq, ATT_KV_WIDTH), BF16),
            jax.ShapeDtypeStruct((ATT_KV_WIDTH, seq), BF16),
        ],
        compiler_params=_cparams(("parallel",)),
        name="att_prep",
    )(h, h, h, h, cos2, sin2, gq, gk)


def _att_kernel(qt_ref, k_ref, vt_ref, z_ref, o_ref, m_ref, l_ref, acc_ref, *, kc, nkc):
    grp = qt_ref.shape[0] // ATT_HEAD_DIM
    m_ref[...] = jnp.full(m_ref.shape, -1e30, F32)
    l_ref[...] = jnp.zeros(l_ref.shape, F32)
    acc_ref[...] = jnp.zeros(acc_ref.shape, F32)

    def chunk(c, carry):
        k0 = pl.multiple_of(c * kc, kc)
        k_c = k_ref[pl.ds(k0, kc), :]
        vt_c = vt_ref[:, pl.ds(k0, kc)]
        s = [_dot(k_c, qt_ref[hh * ATT_HEAD_DIM:(hh + 1) * ATT_HEAD_DIM, :])
             for hh in range(grp)]
        p = [None] * grp
        alpha = [None] * grp
        for hh in range(grp):
            m_old = m_ref[hh]
            m_new = jnp.maximum(m_old, jnp.max(s[hh], axis=0, keepdims=True))
            alpha[hh] = jnp.exp2(m_old - m_new)
            ph = jnp.exp2(s[hh] - m_new)
            l_ref[hh] = alpha[hh] * l_ref[hh] + jnp.sum(ph, axis=0, keepdims=True)
            m_ref[hh] = m_new
            p[hh] = ph.astype(BF16)
        for hh in range(grp):
            acc_ref[hh] = alpha[hh] * acc_ref[hh] + _dot(vt_c, p[hh])
        return carry

    lax.fori_loop(0, nkc, chunk, 0)
    for hh in range(grp):
        sl = slice(hh * ATT_HEAD_DIM, (hh + 1) * ATT_HEAD_DIM)
        o = (acc_ref[hh] / l_ref[hh]).T
        o_ref[:, sl] = (o * _silu(z_ref[:, sl].astype(F32))).astype(BF16)


ATT_HPS = 4


def _att(qt, kr, vt, h):
    seq = kr.shape[0]
    tq = min(256, seq)
    kc = min(4096, seq)
    hw = ATT_HPS * ATT_HEAD_DIM
    steps_per_kv = ATT_HEADS // ATT_KV_HEADS // ATT_HPS
    return pl.pallas_call(
        functools.partial(_att_kernel, kc=kc, nkc=seq // kc),
        grid=(ATT_HEADS // ATT_HPS, seq // tq),
        in_specs=[
            pl.BlockSpec((hw, tq), lambda a, i: (a, i)),
            pl.BlockSpec((seq, ATT_HEAD_DIM), lambda a, i: (0, a // steps_per_kv)),
            pl.BlockSpec((ATT_HEAD_DIM, seq), lambda a, i: (a // steps_per_kv, 0)),
            pl.BlockSpec((tq, hw), lambda a, i: (i, OFF_ZC // hw + a)),
        ],
        out_specs=pl.BlockSpec((tq, hw), lambda a, i: (i, a)),
        out_shape=jax.ShapeDtypeStruct((seq, ATT_WIDTH), BF16),
        scratch_shapes=[
            pltpu.VMEM((ATT_HPS, 1, tq), F32),
            pltpu.VMEM((ATT_HPS, 1, tq), F32),
            pltpu.VMEM((ATT_HPS, ATT_HEAD_DIM, tq), F32),
        ],
        compiler_params=_cparams(("parallel", "parallel")),
        name="grid_att",
    )(qt, kr, vt, h)


def _mem_kv_kernel(mem_ref, g_ref, w_ref, o_ref):
    x = mem_ref[...]
    ms = jnp.mean(x * x, axis=-1, keepdims=True)
    xn = (x * lax.rsqrt(ms + EPS) * g_ref[...]).astype(BF16)
    o_ref[...] = _dot(xn, w_ref[...].astype(BF16)).astype(BF16)


def _mem_kv(mem, g, w_kv):
    n_mem = mem.shape[0]
    tn = 512
    return pl.pallas_call(
        _mem_kv_kernel,
        grid=(2 * MEM_WIDTH // tn,),
        in_specs=[
            pl.BlockSpec((n_mem, D_MODEL), lambda j: (0, 0)),
            pl.BlockSpec((1, D_MODEL), lambda j: (0, 0)),
            pl.BlockSpec((D_MODEL, tn), lambda j: (0, j)),
        ],
        out_specs=pl.BlockSpec((n_mem, tn), lambda j: (0, j)),
        out_shape=jax.ShapeDtypeStruct((n_mem, 2 * MEM_WIDTH), BF16),
        compiler_params=_cparams(("parallel",)),
        name="mem_kv",
    )(mem, g, w_kv)


def _mem_att_kernel(q_ref, z_ref, kv_ref, o_ref):
    scale = MEM_HEAD_DIM ** -0.5
    for hd in range(MEM_HEADS):
        sl = slice(hd * MEM_HEAD_DIM, (hd + 1) * MEM_HEAD_DIM)
        k = kv_ref[:, sl]
        v = kv_ref[:, MEM_WIDTH + hd * MEM_HEAD_DIM:MEM_WIDTH + (hd + 1) * MEM_HEAD_DIM]
        s = _dot_nt(q_ref[:, sl], k) * scale
        m = jnp.max(s, axis=-1, keepdims=True)
        p = jnp.exp(s - m)
        l = jnp.sum(p, axis=-1, keepdims=True)
        o = _dot(p.astype(BF16), v) / l
        o_ref[:, sl] = (o * _silu(z_ref[:, sl].astype(F32))).astype(BF16)


def _mem_att(h, kv):
    seq = h.shape[0]
    tm = min(1024, seq)
    n_mem = kv.shape[0]
    return pl.pallas_call(
        _mem_att_kernel,
        grid=(seq // tm,),
        in_specs=[
            pl.BlockSpec((tm, MEM_WIDTH), lambda i: (i, OFF_MQ // MEM_WIDTH)),
            pl.BlockSpec((tm, MEM_WIDTH), lambda i: (i, OFF_ZM // MEM_WIDTH)),
            pl.BlockSpec((n_mem, 2 * MEM_WIDTH), lambda i: (0, 0)),
        ],
        out_specs=pl.BlockSpec((tm, MEM_WIDTH), lambda i: (i, 0)),
        out_shape=jax.ShapeDtypeStruct((seq, MEM_WIDTH), BF16),
        compiler_params=_cparams(("parallel",)),
        name="mem_att",
    )(h, h, kv)


def _merge_kernel(ya_ref, yb_ref, yc_ref, ym_ref, ga_ref, gb_ref, gc_ref, gm_ref,
                  wa_ref, wb_ref, wc0_ref, wc1_ref, wm_ref, o_ref):
    half = ATT_WIDTH // 2
    ya_w = _dot(ya_ref[...], wa_ref[...])
    yb_w = _dot(yb_ref[...], wb_ref[...])
    yc_w = _dot(yc_ref[:, :half], wc0_ref[...]) + _dot(yc_ref[:, half:], wc1_ref[...])
    ym_w = _dot(ym_ref[...], wm_ref[...])
    merged = jax.nn.sigmoid(ga_ref[...].astype(F32)) * ya_w
    merged = merged + jax.nn.sigmoid(gb_ref[...].astype(F32)) * yb_w
    merged = merged + jax.nn.sigmoid(gc_ref[...].astype(F32)) * yc_w
    merged = merged + jax.nn.sigmoid(gm_ref[...].astype(F32)) * ym_w
    o_ref[...] = merged.astype(BF16)


def _outproj_kernel(m_ref, wo_ref, x_ref, fg_ref, o_ref, *, final_norm):
    y = x_ref[...] + _dot(m_ref[...], wo_ref[...])
    if final_norm:
        ms = jnp.mean(y * y, axis=-1, keepdims=True)
        y = y * lax.rsqrt(ms + EPS) * fg_ref[...]
    o_ref[...] = y


def _outproj(merged, w_out, x, final_g, layer, final_norm):
    seq = x.shape[0]
    tm = min(512, seq)
    return pl.pallas_call(
        functools.partial(_outproj_kernel, final_norm=final_norm),
        grid=(seq // tm,),
        in_specs=[
            pl.BlockSpec((tm, D_MODEL), lambda i: (i, 0)),
            pl.BlockSpec((None, D_MODEL, D_MODEL), lambda i: (layer, 0, 0)),
            pl.BlockSpec((tm, D_MODEL), lambda i: (i, 0)),
            pl.BlockSpec((1, D_MODEL), lambda i: (0, 0)),
        ],
        out_specs=pl.BlockSpec((tm, D_MODEL), lambda i: (i, 0)),
        out_shape=jax.ShapeDtypeStruct((seq, D_MODEL), F32),
        compiler_params=_cparams(("parallel",)),
        name="out_proj",
    )(merged, w_out, x, final_g)


def _merge(ys, h, w_branch, layer):
    seq = h.shape[0]
    tm = min(1024, seq)
    tn = 512
    widths = (SSM_WIDTH, DN_WIDTH, ATT_WIDTH, MEM_WIDTH)
    half = ATT_WIDTH // 2
    assert SSM_WIDTH == DN_WIDTH and (SSM_WIDTH + DN_WIDTH) % half == 0 and MEM_WIDTH == half
    c_blk = (SSM_WIDTH + DN_WIDTH) // half
    in_specs = [pl.BlockSpec((tm, wd), lambda i, j: (i, 0)) for wd in widths]
    for b in range(N_BRANCH):
        base = (OFF_GATE + b * D_MODEL) // tn
        in_specs.append(pl.BlockSpec((tm, tn), lambda i, j, base=base: (i, base + j)))
    in_specs += [
        pl.BlockSpec((None, SSM_WIDTH, tn), lambda i, j: (layer, 0, j)),
        pl.BlockSpec((None, DN_WIDTH, tn), lambda i, j: (layer, 1, j)),
        pl.BlockSpec((None, half, tn), lambda i, j: (layer, c_blk, j)),
        pl.BlockSpec((None, half, tn), lambda i, j: (layer, c_blk + 1, j)),
        pl.BlockSpec((None, half, tn), lambda i, j: (layer, c_blk + 2, j)),
    ]
    return pl.pallas_call(
        _merge_kernel,
        grid=(seq // tm, D_MODEL // tn),
        in_specs=in_specs,
        out_specs=pl.BlockSpec((tm, tn), lambda i, j: (i, j)),
        out_shape=jax.ShapeDtypeStruct((seq, D_MODEL), BF16),
        compiler_params=_cparams(("parallel", "parallel")),
        name="gate_merge",
    )(*ys, h, h, h, h, w_branch, w_branch, w_branch, w_branch, w_branch)


def _rope_tables(seq):
    rows = seq // GRID_W
    row = np.repeat(np.arange(rows), GRID_W).astype(np.float64)
    col = np.tile(np.arange(GRID_W), rows).astype(np.float64)
    axis_dim = ATT_HEAD_DIM // 2
    freqs = ROPE_THETA ** (-np.arange(0, axis_dim, 2, dtype=np.float64) / axis_dim)
    ang = np.concatenate([row[:, None] * freqs, col[:, None] * freqs], axis=-1)
    cos = np.cos(ang).astype(np.float32)
    sin = np.sin(ang).astype(np.float32)
    return (jnp.asarray(np.repeat(cos, 2, axis=-1)),
            jnp.asarray(np.stack([-sin, sin], axis=-1).reshape(seq, ATT_HEAD_DIM)))


def _pair_rows(p):
    return p.reshape(2, SSM_PAIRS, 1, 2 * SSM_STATE).transpose(1, 0, 2, 3)


def _stacked_weights(w_in, w_branch, w_out):
    w_main = jnp.concatenate([w_in[:, :, :IN_AB], w_in[:, :, IN_AB_END:]], axis=2).astype(BF16)
    w_ab = jnp.pad(w_in[:, :, IN_AB:IN_AB_END],
                   ((0, 0), (0, 0), (0, AB_PAD - (IN_AB_END - IN_AB)))).astype(BF16)
    return w_main, w_ab, w_branch.astype(BF16), w_out.astype(BF16)


def _layer_params(layer, ssm_a_re, ssm_a_im, ssm_log_step, ssm_b_re, ssm_b_im, ssm_c_re,
                  ssm_c_im, ssm_d, ssm_w_glu, ssm_b_glu, dn_conv, dn_a_log, dn_dt_bias,
                  dn_norm_g, attn_q_norm, attn_k_norm):
    ls_n = jnp.broadcast_to(ssm_log_step[layer][:, :, None], ssm_a_re[layer].shape)
    flat = lambda p: p.reshape(2, 1, SSM_GROUPS * SSM_STATE)

    def b_pairs(b):
        bt = b.transpose(0, 1, 3, 2).reshape(2, SSM_PAIRS, 2, SSM_GROUP, SSM_STATE)
        return bt.transpose(1, 0, 3, 2, 4).reshape(SSM_PAIRS, 2, SSM_GROUP, 2 * SSM_STATE)

    def c_pairs(c):
        ct = c.transpose(0, 1, 3, 2).reshape(2, SSM_PAIRS, 2 * SSM_STATE, SSM_GROUP)
        return ct.transpose(1, 0, 2, 3)

    conv = dn_conv[layer].T.reshape(DN_CONV, 3, DN_WIDTH).transpose(1, 0, 2)
    conv = jnp.pad(conv, ((0, 0), (0, 8 - DN_CONV), (0, 0)))

    def lane_vec(p):
        return jnp.pad(p.reshape(1, -1), ((0, 0), (0, AB_PAD - p.size)))

    return {
        "ssm_rows": [_pair_rows(ssm_a_re[layer]), _pair_rows(ssm_a_im[layer]), _pair_rows(ls_n)],
        "ssm_b": [b_pairs(ssm_b_re[layer]), b_pairs(ssm_b_im[layer])],
        "ssm_c": [c_pairs(ssm_c_re[layer]), c_pairs(ssm_c_im[layer])],
        "ssm_flat": [flat(ssm_a_re[layer]), flat(ssm_a_im[layer]), flat(ls_n)],
        "ssm_d": ssm_d[layer].reshape(1, SSM_WIDTH),
        "ssm_w_glu": ssm_w_glu[layer],
        "ssm_b_glu": ssm_b_glu[layer].reshape(1, SSM_WIDTH),
        "dn_conv": conv,
        "dn_alog": lane_vec(dn_a_log[layer]),
        "dn_dtb": lane_vec(dn_dt_bias[layer]),
        "dn_norm_g": dn_norm_g[layer].reshape(1, DN_HEAD_DIM),
        "att_gq": attn_q_norm[layer].reshape(1, ATT_HEAD_DIM),
        "att_gk": attn_k_norm[layer].reshape(1, ATT_HEAD_DIM),
    }


def kernel(x, mem, norm_g, w_in, ssm_a_re, ssm_a_im, ssm_log_step, ssm_b_re, ssm_b_im, ssm_c_re, ssm_c_im, ssm_d, ssm_w_glu, ssm_b_glu, dn_conv, dn_a_log, dn_dt_bias, dn_norm_g, attn_q_norm, attn_k_norm, mem_norm_g, w_mem_kv, w_branch, w_out, final_norm_g):
    bsz, seq, _ = x.shape
    depth = w_in.shape[0]
    cos2, sin2 = _rope_tables(seq)
    final_g = final_norm_g.reshape(1, D_MODEL)
    w_main, w_ab, w_br, w_o = _stacked_weights(w_in, w_branch, w_out)
    outs = []
    for b in range(bsz):
        xb = x[b]
        for layer in range(depth):
            lp = _layer_params(layer, ssm_a_re, ssm_a_im, ssm_log_step, ssm_b_re, ssm_b_im,
                               ssm_c_re, ssm_c_im, ssm_d, ssm_w_glu, ssm_b_glu, dn_conv, dn_a_log,
                               dn_dt_bias, dn_norm_g, attn_q_norm, attn_k_norm)
            h, ab = _inproj(xb, norm_g[layer].reshape(1, D_MODEL), w_main, w_ab, layer)
            y_a = _ssm_branch(h, lp)
            y_b = _dn_branch(h, ab, lp)
            qt, kr, vt = _att_prep(h, cos2, sin2, lp["att_gq"], lp["att_gk"])
            y_c = _att(qt, kr, vt, h)
            kv = _mem_kv(mem[b], mem_norm_g[layer].reshape(1, D_MODEL), w_mem_kv[layer])
            y_m = _mem_att(h, kv)
            merged = _merge((y_a, y_b, y_c, y_m), h, w_br, layer)
            xb = _outproj(merged, w_o, xb, final_g, layer, final_norm=(layer == depth - 1))
        outs.append(xb)
    return outs[0][None] if bsz == 1 else jnp.stack(outs, axis=0)
```

```python
import functools
import math

import numpy as np
import jax
import jax.numpy as jnp
from jax import lax
from jax.experimental import pallas as pl
from jax.experimental.pallas import tpu as pltpu

F32 = jnp.float32
BF16 = jnp.bfloat16

D_MODEL = 2048
GRID_W = 64
EPS = 1e-6

SSM_GROUP = 16
SSM_STATE = 64
SSM_GROUPS = 48
SSM_WIDTH = SSM_GROUPS * SSM_GROUP
SSM_T = 16
SSM_PAIRS = SSM_GROUPS // 2
SSM_CW = SSM_T * SSM_GROUP

DN_HEADS = 6
DN_HEAD_DIM = 128
DN_WIDTH = DN_HEADS * DN_HEAD_DIM
DN_CONV = 5
DN_CHUNK = 64
DN_GROUP = 256
DN_HB = 6

ATT_HEADS = 8
ATT_KV_HEADS = 2
ATT_HEAD_DIM = 128
ATT_WIDTH = ATT_HEADS * ATT_HEAD_DIM
ATT_KV_WIDTH = ATT_KV_HEADS * ATT_HEAD_DIM
ROPE_THETA = 10000.0

MEM_HEADS = 4
MEM_HEAD_DIM = 128
MEM_WIDTH = MEM_HEADS * MEM_HEAD_DIM

N_BRANCH = 4

OFF_UA = 0
OFF_ZA = 768
OFF_DQ = 1536
OFF_DK = 2304
OFF_DV = 3072
OFF_ZB = 3840
OFF_AQ = 4608
OFF_AK = 5632
OFF_AV = 5888
OFF_ZC = 6144
OFF_MQ = 7168
OFF_ZM = 7680
OFF_GATE = 8192
IN_AB = 3840
IN_AB_END = IN_AB + 4 * DN_HEADS
H_WIDTH = OFF_GATE + N_BRANCH * D_MODEL
AB_PAD = 128

VMEM_LIMIT = 56 * 1024 * 1024


def _cparams(sem):
    return pltpu.CompilerParams(dimension_semantics=sem, vmem_limit_bytes=VMEM_LIMIT)


def _silu(x):
    return x * jax.nn.sigmoid(x)


def _dot(a, b):
    return jnp.dot(a, b, preferred_element_type=F32)


def _dot_nt(a, b):
    return lax.dot_general(a, b, (((1,), (1,)), ((), ())), preferred_element_type=F32)


def _dot_tn(a, b):
    return lax.dot_general(a, b, (((0,), (0,)), ((), ())), preferred_element_type=F32)


def _split3(x):
    x1 = x.astype(BF16)
    r1 = x - x1.astype(F32)
    x2 = r1.astype(BF16)
    x3 = (r1 - x2.astype(F32)).astype(BF16)
    return x1, x2, x3


def _dot_exact_lhs(a_bf16, x):
    x1, x2, x3 = _split3(x)
    return _dot(a_bf16, x1) + _dot(a_bf16, x2) + _dot(a_bf16, x3)


def _dot_exact_rhs(x, b_bf16):
    x1, x2, x3 = _split3(x)
    return _dot(x1, b_bf16) + _dot(x2, b_bf16) + _dot(x3, b_bf16)


def _dot_f32(a, b):
    a1, a2, a3 = _split3(a)
    b1, b2, b3 = _split3(b)
    return (_dot(a1, b1) + (_dot(a1, b2) + _dot(a2, b1))
            + (_dot(a1, b3) + _dot(a2, b2) + _dot(a3, b1)))


def _inproj_kernel(x_ref, g_ref, w_ref, wab_ref, h_ref, ab_ref, xn_ref):
    @pl.when(pl.program_id(1) == 0)
    def _():
        x = x_ref[...]
        ms = jnp.mean(x * x, axis=-1, keepdims=True)
        xn_ref[...] = (x * lax.rsqrt(ms + EPS) * g_ref[...]).astype(BF16)
        ab_ref[...] = _dot(xn_ref[...], wab_ref[...])

    h_ref[...] = _dot(xn_ref[...], w_ref[...]).astype(BF16)


def _inproj(x, g, w_main, w_ab, layer):
    seq = x.shape[0]
    tm = min(1024, seq)
    tn = 2048
    return pl.pallas_call(
        _inproj_kernel,
        grid=(seq // tm, H_WIDTH // tn),
        in_specs=[
            pl.BlockSpec((tm, D_MODEL), lambda i, j: (i, 0)),
            pl.BlockSpec((1, D_MODEL), lambda i, j: (0, 0)),
            pl.BlockSpec((None, D_MODEL, tn), lambda i, j: (layer, 0, j)),
            pl.BlockSpec((None, D_MODEL, AB_PAD), lambda i, j: (layer, 0, 0)),
        ],
        out_specs=[
            pl.BlockSpec((tm, tn), lambda i, j: (i, j)),
            pl.BlockSpec((tm, AB_PAD), lambda i, j: (i, 0)),
        ],
        out_shape=[
            jax.ShapeDtypeStruct((seq, H_WIDTH), BF16),
            jax.ShapeDtypeStruct((seq, AB_PAD), F32),
        ],
        scratch_shapes=[pltpu.VMEM((tm, D_MODEL), BF16)],
        compiler_params=_cparams(("parallel", "arbitrary")),
        name="inproj",
    )(x, g, w_main, w_ab)


def _cpow_table(base_re, base_im, n):
    re = [jnp.ones_like(base_re)]
    im = [jnp.zeros_like(base_im)]
    for _ in range(n):
        re_n = re[-1] * base_re - im[-1] * base_im
        im_n = re[-1] * base_im + im[-1] * base_re
        re.append(re_n)
        im.append(im_n)
    return re, im


def _ssm_prep_kernel(are_r, aim_r, ls_r, bre_ref, bim_ref, cre_ref, cim_ref,
                     win_ref, wout_ref, m_ref):
    t = SSM_T
    cw = SSM_CW
    lane128 = lax.broadcasted_iota(jnp.int32, (1, 128), 1)
    col_t = lax.broadcasted_iota(jnp.int32, (1, cw), 1) // SSM_GROUP
    sub128 = lax.broadcasted_iota(jnp.int32, (128, 1), 0)
    lane_cw = lax.broadcasted_iota(jnp.int32, (SSM_GROUP, cw), 1)
    tile_p = jnp.where(lane_cw % SSM_GROUP == lax.broadcasted_iota(jnp.int32, (SSM_GROUP, cw), 0),
                       1.0, 0.0).astype(BF16)

    m_ref[...] = jnp.zeros(m_ref.shape, m_ref.dtype)

    krows = [[None, None], [None, None]]
    for d in range(2):
        step = jnp.exp(ls_r[d])
        a_re = are_r[d]
        a_im = aim_r[d]
        mag = jnp.exp(a_re * step)
        lam_re = mag * jnp.cos(a_im * step)
        lam_im = mag * jnp.sin(a_im * step)
        den = a_re * a_re + a_im * a_im
        nr = lam_re - 1.0
        ni = lam_im
        coef_re = (nr * a_re + ni * a_im) / den
        coef_im = (ni * a_re - nr * a_im) / den
        b_re = bre_ref[d]
        b_im = bim_ref[d]
        bbt_re = coef_re * b_re - coef_im * b_im
        bbt_im = coef_re * b_im + coef_im * b_re
        bb_re = jnp.concatenate([bbt_re] * t, axis=0)
        bb_im = jnp.concatenate([bbt_im] * t, axis=0)
        pr_re, pr_im = _cpow_table(lam_re, lam_im, t)
        e_in = [(t - 1 - s) if d == 0 else s for s in range(t)]
        p_re = jnp.concatenate([jnp.broadcast_to(pr_re[e], (SSM_GROUP, 128)) for e in e_in], axis=0)
        p_im = jnp.concatenate([jnp.broadcast_to(pr_im[e], (SSM_GROUP, 128)) for e in e_in], axis=0)
        w_re = p_re * bb_re - p_im * bb_im
        w_im = p_re * bb_im + p_im * bb_re
        for par in range(2):
            keep = (lane128 < 64) if par == 0 else (lane128 >= 64)
            win_ref[par * cw:(par + 1) * cw, (2 * d) * 128:(2 * d + 1) * 128] = (
                jnp.where(keep, w_re, 0.0).astype(BF16))
            win_ref[par * cw:(par + 1) * cw, (2 * d + 1) * 128:(2 * d + 2) * 128] = (
                jnp.where(keep, w_im, 0.0).astype(BF16))

        tab_re = jnp.zeros((128, 128), F32)
        tab_im = jnp.zeros((128, 128), F32)
        for e in range(t + 1):
            tab_re = jnp.where(sub128 == e, pr_re[e], tab_re)
            tab_im = jnp.where(sub128 == e, pr_im[e], tab_im)
        tab_re = tab_re.T
        tab_im = tab_im.T
        c_re = _dot_exact_rhs(cre_ref[d], tile_p)
        c_im = _dot_exact_rhs(cim_ref[d], tile_p)

        def c_lam(expo_row):
            sel = jnp.where(sub128 == expo_row, 1.0, 0.0).astype(BF16)
            q_re = _dot_exact_rhs(tab_re, sel)
            q_im = _dot_exact_rhs(tab_im, sel)
            return c_re * q_re - c_im * q_im, -(c_re * q_im + c_im * q_re)

        o_re, o_im = c_lam((col_t + 1) if d == 0 else (t - col_t))
        for par in range(2):
            keep = (sub128 < 64) if par == 0 else (sub128 >= 64)
            wout_ref[(2 * d) * 128:(2 * d + 1) * 128, par * cw:(par + 1) * cw] = (
                jnp.where(keep, o_re, 0.0).astype(BF16))
            wout_ref[(2 * d + 1) * 128:(2 * d + 2) * 128, par * cw:(par + 1) * cw] = (
                jnp.where(keep, o_im, 0.0).astype(BF16))

        r_re, r_im = c_lam(col_t if d == 0 else (t - 1 - col_t))
        for par in range(2):
            keep = (lane128 < 64) if par == 0 else (lane128 >= 64)
            krows[d][par] = (_dot_f32(jnp.where(keep, bbt_re, 0.0), r_re)
                             + _dot_f32(jnp.where(keep, bbt_im, 0.0), r_im))

    for par in range(2):
        kf = krows[0][par]
        kb = krows[1][par]
        for s in range(t):
            sh_f = SSM_GROUP * s
            blk = jnp.where(lane_cw >= sh_f, pltpu.roll(kf, sh_f, axis=1) if sh_f else kf, 0.0)
            sh_b = SSM_GROUP * (t - 1 - s)
            rolled_b = pltpu.roll(kb, cw - sh_b, axis=1) if sh_b else kb
            blk = blk + jnp.where(lane_cw < cw - sh_b, rolled_b, 0.0)
            m_ref[par * cw + s * SSM_GROUP:par * cw + (s + 1) * SSM_GROUP,
                  par * cw:(par + 1) * cw] = blk.astype(BF16)


def _ssm_prep(rows, b_t, c_t):
    pw = 2 * SSM_CW
    row_spec = pl.BlockSpec((None, 2, 1, 128), lambda g: (g, 0, 0, 0))
    b_spec = pl.BlockSpec((None, 2, SSM_GROUP, 128), lambda g: (g, 0, 0, 0))
    c_spec = pl.BlockSpec((None, 2, 128, SSM_GROUP), lambda g: (g, 0, 0, 0))
    w_spec = pl.BlockSpec((None, pw, pw), lambda g: (g, 0, 0))
    return pl.pallas_call(
        _ssm_prep_kernel,
        grid=(SSM_PAIRS,),
        in_specs=[row_spec] * 3 + [b_spec] * 2 + [c_spec] * 2,
        out_specs=[w_spec] * 3,
        out_shape=[jax.ShapeDtypeStruct((SSM_PAIRS, pw, pw), BF16)] * 3,
        compiler_params=_cparams(("parallel",)),
        name="ssm_prep",
    )(*rows, *b_t, *c_t)


SSM_BG = 8
SSM_BP = SSM_BG // 2
SSM_BW = SSM_BG * SSM_CW


def _ssm_place(t0):
    r = lax.broadcasted_iota(jnp.int32, (256, SSM_BW), 0)
    c = lax.broadcasted_iota(jnp.int32, (256, SSM_BW), 1)
    j = r % 128
    target = (j // SSM_GROUP) * SSM_CW + (t0 + r // 128) * SSM_GROUP + j % SSM_GROUP
    return jnp.where(c == target, 1.0, 0.0).astype(BF16)


def _ssm_in_kernel(u_ref, win_ref, u2_ref, h_ref, uf_ref):
    nchunk = u2_ref.shape[0]
    uf_ref[...] = u_ref[...].astype(F32)
    acc = jnp.zeros((nchunk, SSM_BW), F32)
    for t0 in range(0, SSM_T, 2):
        lhs = jnp.concatenate([uf_ref[pl.ds(t0, nchunk, stride=SSM_T), :],
                               uf_ref[pl.ds(t0 + 1, nchunk, stride=SSM_T), :]], axis=1)
        acc = acc + _dot(lhs.astype(BF16), _ssm_place(t0))
    u2_ref[...] = acc.astype(BF16)
    pw = 2 * SSM_CW
    for j in range(SSM_BP):
        h = _dot(u2_ref[:, j * pw:(j + 1) * pw], win_ref[j])
        for k in range(4):
            h_ref[k, :, j * 128:(j + 1) * 128] = h[:, k * 128:(k + 1) * 128]


def _ssm_in(h, win):
    seq = h.shape[0]
    nchunk = seq // SSM_T
    pw = 2 * SSM_CW
    nblk = SSM_GROUPS // SSM_BG
    return pl.pallas_call(
        _ssm_in_kernel,
        grid=(nblk,),
        in_specs=[
            pl.BlockSpec((seq, 128), lambda b: (0, OFF_UA // 128 + b)),
            pl.BlockSpec((SSM_BP, pw, pw), lambda b: (b, 0, 0)),
        ],
        out_specs=[
            pl.BlockSpec((nchunk, SSM_BW), lambda b: (0, b)),
            pl.BlockSpec((4, nchunk, SSM_BP * 128), lambda b: (0, 0, b)),
        ],
        out_shape=[
            jax.ShapeDtypeStruct((nchunk, SSM_GROUPS * SSM_CW), BF16),
            jax.ShapeDtypeStruct((4, nchunk, SSM_PAIRS * 128), F32),
        ],
        scratch_shapes=[pltpu.VMEM((seq, 128), F32)],
        compiler_params=_cparams(("parallel",)),
        name="ssm_in",
    )(h, win)


def _ssm_scan_kernel(are_ref, aim_ref, ls_ref, h_ref, p_ref, *, nchunk):
    width = h_ref.shape[2]

    def lam_pow_t(d):
        step = jnp.exp(ls_ref[d]) * float(SSM_T)
        mag = jnp.exp(are_ref[d] * step)
        ang = aim_ref[d] * step
        return mag * jnp.cos(ang), mag * jnp.sin(ang)

    lfr, lfi = lam_pow_t(0)
    lbr, lbi = lam_pow_t(1)

    def body(c, carry):
        fr, fi, br, bi = carry
        cb = nchunk - 1 - c
        p_ref[0, pl.ds(c, 1), :] = fr
        p_ref[1, pl.ds(c, 1), :] = fi
        p_ref[2, pl.ds(cb, 1), :] = br
        p_ref[3, pl.ds(cb, 1), :] = bi
        hfr = h_ref[0, pl.ds(c, 1), :]
        hfi = h_ref[1, pl.ds(c, 1), :]
        hbr = h_ref[2, pl.ds(cb, 1), :]
        hbi = h_ref[3, pl.ds(cb, 1), :]
        nfr = lfr * fr - lfi * fi + hfr
        nfi = lfr * fi + lfi * fr + hfi
        nbr = lbr * br - lbi * bi + hbr
        nbi = lbr * bi + lbi * br + hbi
        return nfr, nfi, nbr, nbi

    z = jnp.zeros((1, width), F32)
    lax.fori_loop(0, nchunk, body, (z, z, z, z))


def _ssm_scan(flat_params, h):
    nchunk = h.shape[1]
    width = h.shape[2]
    wt = 768
    return pl.pallas_call(
        functools.partial(_ssm_scan_kernel, nchunk=nchunk),
        grid=(width // wt,),
        in_specs=[pl.BlockSpec((2, 1, wt), lambda j: (0, 0, j))] * 3 + [
            pl.BlockSpec((4, nchunk, wt), lambda j: (0, 0, j)),
        ],
        out_specs=pl.BlockSpec((4, nchunk, wt), lambda j: (0, 0, j)),
        out_shape=jax.ShapeDtypeStruct(h.shape, F32),
        compiler_params=_cparams(("parallel",)),
        name="ssm_scan",
    )(*flat_params, h)


def _ssm_out_kernel(u2_ref, p_ref, m_ref, wout_ref, y_ref):
    nchunk = u2_ref.shape[0]
    pw = 2 * SSM_CW
    parts = []
    for j in range(SSM_BP):
        acc = _dot(u2_ref[:, j * pw:(j + 1) * pw], m_ref[j])
        for k in range(4):
            acc = acc + _dot(p_ref[k, :, j * 128:(j + 1) * 128].astype(BF16),
                             wout_ref[j, k * 128:(k + 1) * 128, :])
        parts.append(acc.astype(BF16))
    y16 = jnp.concatenate(parts, axis=1)
    for t0 in range(0, SSM_T, 2):
        yt = _dot_nt(y16, _ssm_place(t0))
        y_ref[pl.ds(t0, nchunk, stride=SSM_T), :] = yt[:, :128]
        y_ref[pl.ds(t0 + 1, nchunk, stride=SSM_T), :] = yt[:, 128:]


def _ssm_out(u2, p, m, wout):
    nchunk = u2.shape[0]
    seq = nchunk * SSM_T
    pw = 2 * SSM_CW
    nblk = SSM_GROUPS // SSM_BG
    return pl.pallas_call(
        _ssm_out_kernel,
        grid=(nblk,),
        in_specs=[
            pl.BlockSpec((nchunk, SSM_BW), lambda b: (0, b)),
            pl.BlockSpec((4, nchunk, SSM_BP * 128), lambda b: (0, 0, b)),
            pl.BlockSpec((SSM_BP, pw, pw), lambda b: (b, 0, 0)),
            pl.BlockSpec((SSM_BP, pw, pw), lambda b: (b, 0, 0)),
        ],
        out_specs=pl.BlockSpec((seq, 128), lambda b: (0, b)),
        out_shape=jax.ShapeDtypeStruct((seq, SSM_WIDTH), F32),
        compiler_params=_cparams(("parallel",)),
        name="ssm_out",
    )(u2, p, m, wout)


def _ssm_epi_kernel(y_ref, u_ref, z_ref, d_ref, wg_ref, bg_ref, o_ref):
    y = y_ref[...] + d_ref[...] * u_ref[...].astype(F32)
    y = jax.nn.gelu(y)
    glu = _dot(y.astype(BF16), wg_ref[...].astype(BF16)) + bg_ref[...]
    y = y * jax.nn.sigmoid(glu)
    o_ref[...] = (y * _silu(z_ref[...].astype(F32))).astype(BF16)


def _ssm_epi(y, h, d, w_glu, b_glu):
    seq = y.shape[0]
    tm = min(512, seq)
    w = SSM_WIDTH
    return pl.pallas_call(
        _ssm_epi_kernel,
        grid=(seq // tm,),
        in_specs=[
            pl.BlockSpec((tm, w), lambda i: (i, 0)),
            pl.BlockSpec((tm, w), lambda i: (i, OFF_UA // w)),
            pl.BlockSpec((tm, w), lambda i: (i, OFF_ZA // w)),
            pl.BlockSpec((1, w), lambda i: (0, 0)),
            pl.BlockSpec((w, w), lambda i: (0, 0)),
            pl.BlockSpec((1, w), lambda i: (0, 0)),
        ],
        out_specs=pl.BlockSpec((tm, w), lambda i: (i, 0)),
        out_shape=jax.ShapeDtypeStruct((seq, w), BF16),
        compiler_params=_cparams(("parallel",)),
        name="ssm_epi",
    )(y, h, h, d, w_glu, b_glu)


def _ssm_branch(h, lp):
    win, wout, m = _ssm_prep(lp["ssm_rows"], lp["ssm_b"], lp["ssm_c"])
    u2, hs = _ssm_in(h, win)
    p = _ssm_scan(lp["ssm_flat"], hs)
    y = _ssm_out(u2, p, m, wout)
    return _ssm_epi(y, h, lp["ssm_d"], lp["ssm_w_glu"], lp["ssm_b_glu"])


def _dn_prep_kernel(qc, qp, qn, kc, kp, kn, vc, vp, vn, ab_ref, cw_ref, alog_ref, dtb_ref,
                    qo_ref, ko_ref, vo_ref, sc_ref, sct_ref, ext_ref, *, tm, nblk):
    i = pl.program_id(0)
    halo = 16
    pad = DN_CONV // 2

    def conv_silu(cur, prev, nxt, part):
        ext_ref[0:halo, :] = jnp.where(i > 0, prev[...].astype(F32), 0.0)
        ext_ref[halo:halo + tm, :] = cur[...].astype(F32)
        ext_ref[halo + tm:halo + tm + halo, :] = jnp.where(i < nblk - 1, nxt[...].astype(F32), 0.0)
        acc = jnp.zeros((tm, DN_WIDTH), F32)
        for j in range(DN_CONV):
            acc = acc + cw_ref[part, j:j + 1, :] * ext_ref[pl.ds(halo - pad + j, tm), :]
        return _silu(acc)

    def l2n(x):
        outs = []
        for hd in range(DN_HEADS):
            xh = x[:, hd * DN_HEAD_DIM:(hd + 1) * DN_HEAD_DIM]
            outs.append(xh * lax.rsqrt(jnp.sum(xh * xh, axis=-1, keepdims=True) + EPS))
        return jnp.concatenate(outs, axis=1)

    qo_ref[...] = (l2n(conv_silu(qc, qp, qn, 0)) * (DN_HEAD_DIM ** -0.5)).astype(BF16)
    ko_ref[...] = l2n(conv_silu(kc, kp, kn, 1)).astype(BF16)
    vo_ref[...] = conv_silu(vc, vp, vn, 2).astype(BF16)

    ab = ab_ref[...]
    g_all = -jnp.exp(alog_ref[...]) * jax.nn.softplus(ab + dtb_ref[...])
    beta_all = jax.nn.sigmoid(ab)
    r = lax.broadcasted_iota(jnp.int32, (tm, tm), 0)
    c = lax.broadcasted_iota(jnp.int32, (tm, tm), 1)
    same = (r // DN_CHUNK) == (c // DN_CHUNK)
    tri_f = jnp.where(same & (c <= r), 1.0, 0.0).astype(BF16)
    tri_b = jnp.where(same & (c >= r), 1.0, 0.0).astype(BF16)
    blk = jnp.where(same, 1.0, 0.0).astype(BF16)
    gcf = _dot_exact_lhs(tri_f, g_all)
    gcb = _dot_exact_lhs(tri_b, g_all)
    gtot = _dot_exact_lhs(blk, g_all)
    lane = lax.broadcasted_iota(jnp.int32, (tm, AB_PAD), 1)
    sc = jnp.where(lane < 6, gcf,
                   jnp.where(lane < 12, gcb,
                             jnp.where(lane < 24, beta_all,
                                       jnp.where(lane < 36, pltpu.roll(gtot, 24, axis=1), 0.0))))
    sc_ref[...] = sc
    sct_ref[...] = sc.T


def _dn_prep(h, ab, conv_w, alog, dtb):
    seq = h.shape[0]
    tm = min(256, seq)
    nblk = seq // tm
    w = DN_WIDTH
    hb = tm // 16
    nh = seq // 16

    def cur(ci):
        return pl.BlockSpec((tm, w), lambda i: (i, ci))

    def prev(ci):
        return pl.BlockSpec((16, w), lambda i: (jnp.maximum(i * hb - 1, 0), ci))

    def nxt(ci):
        return pl.BlockSpec((16, w), lambda i: (jnp.minimum((i + 1) * hb, nh - 1), ci))

    in_specs = []
    for off in (OFF_DQ, OFF_DK, OFF_DV):
        ci = off // w
        in_specs += [cur(ci), prev(ci), nxt(ci)]
    in_specs += [
        pl.BlockSpec((tm, AB_PAD), lambda i: (i, 0)),
        pl.BlockSpec((3, 8, w), lambda i: (0, 0, 0)),
        pl.BlockSpec((1, AB_PAD), lambda i: (0, 0)),
        pl.BlockSpec((1, AB_PAD), lambda i: (0, 0)),
    ]
    return pl.pallas_call(
        functools.partial(_dn_prep_kernel, tm=tm, nblk=nblk),
        grid=(nblk,),
        in_specs=in_specs,
        out_specs=[pl.BlockSpec((tm, w), lambda i: (i, 0))] * 3
        + [pl.BlockSpec((tm, AB_PAD), lambda i: (i, 0)), pl.BlockSpec((AB_PAD, tm), lambda i: (0, i))],
        out_shape=[jax.ShapeDtypeStruct((seq, w), BF16)] * 3
        + [jax.ShapeDtypeStruct((seq, AB_PAD), F32), jax.ShapeDtypeStruct((AB_PAD, seq), F32)],
        scratch_shapes=[pltpu.VMEM((tm + 32, w), F32)],
        compiler_params=_cparams(("parallel",)),
        name="dn_prep",
    )(h, h, h, h, h, h, h, h, h, ab, conv_w, alog, dtb)


def _dn_main_kernel(qf_ref, kf_ref, vf_ref, scf_ref, sctf_ref,
                    qb_ref, kb_ref, vb_ref, scb_ref, sctb_ref, of_ref, ob_ref, s_ref):
    gsz = DN_GROUP
    nck = gsz // DN_CHUNK

    @pl.when(pl.program_id(1) == 0)
    def _():
        s_ref[...] = jnp.zeros(s_ref.shape, F32)

    refs = ((qf_ref, kf_ref, vf_ref, scf_ref, sctf_ref, of_ref),
            (qb_ref, kb_ref, vb_ref, scb_ref, sctb_ref, ob_ref))
    r = lax.broadcasted_iota(jnp.int32, (gsz, gsz), 0)
    c = lax.broadcasted_iota(jnp.int32, (gsz, gsz), 1)
    same = (r // DN_CHUNK) == (c // DN_CHUNK)
    incl = (same & (r >= c), same & (r <= c))
    strict = (same & (r > c), same & (r < c))

    chains = []
    for d in range(2):
        q_ref, k_ref, v_ref, sc_ref, sct_ref, o_ref = refs[d]
        sc = sc_ref[...]
        sct = sct_ref[...]
        lane = lax.broadcasted_iota(jnp.int32, sc.shape, 1)
        sub = lax.broadcasted_iota(jnp.int32, sct.shape, 0)
        for hl in range(DN_HB):
            cidx = d * DN_HEADS + pl.program_id(0) * DN_HB + hl
            hs = slice(hl * DN_HEAD_DIM, (hl + 1) * DN_HEAD_DIM)

            def col(ci, sc=sc, lane=lane):
                return jnp.sum(jnp.where(lane == ci, sc, 0.0), axis=1, keepdims=True)

            ch = {"d": d, "hl": hl, "hs": hs, "o_ref": o_ref}
            ch["q"] = q_ref[:, hs]
            ch["k"] = k_ref[:, hs]
            ch["v"] = v_ref[:, hs]
            ch["gc_col"] = col(cidx)
            ch["gc_row"] = jnp.sum(jnp.where(sub == cidx, sct, 0.0), axis=0, keepdims=True)
            ch["beta_col"] = col(12 + cidx)
            ch["gtot_col"] = col(24 + cidx)
            chains.append(ch)

    for ch in chains:
        ch["kf"] = ch["k"].astype(F32)
        ch["kb"] = ch["kf"] * ch["beta_col"]
        ch["gram"] = _dot_nt(ch["kb"].astype(BF16), ch["k"])
    for ch in chains:
        ch["qk"] = _dot_nt(ch["q"], ch["k"])
    for ch in chains:
        d = ch["d"]
        ch["decay"] = jnp.where(incl[d], jnp.exp(jnp.where(incl[d], ch["gc_col"] - ch["gc_row"], 0.0)), 0.0)
        ch["x"] = jnp.concatenate([ch["v"].astype(F32) * ch["beta_col"],
                                   ch["kb"] * jnp.exp(ch["gc_col"])], axis=1)
        ch["qd16"] = (ch["q"].astype(F32) * jnp.exp(ch["gc_col"])).astype(BF16)
        ch["kd16"] = (ch["kf"] * jnp.exp(ch["gtot_col"] - ch["gc_col"])).astype(BF16)
    for ch in chains:
        ch["p"] = jnp.where(strict[ch["d"]], ch["gram"] * ch["decay"], 0.0).astype(BF16)
        ch["intra"] = (ch["qk"] * ch["decay"]).astype(BF16)
    for ch in chains:
        ch["x"] = ch["x"] - _dot(ch["p"], ch["x"].astype(BF16))
    npow = 2
    while npow < DN_CHUNK:
        for ch in chains:
            ch["p"] = _dot(ch["p"], ch["p"]).astype(BF16)
        for ch in chains:
            ch["x"] = ch["x"] + _dot(ch["p"], ch["x"].astype(BF16))
        npow *= 2
    for ch in chains:
        ch["u"] = ch["x"][:, :DN_HEAD_DIM]
        ch["w16"] = ch["x"][:, DN_HEAD_DIM:].astype(BF16)
        ch["s"] = s_ref[ch["d"], ch["hl"]]
        ch["vnew"] = [None] * nck
        ch["oq"] = [None] * nck

    for step in range(nck):
        for ch in chains:
            j = step if ch["d"] == 0 else nck - 1 - step
            lo, hi = j * DN_CHUNK, (j + 1) * DN_CHUNK
            s16 = ch["s"].astype(BF16)
            ws = _dot(jnp.concatenate([ch["w16"][lo:hi], ch["qd16"][lo:hi]], axis=0), s16)
            vn = ch["u"][lo:hi] - ws[:DN_CHUNK]
            ch["oq"][j] = ws[DN_CHUNK:]
            ch["vnew"][j] = vn
            ch["s"] = (ch["s"] * jnp.exp(ch["gtot_col"][lo:lo + 1, :])
                       + _dot_tn(ch["kd16"][lo:hi], vn.astype(BF16)))
    for ch in chains:
        s_ref[ch["d"], ch["hl"]] = ch["s"]
        vn_all = jnp.concatenate(ch["vnew"], axis=0).astype(BF16)
        ch["o_ref"][:, ch["hs"]] = jnp.concatenate(ch["oq"], axis=0) + _dot(ch["intra"], vn_all)


def _dn_main(qn, kn, vc, sc, sct):
    seq = qn.shape[0]
    gsz = DN_GROUP
    ng = seq // gsz
    hw = DN_HB * DN_HEAD_DIM
    sct_rows = sct.shape[0]
    fwd = lambda h, i: (i, h)
    bwd = lambda h, i: (ng - 1 - i, h)
    in_specs = [
        pl.BlockSpec((gsz, hw), fwd), pl.BlockSpec((gsz, hw), fwd), pl.BlockSpec((gsz, hw), fwd),
        pl.BlockSpec((gsz, AB_PAD), lambda h, i: (i, 0)),
        pl.BlockSpec((sct_rows, gsz), lambda h, i: (0, i)),
        pl.BlockSpec((gsz, hw), bwd), pl.BlockSpec((gsz, hw), bwd), pl.BlockSpec((gsz, hw), bwd),
        pl.BlockSpec((gsz, AB_PAD), lambda h, i: (ng - 1 - i, 0)),
        pl.BlockSpec((sct_rows, gsz), lambda h, i: (0, ng - 1 - i)),
    ]
    return pl.pallas_call(
        _dn_main_kernel,
        grid=(DN_HEADS // DN_HB, ng),
        in_specs=in_specs,
        out_specs=[pl.BlockSpec((gsz, hw), fwd), pl.BlockSpec((gsz, hw), bwd)],
        out_shape=[jax.ShapeDtypeStruct((seq, DN_WIDTH), F32)] * 2,
        scratch_shapes=[pltpu.VMEM((2, DN_HB, DN_HEAD_DIM, DN_HEAD_DIM), F32)],
        compiler_params=_cparams(("parallel", "arbitrary")),
        name="dn_main",
    )(qn, kn, vc, sc, sct, qn, kn, vc, sc, sct)


def _dn_epi_kernel(of_ref, ob_ref, z_ref, g_ref, o_ref):
    o = of_ref[...] + ob_ref[...]
    g = g_ref[...]
    outs = []
    for hd in range(DN_HEADS):
        oh = o[:, hd * DN_HEAD_DIM:(hd + 1) * DN_HEAD_DIM]
        ms = jnp.mean(oh * oh, axis=-1, keepdims=True)
        outs.append(oh * lax.rsqrt(ms + EPS) * g)
    y = jnp.concatenate(outs, axis=1)
    o_ref[...] = (y * _silu(z_ref[...].astype(F32))).astype(BF16)


def _dn_epi(o_f, o_b, h, norm_g):
    seq = o_f.shape[0]
    tm = min(512, seq)
    w = DN_WIDTH
    return pl.pallas_call(
        _dn_epi_kernel,
        grid=(seq // tm,),
        in_specs=[
            pl.BlockSpec((tm, w), lambda i: (i, 0)),
            pl.BlockSpec((tm, w), lambda i: (i, 0)),
            pl.BlockSpec((tm, w), lambda i: (i, OFF_ZB // w)),
            pl.BlockSpec((1, DN_HEAD_DIM), lambda i: (0, 0)),
        ],
        out_specs=pl.BlockSpec((tm, w), lambda i: (i, 0)),
        out_shape=jax.ShapeDtypeStruct((seq, w), BF16),
        compiler_params=_cparams(("parallel",)),
        name="dn_epi",
    )(o_f, o_b, h, norm_g)


def _dn_branch(h, ab, lp):
    qn, kn, vc, sc, sct = _dn_prep(h, ab, lp["dn_conv"], lp["dn_alog"], lp["dn_dtb"])
    o_f, o_b = _dn_main(qn, kn, vc, sc, sct)
    return _dn_epi(o_f, o_b, h, lp["dn_norm_g"])


def _att_prep_kernel(qlo_ref, qhi_ref, k_ref, v_ref, cos_ref, sin_ref, gq_ref, gk_ref,
                     qt_ref, ko_ref, vt_ref):
    cos2 = cos_ref[...]
    sin2 = sin_ref[...]
    even = lax.broadcasted_iota(jnp.int32, cos2.shape, 1) % 2 == 0

    def norm_rope(x, g):
        ms = jnp.mean(x * x, axis=-1, keepdims=True)
        xn = x * lax.rsqrt(ms + EPS) * g
        partner = jnp.where(even, pltpu.roll(xn, ATT_HEAD_DIM - 1, axis=1), pltpu.roll(xn, 1, axis=1))
        return xn * cos2 + partner * sin2

    gq = gq_ref[...]
    gk = gk_ref[...]
    scale = ATT_HEAD_DIM ** -0.5 * math.log2(math.e)
    half = ATT_HEADS // 2
    for hd in range(ATT_HEADS):
        src = qlo_ref if hd < half else qhi_ref
        ssl = slice((hd % half) * ATT_HEAD_DIM, (hd % half + 1) * ATT_HEAD_DIM)
        sl = slice(hd * ATT_HEAD_DIM, (hd + 1) * ATT_HEAD_DIM)
        qt_ref[sl, :] = (norm_rope(src[:, ssl].astype(F32), gq) * scale).T.astype(BF16)
    for hd in range(ATT_KV_HEADS):
        sl = slice(hd * ATT_HEAD_DIM, (hd + 1) * ATT_HEAD_DIM)
        ko_ref[:, sl] = norm_rope(k_ref[:, sl].astype(F32), gk).astype(BF16)
        vt_ref[sl, :] = v_ref[:, sl].astype(F32).T.astype(BF16)


def _att_prep(h, cos2, sin2, gq, gk):
    seq = h.shape[0]
    tm = min(512, seq)
    return pl.pallas_call(
        _att_prep_kernel,
        grid=(seq // tm,),
        in_specs=[
            pl.BlockSpec((tm, ATT_WIDTH // 2), lambda i: (i, OFF_AQ // (ATT_WIDTH // 2))),
            pl.BlockSpec((tm, ATT_WIDTH // 2), lambda i: (i, OFF_AQ // (ATT_WIDTH // 2) + 1)),
            pl.BlockSpec((tm, ATT_KV_WIDTH), lambda i: (i, OFF_AK // ATT_KV_WIDTH)),
            pl.BlockSpec((tm, ATT_KV_WIDTH), lambda i: (i, OFF_AV // ATT_KV_WIDTH)),
            pl.BlockSpec((tm, ATT_HEAD_DIM), lambda i: (i, 0)),
            pl.BlockSpec((tm, ATT_HEAD_DIM), lambda i: (i, 0)),
            pl.BlockSpec((1, ATT_HEAD_DIM), lambda i: (0, 0)),
            pl.BlockSpec((1, ATT_HEAD_DIM), lambda i: (0, 0)),
        ],
        out_specs=[
            pl.BlockSpec((ATT_WIDTH, tm), lambda i: (0, i)),
            pl.BlockSpec((tm, ATT_KV_WIDTH), lambda i: (i, 0)),
            pl.BlockSpec((ATT_KV_WIDTH, tm), lambda i: (0, i)),
        ],
        out_shape=[
            jax.ShapeDtypeStruct((ATT_WIDTH, seq), BF16),
            jax.ShapeDtypeStruct((seq, ATT_KV_WIDTH), BF16),
            jax.ShapeDtypeStruct((ATT_KV_WIDTH, seq), BF16),
        ],
        compiler_params=_cparams(("parallel",)),
        name="att_prep",
    )(h, h, h, h, cos2, sin2, gq, gk)


def _att_kernel(qt_ref, k_ref, vt_ref, z_ref, o_ref, m_ref, l_ref, acc_ref, *, kc, nkc):
    grp = qt_ref.shape[0] // ATT_HEAD_DIM
    m_ref[...] = jnp.full(m_ref.shape, -1e30, F32)
    l_ref[...] = jnp.zeros(l_ref.shape, F32)
    acc_ref[...] = jnp.zeros(acc_ref.shape, F32)

    def chunk(c, carry):
        k0 = pl.multiple_of(c * kc, kc)
        k_c = k_ref[pl.ds(k0, kc), :]
        vt_c = vt_ref[:, pl.ds(k0, kc)]
        s = [_dot(k_c, qt_ref[hh * ATT_HEAD_DIM:(hh + 1) * ATT_HEAD_DIM, :])
             for hh in range(grp)]
        p = [None] * grp
        alpha = [None] * grp
        for hh in range(grp):
            m_old = m_ref[hh]
            m_new = jnp.maximum(m_old, jnp.max(s[hh], axis=0, keepdims=True))
            alpha[hh] = jnp.exp2(m_old - m_new)
            ph = jnp.exp2(s[hh] - m_new)
            l_ref[hh] = alpha[hh] * l_ref[hh] + jnp.sum(ph, axis=0, keepdims=True)
            m_ref[hh] = m_new
            p[hh] = ph.astype(BF16)
        for hh in range(grp):
            acc_ref[hh] = alpha[hh] * acc_ref[hh] + _dot(vt_c, p[hh])
        return carry

    lax.fori_loop(0, nkc, chunk, 0)
    for hh in range(grp):
        sl = slice(hh * ATT_HEAD_DIM, (hh + 1) * ATT_HEAD_DIM)
        o = (acc_ref[hh] / l_ref[hh]).T
        o_ref[:, sl] = (o * _silu(z_ref[:, sl].astype(F32))).astype(BF16)


ATT_HPS = 4


def _att(qt, kr, vt, h):
    seq = kr.shape[0]
    tq = min(256, seq)
    kc = min(4096, seq)
    hw = ATT_HPS * ATT_HEAD_DIM
    steps_per_kv = ATT_HEADS // ATT_KV_HEADS // ATT_HPS
    return pl.pallas_call(
        functools.partial(_att_kernel, kc=kc, nkc=seq // kc),
        grid=(ATT_HEADS // ATT_HPS, seq // tq),
        in_specs=[
            pl.BlockSpec((hw, tq), lambda a, i: (a, i)),
            pl.BlockSpec((seq, ATT_HEAD_DIM), lambda a, i: (0, a // steps_per_kv)),
            pl.BlockSpec((ATT_HEAD_DIM, seq), lambda a, i: (a // steps_per_kv, 0)),
            pl.BlockSpec((tq, hw), lambda a, i: (i, OFF_ZC // hw + a)),
        ],
        out_specs=pl.BlockSpec((tq, hw), lambda a, i: (i, a)),
        out_shape=jax.ShapeDtypeStruct((seq, ATT_WIDTH), BF16),
        scratch_shapes=[
            pltpu.VMEM((ATT_HPS, 1, tq), F32),
            pltpu.VMEM((ATT_HPS, 1, tq), F32),
            pltpu.VMEM((ATT_HPS, ATT_HEAD_DIM, tq), F32),
        ],
        compiler_params=_cparams(("parallel", "parallel")),
        name="grid_att",
    )(qt, kr, vt, h)


def _mem_kv_kernel(mem_ref, g_ref, w_ref, o_ref):
    x = mem_ref[...]
    ms = jnp.mean(x * x, axis=-1, keepdims=True)
    xn = (x * lax.rsqrt(ms + EPS) * g_ref[...]).astype(BF16)
    o_ref[...] = _dot(xn, w_ref[...].astype(BF16)).astype(BF16)


def _mem_kv(mem, g, w_kv):
    n_mem = mem.shape[0]
    tn = 512
    return pl.pallas_call(
        _mem_kv_kernel,
        grid=(2 * MEM_WIDTH // tn,),
        in_specs=[
            pl.BlockSpec((n_mem, D_MODEL), lambda j: (0, 0)),
            pl.BlockSpec((1, D_MODEL), lambda j: (0, 0)),
            pl.BlockSpec((D_MODEL, tn), lambda j: (0, j)),
        ],
        out_specs=pl.BlockSpec((n_mem, tn), lambda j: (0, j)),
        out_shape=jax.ShapeDtypeStruct((n_mem, 2 * MEM_WIDTH), BF16),
        compiler_params=_cparams(("parallel",)),
        name="mem_kv",
    )(mem, g, w_kv)


def _mem_att_kernel(q_ref, z_ref, kv_ref, o_ref):
    scale = MEM_HEAD_DIM ** -0.5
    for hd in range(MEM_HEADS):
        sl = slice(hd * MEM_HEAD_DIM, (hd + 1) * MEM_HEAD_DIM)
        k = kv_ref[:, sl]
        v = kv_ref[:, MEM_WIDTH + hd * MEM_HEAD_DIM:MEM_WIDTH + (hd + 1) * MEM_HEAD_DIM]
        s = _dot_nt(q_ref[:, sl], k) * scale
        m = jnp.max(s, axis=-1, keepdims=True)
        p = jnp.exp(s - m)
        l = jnp.sum(p, axis=-1, keepdims=True)
        o = _dot(p.astype(BF16), v) / l
        o_ref[:, sl] = (o * _silu(z_ref[:, sl].astype(F32))).astype(BF16)


def _mem_att(h, kv):
    seq = h.shape[0]
    tm = min(1024, seq)
    n_mem = kv.shape[0]
    return pl.pallas_call(
        _mem_att_kernel,
        grid=(seq // tm,),
        in_specs=[
            pl.BlockSpec((tm, MEM_WIDTH), lambda i: (i, OFF_MQ // MEM_WIDTH)),
            pl.BlockSpec((tm, MEM_WIDTH), lambda i: (i, OFF_ZM // MEM_WIDTH)),
            pl.BlockSpec((n_mem, 2 * MEM_WIDTH), lambda i: (0, 0)),
        ],
        out_specs=pl.BlockSpec((tm, MEM_WIDTH), lambda i: (i, 0)),
        out_shape=jax.ShapeDtypeStruct((seq, MEM_WIDTH), BF16),
        compiler_params=_cparams(("parallel",)),
        name="mem_att",
    )(h, h, kv)


def _merge_kernel(ya_ref, yb_ref, yc_ref, ym_ref, ga_ref, gb_ref, gc_ref, gm_ref,
                  wa_ref, wb_ref, wc0_ref, wc1_ref, wm_ref, o_ref):
    half = ATT_WIDTH // 2
    ya_w = _dot(ya_ref[...], wa_ref[...])
    yb_w = _dot(yb_ref[...], wb_ref[...])
    yc_w = _dot(yc_ref[:, :half], wc0_ref[...]) + _dot(yc_ref[:, half:], wc1_ref[...])
    ym_w = _dot(ym_ref[...], wm_ref[...])
    merged = jax.nn.sigmoid(ga_ref[...].astype(F32)) * ya_w
    merged = merged + jax.nn.sigmoid(gb_ref[...].astype(F32)) * yb_w
    merged = merged + jax.nn.sigmoid(gc_ref[...].astype(F32)) * yc_w
    merged = merged + jax.nn.sigmoid(gm_ref[...].astype(F32)) * ym_w
    o_ref[...] = merged.astype(BF16)


def _outproj_kernel(m_ref, wo_ref, x_ref, fg_ref, o_ref, *, final_norm):
    y = x_ref[...] + _dot(m_ref[...], wo_ref[...])
    if final_norm:
        ms = jnp.mean(y * y, axis=-1, keepdims=True)
        y = y * lax.rsqrt(ms + EPS) * fg_ref[...]
    o_ref[...] = y


def _outproj(merged, w_out, x, final_g, layer, final_norm):
    seq = x.shape[0]
    tm = min(512, seq)
    return pl.pallas_call(
        functools.partial(_outproj_kernel, final_norm=final_norm),
        grid=(seq // tm,),
        in_specs=[
            pl.BlockSpec((tm, D_MODEL), lambda i: (i, 0)),
            pl.BlockSpec((None, D_MODEL, D_MODEL), lambda i: (layer, 0, 0)),
            pl.BlockSpec((tm, D_MODEL), lambda i: (i, 0)),
            pl.BlockSpec((1, D_MODEL), lambda i: (0, 0)),
        ],
        out_specs=pl.BlockSpec((tm, D_MODEL), lambda i: (i, 0)),
        out_shape=jax.ShapeDtypeStruct((seq, D_MODEL), F32),
        compiler_params=_cparams(("parallel",)),
        name="out_proj",
    )(merged, w_out, x, final_g)


def _merge(ys, h, w_branch, layer):
    seq = h.shape[0]
    tm = min(1024, seq)
    tn = 512
    widths = (SSM_WIDTH, DN_WIDTH, ATT_WIDTH, MEM_WIDTH)
    half = ATT_WIDTH // 2
    assert SSM_WIDTH == DN_WIDTH and (SSM_WIDTH + DN_WIDTH) % half == 0 and MEM_WIDTH == half
    c_blk = (SSM_WIDTH + DN_WIDTH) // half
    in_specs = [pl.BlockSpec((tm, wd), lambda i, j: (i, 0)) for wd in widths]
    for b in range(N_BRANCH):
        base = (OFF_GATE + b * D_MODEL) // tn
        in_specs.append(pl.BlockSpec((tm, tn), lambda i, j, base=base: (i, base + j)))
    in_specs += [
        pl.BlockSpec((None, SSM_WIDTH, tn), lambda i, j: (layer, 0, j)),
        pl.BlockSpec((None, DN_WIDTH, tn), lambda i, j: (layer, 1, j)),
        pl.BlockSpec((None, half, tn), lambda i, j: (layer, c_blk, j)),
        pl.BlockSpec((None, half, tn), lambda i, j: (layer, c_blk + 1, j)),
        pl.BlockSpec((None, half, tn), lambda i, j: (layer, c_blk + 2, j)),
    ]
    return pl.pallas_call(
        _merge_kernel,
        grid=(seq // tm, D_MODEL // tn),
        in_specs=in_specs,
        out_specs=pl.BlockSpec((tm, tn), lambda i, j: (i, j)),
        out_shape=jax.ShapeDtypeStruct((seq, D_MODEL), BF16),
        compiler_params=_cparams(("parallel", "parallel")),
        name="gate_merge",
    )(*ys, h, h, h, h, w_branch, w_branch, w_branch, w_branch, w_branch)


def _rope_tables(seq):
    rows = seq // GRID_W
    row = np.repeat(np.arange(rows), GRID_W).astype(np.float64)
    col = np.tile(np.arange(GRID_W), rows).astype(np.float64)
    axis_dim = ATT_HEAD_DIM // 2
    freqs = ROPE_THETA ** (-np.arange(0, axis_dim, 2, dtype=np.float64) / axis_dim)
    ang = np.concatenate([row[:, None] * freqs, col[:, None] * freqs], axis=-1)
    cos = np.cos(ang).astype(np.float32)
    sin = np.sin(ang).astype(np.float32)
    return (jnp.asarray(np.repeat(cos, 2, axis=-1)),
            jnp.asarray(np.stack([-sin, sin], axis=-1).reshape(seq, ATT_HEAD_DIM)))


def _pair_rows(p):
    return p.reshape(2, SSM_PAIRS, 1, 2 * SSM_STATE).transpose(1, 0, 2, 3)


W_MAIN_COLS = 256


def _w_main_kernel(a_ref, b_ref, o_ref):
    n_ab = IN_AB_END - IN_AB
    past_ab = pl.program_id(1) * W_MAIN_COLS >= IN_AB

    @pl.when(jnp.logical_not(past_ab))
    def _():
        o_ref[...] = a_ref[...].astype(BF16)

    @pl.when(past_ab)
    def _():
        ext = jnp.concatenate([a_ref[...], b_ref[...]], axis=1)
        o_ref[...] = ext[:, n_ab:n_ab + W_MAIN_COLS].astype(BF16)


def _w_main(w_in):
    depth = w_in.shape[0]
    wc = W_MAIN_COLS
    assert IN_AB % wc == 0 and H_WIDTH % wc == 0 and IN_AB_END - IN_AB < 128
    return pl.pallas_call(
        _w_main_kernel,
        grid=(depth, H_WIDTH // wc),
        in_specs=[
            pl.BlockSpec((None, D_MODEL, wc), lambda l, j: (l, 0, j)),
            pl.BlockSpec((None, D_MODEL, 128), lambda l, j: (l, 0, (j + 1) * (wc // 128))),
        ],
        out_specs=pl.BlockSpec((None, D_MODEL, wc), lambda l, j: (l, 0, j)),
        out_shape=jax.ShapeDtypeStruct((depth, D_MODEL, H_WIDTH), BF16),
        compiler_params=_cparams(("parallel", "parallel")),
        name="w_main",
    )(w_in, w_in)


def _stacked_weights(w_in, w_branch, w_out):
    w_main = _w_main(w_in)
    w_ab = jnp.pad(w_in[:, :, IN_AB:IN_AB_END],
                   ((0, 0), (0, 0), (0, AB_PAD - (IN_AB_END - IN_AB)))).astype(BF16)
    return w_main, w_ab, w_branch.astype(BF16), w_out.astype(BF16)


def _layer_params(layer, ssm_a_re, ssm_a_im, ssm_log_step, ssm_b_re, ssm_b_im, ssm_c_re,
                  ssm_c_im, ssm_d, ssm_w_glu, ssm_b_glu, dn_conv, dn_a_log, dn_dt_bias,
                  dn_norm_g, attn_q_norm, attn_k_norm):
    ls_n = jnp.broadcast_to(ssm_log_step[layer][:, :, None], ssm_a_re[layer].shape)
    flat = lambda p: p.reshape(2, 1, SSM_GROUPS * SSM_STATE)

    def b_pairs(b):
        bt = b.transpose(0, 1, 3, 2).reshape(2, SSM_PAIRS, 2, SSM_GROUP, SSM_STATE)
        return bt.transpose(1, 0, 3, 2, 4).reshape(SSM_PAIRS, 2, SSM_GROUP, 2 * SSM_STATE)

    def c_pairs(c):
        ct = c.transpose(0, 1, 3, 2).reshape(2, SSM_PAIRS, 2 * SSM_STATE, SSM_GROUP)
        return ct.transpose(1, 0, 2, 3)

    conv = dn_conv[layer].T.reshape(DN_CONV, 3, DN_WIDTH).transpose(1, 0, 2)
    conv = jnp.pad(conv, ((0, 0), (0, 8 - DN_CONV), (0, 0)))

    def lane_vec(p):
        return jnp.pad(p.reshape(1, -1), ((0, 0), (0, AB_PAD - p.size)))

    return {
        "ssm_rows": [_pair_rows(ssm_a_re[layer]), _pair_rows(ssm_a_im[layer]), _pair_rows(ls_n)],
        "ssm_b": [b_pairs(ssm_b_re[layer]), b_pairs(ssm_b_im[layer])],
        "ssm_c": [c_pairs(ssm_c_re[layer]), c_pairs(ssm_c_im[layer])],
        "ssm_flat": [flat(ssm_a_re[layer]), flat(ssm_a_im[layer]), flat(ls_n)],
        "ssm_d": ssm_d[layer].reshape(1, SSM_WIDTH),
        "ssm_w_glu": ssm_w_glu[layer],
        "ssm_b_glu": ssm_b_glu[layer].reshape(1, SSM_WIDTH),
        "dn_conv": conv,
        "dn_alog": lane_vec(dn_a_log[layer]),
        "dn_dtb": lane_vec(dn_dt_bias[layer]),
        "dn_norm_g": dn_norm_g[layer].reshape(1, DN_HEAD_DIM),
        "att_gq": attn_q_norm[layer].reshape(1, ATT_HEAD_DIM),
        "att_gk": attn_k_norm[layer].reshape(1, ATT_HEAD_DIM),
    }


def kernel(x, mem, norm_g, w_in, ssm_a_re, ssm_a_im, ssm_log_step, ssm_b_re, ssm_b_im, ssm_c_re, ssm_c_im, ssm_d, ssm_w_glu, ssm_b_glu, dn_conv, dn_a_log, dn_dt_bias, dn_norm_g, attn_q_norm, attn_k_norm, mem_norm_g, w_mem_kv, w_branch, w_out, final_norm_g):
    bsz, seq, _ = x.shape
    depth = w_in.shape[0]
    cos2, sin2 = _rope_tables(seq)
    final_g = final_norm_g.reshape(1, D_MODEL)
    w_main, w_ab, w_br, w_o = _stacked_weights(w_in, w_branch, w_out)
    outs = []
    for b in range(bsz):
        xb = x[b]
        for layer in range(depth):
            lp = _layer_params(layer, ssm_a_re, ssm_a_im, ssm_log_step, ssm_b_re, ssm_b_im,
                               ssm_c_re, ssm_c_im, ssm_d, ssm_w_glu, ssm_b_glu, dn_conv, dn_a_log,
                               dn_dt_bias, dn_norm_g, attn_q_norm, attn_k_norm)
            h, ab = _inproj(xb, norm_g[layer].reshape(1, D_MODEL), w_main, w_ab, layer)
            y_a = _ssm_branch(h, lp)
            y_b = _dn_branch(h, ab, lp)
            qt, kr, vt = _att_prep(h, cos2, sin2, lp["att_gq"], lp["att_gk"])
            y_c = _att(qt, kr, vt, h)
            kv = _mem_kv(mem[b], mem_norm_g[layer].reshape(1, D_MODEL), w_mem_kv[layer])
            y_m = _mem_att(h, kv)
            merged = _merge((y_a, y_b, y_c, y_m), h, w_br, layer)
            xb = _outproj(merged, w_o, xb, final_g, layer, final_norm=(layer == depth - 1))
        outs.append(xb)
    return outs[0][None] if bsz == 1 else jnp.stack(outs, axis=0)
```

```python
import functools
import math

import numpy as np
import jax
import jax.numpy as jnp
from jax import lax
from jax.experimental import pallas as pl
from jax.experimental.pallas import tpu as pltpu

F32 = jnp.float32
BF16 = jnp.bfloat16

D_MODEL = 2048
GRID_W = 64
EPS = 1e-6

SSM_GROUP = 16
SSM_STATE = 64
SSM_GROUPS = 48
SSM_WIDTH = SSM_GROUPS * SSM_GROUP
SSM_T = 16
SSM_PAIRS = SSM_GROUPS // 2
SSM_CW = SSM_T * SSM_GROUP

DN_HEADS = 6
DN_HEAD_DIM = 128
DN_WIDTH = DN_HEADS * DN_HEAD_DIM
DN_CONV = 5
DN_CHUNK = 64
DN_GROUP = 256
DN_HB = 6

ATT_HEADS = 8
ATT_KV_HEADS = 2
ATT_HEAD_DIM = 128
ATT_WIDTH = ATT_HEADS * ATT_HEAD_DIM
ATT_KV_WIDTH = ATT_KV_HEADS * ATT_HEAD_DIM
ROPE_THETA = 10000.0

MEM_HEADS = 4
MEM_HEAD_DIM = 128
MEM_WIDTH = MEM_HEADS * MEM_HEAD_DIM

N_BRANCH = 4

OFF_UA = 0
OFF_ZA = 768
OFF_DQ = 1536
OFF_DK = 2304
OFF_DV = 3072
OFF_ZB = 3840
OFF_AQ = 4608
OFF_AK = 5632
OFF_AV = 5888
OFF_ZC = 6144
OFF_MQ = 7168
OFF_ZM = 7680
OFF_GATE = 8192
IN_AB = 3840
IN_AB_END = IN_AB + 4 * DN_HEADS
H_WIDTH = OFF_GATE + N_BRANCH * D_MODEL
AB_PAD = 128

VMEM_LIMIT = 56 * 1024 * 1024


def _cparams(sem):
    return pltpu.CompilerParams(dimension_semantics=sem, vmem_limit_bytes=VMEM_LIMIT)


def _silu(x):
    return x * jax.nn.sigmoid(x)


def _dot(a, b):
    return jnp.dot(a, b, preferred_element_type=F32)


def _dot_nt(a, b):
    return lax.dot_general(a, b, (((1,), (1,)), ((), ())), preferred_element_type=F32)


def _dot_tn(a, b):
    return lax.dot_general(a, b, (((0,), (0,)), ((), ())), preferred_element_type=F32)


def _split3(x):
    x1 = x.astype(BF16)
    r1 = x - x1.astype(F32)
    x2 = r1.astype(BF16)
    x3 = (r1 - x2.astype(F32)).astype(BF16)
    return x1, x2, x3


def _dot_exact_lhs(a_bf16, x):
    x1, x2, x3 = _split3(x)
    return _dot(a_bf16, x1) + _dot(a_bf16, x2) + _dot(a_bf16, x3)


def _dot_exact_rhs(x, b_bf16):
    x1, x2, x3 = _split3(x)
    return _dot(x1, b_bf16) + _dot(x2, b_bf16) + _dot(x3, b_bf16)


def _dot_f32(a, b):
    a1, a2, a3 = _split3(a)
    b1, b2, b3 = _split3(b)
    return (_dot(a1, b1) + (_dot(a1, b2) + _dot(a2, b1))
            + (_dot(a1, b3) + _dot(a2, b2) + _dot(a3, b1)))


def _inproj_kernel(x_ref, g_ref, w_ref, wab_ref, h_ref, ab_ref, xn_ref):
    @pl.when(pl.program_id(1) == 0)
    def _():
        x = x_ref[...]
        ms = jnp.mean(x * x, axis=-1, keepdims=True)
        xn_ref[...] = (x * lax.rsqrt(ms + EPS) * g_ref[...]).astype(BF16)
        ab_ref[...] = _dot(xn_ref[...], wab_ref[...])

    h_ref[...] = _dot(xn_ref[...], w_ref[...]).astype(BF16)


def _inproj(x, g, w_main, w_ab, layer):
    seq = x.shape[0]
    tm = min(1024, seq)
    tn = 2048
    return pl.pallas_call(
        _inproj_kernel,
        grid=(seq // tm, H_WIDTH // tn),
        in_specs=[
            pl.BlockSpec((tm, D_MODEL), lambda i, j: (i, 0)),
            pl.BlockSpec((1, D_MODEL), lambda i, j: (0, 0)),
            pl.BlockSpec((None, D_MODEL, tn), lambda i, j: (layer, 0, j)),
            pl.BlockSpec((None, D_MODEL, AB_PAD), lambda i, j: (layer, 0, 0)),
        ],
        out_specs=[
            pl.BlockSpec((tm, tn), lambda i, j: (i, j)),
            pl.BlockSpec((tm, AB_PAD), lambda i, j: (i, 0)),
        ],
        out_shape=[
            jax.ShapeDtypeStruct((seq, H_WIDTH), BF16),
            jax.ShapeDtypeStruct((seq, AB_PAD), F32),
        ],
        scratch_shapes=[pltpu.VMEM((tm, D_MODEL), BF16)],
        compiler_params=_cparams(("parallel", "arbitrary")),
        name="inproj",
    )(x, g, w_main, w_ab)


def _cpow_table(base_re, base_im, n):
    re = [jnp.ones_like(base_re)]
    im = [jnp.zeros_like(base_im)]
    for _ in range(n):
        re_n = re[-1] * base_re - im[-1] * base_im
        im_n = re[-1] * base_im + im[-1] * base_re
        re.append(re_n)
        im.append(im_n)
    return re, im


def _ssm_prep_kernel(are_r, aim_r, ls_r, bre_ref, bim_ref, cre_ref, cim_ref,
                     win_ref, wout_ref, m_ref):
    t = SSM_T
    cw = SSM_CW
    lane128 = lax.broadcasted_iota(jnp.int32, (1, 128), 1)
    col_t = lax.broadcasted_iota(jnp.int32, (1, cw), 1) // SSM_GROUP
    sub128 = lax.broadcasted_iota(jnp.int32, (128, 1), 0)
    lane_cw = lax.broadcasted_iota(jnp.int32, (SSM_GROUP, cw), 1)
    tile_p = jnp.where(lane_cw % SSM_GROUP == lax.broadcasted_iota(jnp.int32, (SSM_GROUP, cw), 0),
                       1.0, 0.0).astype(BF16)

    m_ref[...] = jnp.zeros(m_ref.shape, m_ref.dtype)

    krows = [[None, None], [None, None]]
    for d in range(2):
        step = jnp.exp(ls_r[d])
        a_re = are_r[d]
        a_im = aim_r[d]
        mag = jnp.exp(a_re * step)
        lam_re = mag * jnp.cos(a_im * step)
        lam_im = mag * jnp.sin(a_im * step)
        den = a_re * a_re + a_im * a_im
        nr = lam_re - 1.0
        ni = lam_im
        coef_re = (nr * a_re + ni * a_im) / den
        coef_im = (ni * a_re - nr * a_im) / den
        b_re = bre_ref[d]
        b_im = bim_ref[d]
        bbt_re = coef_re * b_re - coef_im * b_im
        bbt_im = coef_re * b_im + coef_im * b_re
        bb_re = jnp.concatenate([bbt_re] * t, axis=0)
        bb_im = jnp.concatenate([bbt_im] * t, axis=0)
        pr_re, pr_im = _cpow_table(lam_re, lam_im, t)
        e_in = [(t - 1 - s) if d == 0 else s for s in range(t)]
        p_re = jnp.concatenate([jnp.broadcast_to(pr_re[e], (SSM_GROUP, 128)) for e in e_in], axis=0)
        p_im = jnp.concatenate([jnp.broadcast_to(pr_im[e], (SSM_GROUP, 128)) for e in e_in], axis=0)
        w_re = p_re * bb_re - p_im * bb_im
        w_im = p_re * bb_im + p_im * bb_re
        for par in range(2):
            keep = (lane128 < 64) if par == 0 else (lane128 >= 64)
            win_ref[par * cw:(par + 1) * cw, (2 * d) * 128:(2 * d + 1) * 128] = (
                jnp.where(keep, w_re, 0.0).astype(BF16))
            win_ref[par * cw:(par + 1) * cw, (2 * d + 1) * 128:(2 * d + 2) * 128] = (
                jnp.where(keep, w_im, 0.0).astype(BF16))

        tab_re = jnp.zeros((128, 128), F32)
        tab_im = jnp.zeros((128, 128), F32)
        for e in range(t + 1):
            tab_re = jnp.where(sub128 == e, pr_re[e], tab_re)
            tab_im = jnp.where(sub128 == e, pr_im[e], tab_im)
        tab_re = tab_re.T
        tab_im = tab_im.T
        c_re = _dot_exact_rhs(cre_ref[d], tile_p)
        c_im = _dot_exact_rhs(cim_ref[d], tile_p)

        def c_lam(expo_row):
            sel = jnp.where(sub128 == expo_row, 1.0, 0.0).astype(BF16)
            q_re = _dot_exact_rhs(tab_re, sel)
            q_im = _dot_exact_rhs(tab_im, sel)
            return c_re * q_re - c_im * q_im, -(c_re * q_im + c_im * q_re)

        o_re, o_im = c_lam((col_t + 1) if d == 0 else (t - col_t))
        for par in range(2):
            keep = (sub128 < 64) if par == 0 else (sub128 >= 64)
            wout_ref[(2 * d) * 128:(2 * d + 1) * 128, par * cw:(par + 1) * cw] = (
                jnp.where(keep, o_re, 0.0).astype(BF16))
            wout_ref[(2 * d + 1) * 128:(2 * d + 2) * 128, par * cw:(par + 1) * cw] = (
                jnp.where(keep, o_im, 0.0).astype(BF16))

        r_re, r_im = c_lam(col_t if d == 0 else (t - 1 - col_t))
        for par in range(2):
            keep = (lane128 < 64) if par == 0 else (lane128 >= 64)
            krows[d][par] = (_dot_f32(jnp.where(keep, bbt_re, 0.0), r_re)
                             + _dot_f32(jnp.where(keep, bbt_im, 0.0), r_im))

    for par in range(2):
        kf = krows[0][par]
        kb = krows[1][par]
        for s in range(t):
            sh_f = SSM_GROUP * s
            blk = jnp.where(lane_cw >= sh_f, pltpu.roll(kf, sh_f, axis=1) if sh_f else kf, 0.0)
            sh_b = SSM_GROUP * (t - 1 - s)
            rolled_b = pltpu.roll(kb, cw - sh_b, axis=1) if sh_b else kb
            blk = blk + jnp.where(lane_cw < cw - sh_b, rolled_b, 0.0)
            m_ref[par * cw + s * SSM_GROUP:par * cw + (s + 1) * SSM_GROUP,
                  par * cw:(par + 1) * cw] = blk.astype(BF16)


def _ssm_prep(rows, b_t, c_t):
    pw = 2 * SSM_CW
    row_spec = pl.BlockSpec((None, 2, 1, 128), lambda g: (g, 0, 0, 0))
    b_spec = pl.BlockSpec((None, 2, SSM_GROUP, 128), lambda g: (g, 0, 0, 0))
    c_spec = pl.BlockSpec((None, 2, 128, SSM_GROUP), lambda g: (g, 0, 0, 0))
    w_spec = pl.BlockSpec((None, pw, pw), lambda g: (g, 0, 0))
    return pl.pallas_call(
        _ssm_prep_kernel,
        grid=(SSM_PAIRS,),
        in_specs=[row_spec] * 3 + [b_spec] * 2 + [c_spec] * 2,
        out_specs=[w_spec] * 3,
        out_shape=[jax.ShapeDtypeStruct((SSM_PAIRS, pw, pw), BF16)] * 3,
        compiler_params=_cparams(("parallel",)),
        name="ssm_prep",
    )(*rows, *b_t, *c_t)


SSM_BG = 8
SSM_BP = SSM_BG // 2
SSM_BW = SSM_BG * SSM_CW


def _ssm_place(t0):
    r = lax.broadcasted_iota(jnp.int32, (256, SSM_BW), 0)
    c = lax.broadcasted_iota(jnp.int32, (256, SSM_BW), 1)
    j = r % 128
    target = (j // SSM_GROUP) * SSM_CW + (t0 + r // 128) * SSM_GROUP + j % SSM_GROUP
    return jnp.where(c == target, 1.0, 0.0).astype(BF16)


def _ssm_in_kernel(u_ref, win_ref, u2_ref, h_ref, uf_ref):
    nchunk = u2_ref.shape[0]
    uf_ref[...] = u_ref[...].astype(F32)
    acc = jnp.zeros((nchunk, SSM_BW), F32)
    for t0 in range(0, SSM_T, 2):
        lhs = jnp.concatenate([uf_ref[pl.ds(t0, nchunk, stride=SSM_T), :],
                               uf_ref[pl.ds(t0 + 1, nchunk, stride=SSM_T), :]], axis=1)
        acc = acc + _dot(lhs.astype(BF16), _ssm_place(t0))
    u2_ref[...] = acc.astype(BF16)
    pw = 2 * SSM_CW
    for j in range(SSM_BP):
        h = _dot(u2_ref[:, j * pw:(j + 1) * pw], win_ref[j])
        for k in range(4):
            h_ref[k, :, j * 128:(j + 1) * 128] = h[:, k * 128:(k + 1) * 128]


def _ssm_in(h, win):
    seq = h.shape[0]
    nchunk = seq // SSM_T
    pw = 2 * SSM_CW
    nblk = SSM_GROUPS // SSM_BG
    return pl.pallas_call(
        _ssm_in_kernel,
        grid=(nblk,),
        in_specs=[
            pl.BlockSpec((seq, 128), lambda b: (0, OFF_UA // 128 + b)),
            pl.BlockSpec((SSM_BP, pw, pw), lambda b: (b, 0, 0)),
        ],
        out_specs=[
            pl.BlockSpec((nchunk, SSM_BW), lambda b: (0, b)),
            pl.BlockSpec((4, nchunk, SSM_BP * 128), lambda b: (0, 0, b)),
        ],
        out_shape=[
            jax.ShapeDtypeStruct((nchunk, SSM_GROUPS * SSM_CW), BF16),
            jax.ShapeDtypeStruct((4, nchunk, SSM_PAIRS * 128), F32),
        ],
        scratch_shapes=[pltpu.VMEM((seq, 128), F32)],
        compiler_params=_cparams(("parallel",)),
        name="ssm_in",
    )(h, win)


def _ssm_scan_kernel(are_ref, aim_ref, ls_ref, h_ref, p_ref, *, nchunk):
    width = h_ref.shape[2]

    def lam_pow_t(d):
        step = jnp.exp(ls_ref[d]) * float(SSM_T)
        mag = jnp.exp(are_ref[d] * step)
        ang = aim_ref[d] * step
        return mag * jnp.cos(ang), mag * jnp.sin(ang)

    lfr, lfi = lam_pow_t(0)
    lbr, lbi = lam_pow_t(1)

    def body(c, carry):
        fr, fi, br, bi = carry
        cb = nchunk - 1 - c
        p_ref[0, pl.ds(c, 1), :] = fr
        p_ref[1, pl.ds(c, 1), :] = fi
        p_ref[2, pl.ds(cb, 1), :] = br
        p_ref[3, pl.ds(cb, 1), :] = bi
        hfr = h_ref[0, pl.ds(c, 1), :]
        hfi = h_ref[1, pl.ds(c, 1), :]
        hbr = h_ref[2, pl.ds(cb, 1), :]
        hbi = h_ref[3, pl.ds(cb, 1), :]
        nfr = lfr * fr - lfi * fi + hfr
        nfi = lfr * fi + lfi * fr + hfi
        nbr = lbr * br - lbi * bi + hbr
        nbi = lbr * bi + lbi * br + hbi
        return nfr, nfi, nbr, nbi

    z = jnp.zeros((1, width), F32)
    lax.fori_loop(0, nchunk, body, (z, z, z, z))


def _ssm_scan(flat_params, h):
    nchunk = h.shape[1]
    width = h.shape[2]
    wt = 768
    return pl.pallas_call(
        functools.partial(_ssm_scan_kernel, nchunk=nchunk),
        grid=(width // wt,),
        in_specs=[pl.BlockSpec((2, 1, wt), lambda j: (0, 0, j))] * 3 + [
            pl.BlockSpec((4, nchunk, wt), lambda j: (0, 0, j)),
        ],
        out_specs=pl.BlockSpec((4, nchunk, wt), lambda j: (0, 0, j)),
        out_shape=jax.ShapeDtypeStruct(h.shape, F32),
        compiler_params=_cparams(("parallel",)),
        name="ssm_scan",
    )(*flat_params, h)


def _ssm_out_kernel(u2_ref, p_ref, m_ref, wout_ref, y_ref):
    nchunk = u2_ref.shape[0]
    pw = 2 * SSM_CW
    parts = []
    for j in range(SSM_BP):
        acc = _dot(u2_ref[:, j * pw:(j + 1) * pw], m_ref[j])
        for k in range(4):
            acc = acc + _dot(p_ref[k, :, j * 128:(j + 1) * 128].astype(BF16),
                             wout_ref[j, k * 128:(k + 1) * 128, :])
        parts.append(acc.astype(BF16))
    y16 = jnp.concatenate(parts, axis=1)
    for t0 in range(0, SSM_T, 2):
        yt = _dot_nt(y16, _ssm_place(t0))
        y_ref[pl.ds(t0, nchunk, stride=SSM_T), :] = yt[:, :128]
        y_ref[pl.ds(t0 + 1, nchunk, stride=SSM_T), :] = yt[:, 128:]


def _ssm_out(u2, p, m, wout):
    nchunk = u2.shape[0]
    seq = nchunk * SSM_T
    pw = 2 * SSM_CW
    nblk = SSM_GROUPS // SSM_BG
    return pl.pallas_call(
        _ssm_out_kernel,
        grid=(nblk,),
        in_specs=[
            pl.BlockSpec((nchunk, SSM_BW), lambda b: (0, b)),
            pl.BlockSpec((4, nchunk, SSM_BP * 128), lambda b: (0, 0, b)),
            pl.BlockSpec((SSM_BP, pw, pw), lambda b: (b, 0, 0)),
            pl.BlockSpec((SSM_BP, pw, pw), lambda b: (b, 0, 0)),
        ],
        out_specs=pl.BlockSpec((seq, 128), lambda b: (0, b)),
        out_shape=jax.ShapeDtypeStruct((seq, SSM_WIDTH), F32),
        compiler_params=_cparams(("parallel",)),
        name="ssm_out",
    )(u2, p, m, wout)


def _ssm_epi_kernel(y_ref, u_ref, z_ref, d_ref, wg_ref, bg_ref, o_ref):
    y = y_ref[...] + d_ref[...] * u_ref[...].astype(F32)
    y = jax.nn.gelu(y)
    glu = _dot(y.astype(BF16), wg_ref[...].astype(BF16)) + bg_ref[...]
    y = y * jax.nn.sigmoid(glu)
    o_ref[...] = (y * _silu(z_ref[...].astype(F32))).astype(BF16)


def _ssm_epi(y, h, d, w_glu, b_glu):
    seq = y.shape[0]
    tm = min(512, seq)
    w = SSM_WIDTH
    return pl.pallas_call(
        _ssm_epi_kernel,
        grid=(seq // tm,),
        in_specs=[
            pl.BlockSpec((tm, w), lambda i: (i, 0)),
            pl.BlockSpec((tm, w), lambda i: (i, OFF_UA // w)),
            pl.BlockSpec((tm, w), lambda i: (i, OFF_ZA // w)),
            pl.BlockSpec((1, w), lambda i: (0, 0)),
            pl.BlockSpec((w, w), lambda i: (0, 0)),
            pl.BlockSpec((1, w), lambda i: (0, 0)),
        ],
        out_specs=pl.BlockSpec((tm, w), lambda i: (i, 0)),
        out_shape=jax.ShapeDtypeStruct((seq, w), BF16),
        compiler_params=_cparams(("parallel",)),
        name="ssm_epi",
    )(y, h, h, d, w_glu, b_glu)


def _ssm_branch(h, lp):
    win, wout, m = _ssm_prep(lp["ssm_rows"], lp["ssm_b"], lp["ssm_c"])
    u2, hs = _ssm_in(h, win)
    p = _ssm_scan(lp["ssm_flat"], hs)
    y = _ssm_out(u2, p, m, wout)
    return _ssm_epi(y, h, lp["ssm_d"], lp["ssm_w_glu"], lp["ssm_b_glu"])


def _dn_prep_kernel(qc, qp, qn, kc, kp, kn, vc, vp, vn, ab_ref, cw_ref, alog_ref, dtb_ref,
                    qo_ref, ko_ref, vo_ref, sc_ref, sct_ref, ext_ref, *, tm, nblk):
    i = pl.program_id(0)
    halo = 16
    pad = DN_CONV // 2

    def conv_silu(cur, prev, nxt, part):
        ext_ref[0:halo, :] = jnp.where(i > 0, prev[...].astype(F32), 0.0)
        ext_ref[halo:halo + tm, :] = cur[...].astype(F32)
        ext_ref[halo + tm:halo + tm + halo, :] = jnp.where(i < nblk - 1, nxt[...].astype(F32), 0.0)
        acc = jnp.zeros((tm, DN_WIDTH), F32)
        for j in range(DN_CONV):
            acc = acc + cw_ref[part, j:j + 1, :] * ext_ref[pl.ds(halo - pad + j, tm), :]
        return _silu(acc)

    def l2n(x):
        outs = []
        for hd in range(DN_HEADS):
            xh = x[:, hd * DN_HEAD_DIM:(hd + 1) * DN_HEAD_DIM]
            outs.append(xh * lax.rsqrt(jnp.sum(xh * xh, axis=-1, keepdims=True) + EPS))
        return jnp.concatenate(outs, axis=1)

    qo_ref[...] = (l2n(conv_silu(qc, qp, qn, 0)) * (DN_HEAD_DIM ** -0.5)).astype(BF16)
    ko_ref[...] = l2n(conv_silu(kc, kp, kn, 1)).astype(BF16)
    vo_ref[...] = conv_silu(vc, vp, vn, 2).astype(BF16)

    ab = ab_ref[...]
    g_all = -jnp.exp(alog_ref[...]) * jax.nn.softplus(ab + dtb_ref[...])
    beta_all = jax.nn.sigmoid(ab)
    r = lax.broadcasted_iota(jnp.int32, (tm, tm), 0)
    c = lax.broadcasted_iota(jnp.int32, (tm, tm), 1)
    same = (r // DN_CHUNK) == (c // DN_CHUNK)
    tri_f = jnp.where(same & (c <= r), 1.0, 0.0).astype(BF16)
    tri_b = jnp.where(same & (c >= r), 1.0, 0.0).astype(BF16)
    blk = jnp.where(same, 1.0, 0.0).astype(BF16)
    gcf = _dot_exact_lhs(tri_f, g_all)
    gcb = _dot_exact_lhs(tri_b, g_all)
    gtot = _dot_exact_lhs(blk, g_all)
    lane = lax.broadcasted_iota(jnp.int32, (tm, AB_PAD), 1)
    sc = jnp.where(lane < 6, gcf,
                   jnp.where(lane < 12, gcb,
                             jnp.where(lane < 24, beta_all,
                                       jnp.where(lane < 36, pltpu.roll(gtot, 24, axis=1), 0.0))))
    sc_ref[...] = sc
    sct_ref[...] = sc.T


def _dn_prep(h, ab, conv_w, alog, dtb):
    seq = h.shape[0]
    tm = min(256, seq)
    nblk = seq // tm
    w = DN_WIDTH
    hb = tm // 16
    nh = seq // 16

    def cur(ci):
        return pl.BlockSpec((tm, w), lambda i: (i, ci))

    def prev(ci):
        return pl.BlockSpec((16, w), lambda i: (jnp.maximum(i * hb - 1, 0), ci))

    def nxt(ci):
        return pl.BlockSpec((16, w), lambda i: (jnp.minimum((i + 1) * hb, nh - 1), ci))

    in_specs = []
    for off in (OFF_DQ, OFF_DK, OFF_DV):
        ci = off // w
        in_specs += [cur(ci), prev(ci), nxt(ci)]
    in_specs += [
        pl.BlockSpec((tm, AB_PAD), lambda i: (i, 0)),
        pl.BlockSpec((3, 8, w), lambda i: (0, 0, 0)),
        pl.BlockSpec((1, AB_PAD), lambda i: (0, 0)),
        pl.BlockSpec((1, AB_PAD), lambda i: (0, 0)),
    ]
    return pl.pallas_call(
        functools.partial(_dn_prep_kernel, tm=tm, nblk=nblk),
        grid=(nblk,),
        in_specs=in_specs,
        out_specs=[pl.BlockSpec((tm, w), lambda i: (i, 0))] * 3
        + [pl.BlockSpec((tm, AB_PAD), lambda i: (i, 0)), pl.BlockSpec((AB_PAD, tm), lambda i: (0, i))],
        out_shape=[jax.ShapeDtypeStruct((seq, w), BF16)] * 3
        + [jax.ShapeDtypeStruct((seq, AB_PAD), F32), jax.ShapeDtypeStruct((AB_PAD, seq), F32)],
        scratch_shapes=[pltpu.VMEM((tm + 32, w), F32)],
        compiler_params=_cparams(("parallel",)),
        name="dn_prep",
    )(h, h, h, h, h, h, h, h, h, ab, conv_w, alog, dtb)


def _dn_main_kernel(qf_ref, kf_ref, vf_ref, scf_ref, sctf_ref,
                    qb_ref, kb_ref, vb_ref, scb_ref, sctb_ref, of_ref, ob_ref, s_ref):
    gsz = DN_GROUP
    nck = gsz // DN_CHUNK

    @pl.when(pl.program_id(1) == 0)
    def _():
        s_ref[...] = jnp.zeros(s_ref.shape, F32)

    refs = ((qf_ref, kf_ref, vf_ref, scf_ref, sctf_ref, of_ref),
            (qb_ref, kb_ref, vb_ref, scb_ref, sctb_ref, ob_ref))
    r = lax.broadcasted_iota(jnp.int32, (gsz, gsz), 0)
    c = lax.broadcasted_iota(jnp.int32, (gsz, gsz), 1)
    same = (r // DN_CHUNK) == (c // DN_CHUNK)
    incl = (same & (r >= c), same & (r <= c))
    strict = (same & (r > c), same & (r < c))

    chains = []
    for d in range(2):
        q_ref, k_ref, v_ref, sc_ref, sct_ref, o_ref = refs[d]
        sc = sc_ref[...]
        sct = sct_ref[...]
        lane = lax.broadcasted_iota(jnp.int32, sc.shape, 1)
        sub = lax.broadcasted_iota(jnp.int32, sct.shape, 0)
        for hl in range(DN_HB):
            cidx = d * DN_HEADS + pl.program_id(0) * DN_HB + hl
            hs = slice(hl * DN_HEAD_DIM, (hl + 1) * DN_HEAD_DIM)

            def col(ci, sc=sc, lane=lane):
                return jnp.sum(jnp.where(lane == ci, sc, 0.0), axis=1, keepdims=True)

            ch = {"d": d, "hl": hl, "hs": hs, "o_ref": o_ref}
            ch["q"] = q_ref[:, hs]
            ch["k"] = k_ref[:, hs]
            ch["v"] = v_ref[:, hs]
            ch["gc_col"] = col(cidx)
            ch["gc_row"] = jnp.sum(jnp.where(sub == cidx, sct, 0.0), axis=0, keepdims=True)
            ch["beta_col"] = col(12 + cidx)
            ch["gtot_col"] = col(24 + cidx)
            chains.append(ch)

    for ch in chains:
        ch["kf"] = ch["k"].astype(F32)
        ch["kb"] = ch["kf"] * ch["beta_col"]
        ch["gram"] = _dot_nt(ch["kb"].astype(BF16), ch["k"])
    for ch in chains:
        ch["qk"] = _dot_nt(ch["q"], ch["k"])
    for ch in chains:
        d = ch["d"]
        ch["decay"] = jnp.where(incl[d], jnp.exp(jnp.where(incl[d], ch["gc_col"] - ch["gc_row"], 0.0)), 0.0)
        ch["x"] = jnp.concatenate([ch["v"].astype(F32) * ch["beta_col"],
                                   ch["kb"] * jnp.exp(ch["gc_col"])], axis=1)
        ch["qd16"] = (ch["q"].astype(F32) * jnp.exp(ch["gc_col"])).astype(BF16)
        ch["kd16"] = (ch["kf"] * jnp.exp(ch["gtot_col"] - ch["gc_col"])).astype(BF16)
    for ch in chains:
        ch["p"] = jnp.where(strict[ch["d"]], ch["gram"] * ch["decay"], 0.0).astype(BF16)
        ch["intra"] = (ch["qk"] * ch["decay"]).astype(BF16)
    for ch in chains:
        ch["x"] = ch["x"] - _dot(ch["p"], ch["x"].astype(BF16))
    npow = 2
    while npow < DN_CHUNK:
        for ch in chains:
            ch["p"] = _dot(ch["p"], ch["p"]).astype(BF16)
        for ch in chains:
            ch["x"] = ch["x"] + _dot(ch["p"], ch["x"].astype(BF16))
        npow *= 2
    for ch in chains:
        ch["u"] = ch["x"][:, :DN_HEAD_DIM]
        ch["w16"] = ch["x"][:, DN_HEAD_DIM:].astype(BF16)
        ch["s"] = s_ref[ch["d"], ch["hl"]]
        ch["vnew"] = [None] * nck
        ch["oq"] = [None] * nck

    for step in range(nck):
        for ch in chains:
            j = step if ch["d"] == 0 else nck - 1 - step
            lo, hi = j * DN_CHUNK, (j + 1) * DN_CHUNK
            s16 = ch["s"].astype(BF16)
            ws = _dot(jnp.concatenate([ch["w16"][lo:hi], ch["qd16"][lo:hi]], axis=0), s16)
            vn = ch["u"][lo:hi] - ws[:DN_CHUNK]
            ch["oq"][j] = ws[DN_CHUNK:]
            ch["vnew"][j] = vn
            ch["s"] = (ch["s"] * jnp.exp(ch["gtot_col"][lo:lo + 1, :])
                       + _dot_tn(ch["kd16"][lo:hi], vn.astype(BF16)))
    for ch in chains:
        s_ref[ch["d"], ch["hl"]] = ch["s"]
        vn_all = jnp.concatenate(ch["vnew"], axis=0).astype(BF16)
        ch["o_ref"][:, ch["hs"]] = jnp.concatenate(ch["oq"], axis=0) + _dot(ch["intra"], vn_all)


def _dn_main(qn, kn, vc, sc, sct):
    seq = qn.shape[0]
    gsz = DN_GROUP
    ng = seq // gsz
    hw = DN_HB * DN_HEAD_DIM
    sct_rows = sct.shape[0]
    fwd = lambda h, i: (i, h)
    bwd = lambda h, i: (ng - 1 - i, h)
    in_specs = [
        pl.BlockSpec((gsz, hw), fwd), pl.BlockSpec((gsz, hw), fwd), pl.BlockSpec((gsz, hw), fwd),
        pl.BlockSpec((gsz, AB_PAD), lambda h, i: (i, 0)),
        pl.BlockSpec((sct_rows, gsz), lambda h, i: (0, i)),
        pl.BlockSpec((gsz, hw), bwd), pl.BlockSpec((gsz, hw), bwd), pl.BlockSpec((gsz, hw), bwd),
        pl.BlockSpec((gsz, AB_PAD), lambda h, i: (ng - 1 - i, 0)),
        pl.BlockSpec((sct_rows, gsz), lambda h, i: (0, ng - 1 - i)),
    ]
    return pl.pallas_call(
        _dn_main_kernel,
        grid=(DN_HEADS // DN_HB, ng),
        in_specs=in_specs,
        out_specs=[pl.BlockSpec((gsz, hw), fwd), pl.BlockSpec((gsz, hw), bwd)],
        out_shape=[jax.ShapeDtypeStruct((seq, DN_WIDTH), F32)] * 2,
        scratch_shapes=[pltpu.VMEM((2, DN_HB, DN_HEAD_DIM, DN_HEAD_DIM), F32)],
        compiler_params=_cparams(("parallel", "arbitrary")),
        name="dn_main",
    )(qn, kn, vc, sc, sct, qn, kn, vc, sc, sct)


def _dn_epi_kernel(of_ref, ob_ref, z_ref, g_ref, o_ref):
    o = of_ref[...] + ob_ref[...]
    g = g_ref[...]
    outs = []
    for hd in range(DN_HEADS):
        oh = o[:, hd * DN_HEAD_DIM:(hd + 1) * DN_HEAD_DIM]
        ms = jnp.mean(oh * oh, axis=-1, keepdims=True)
        outs.append(oh * lax.rsqrt(ms + EPS) * g)
    y = jnp.concatenate(outs, axis=1)
    o_ref[...] = (y * _silu(z_ref[...].astype(F32))).astype(BF16)


def _dn_epi(o_f, o_b, h, norm_g):
    seq = o_f.shape[0]
    tm = min(512, seq)
    w = DN_WIDTH
    return pl.pallas_call(
        _dn_epi_kernel,
        grid=(seq // tm,),
        in_specs=[
            pl.BlockSpec((tm, w), lambda i: (i, 0)),
            pl.BlockSpec((tm, w), lambda i: (i, 0)),
            pl.BlockSpec((tm, w), lambda i: (i, OFF_ZB // w)),
            pl.BlockSpec((1, DN_HEAD_DIM), lambda i: (0, 0)),
        ],
        out_specs=pl.BlockSpec((tm, w), lambda i: (i, 0)),
        out_shape=jax.ShapeDtypeStruct((seq, w), BF16),
        compiler_params=_cparams(("parallel",)),
        name="dn_epi",
    )(o_f, o_b, h, norm_g)


def _dn_branch(h, ab, lp):
    qn, kn, vc, sc, sct = _dn_prep(h, ab, lp["dn_conv"], lp["dn_alog"], lp["dn_dtb"])
    o_f, o_b = _dn_main(qn, kn, vc, sc, sct)
    return _dn_epi(o_f, o_b, h, lp["dn_norm_g"])


def _att_prep_kernel(qlo_ref, qhi_ref, k_ref, v_ref, cos_ref, sin_ref, gq_ref, gk_ref,
                     qt_ref, ko_ref, vt_ref):
    cos2 = cos_ref[...]
    sin2 = sin_ref[...]
    rr = lax.broadcasted_iota(jnp.int32, (ATT_HEAD_DIM, ATT_HEAD_DIM), 0)
    cc = lax.broadcasted_iota(jnp.int32, (ATT_HEAD_DIM, ATT_HEAD_DIM), 1)
    eye = jnp.where(rr == cc, 1.0, 0.0).astype(BF16)
    swap = jnp.where(rr == (cc ^ 1), 1.0, 0.0).astype(BF16)

    gq = gq_ref[...]
    gk = gk_ref[...]
    scale = ATT_HEAD_DIM ** -0.5 * math.log2(math.e)

    def transposed(x16):
        return _dot_nt(eye, x16).astype(BF16)

    half = ATT_HEADS // 2
    xs = []
    for hd in range(ATT_HEADS):
        src = qlo_ref if hd < half else qhi_ref
        xs.append((src[:, (hd % half) * ATT_HEAD_DIM:(hd % half + 1) * ATT_HEAD_DIM], gq))
    for hd in range(ATT_KV_HEADS):
        xs.append((k_ref[:, hd * ATT_HEAD_DIM:(hd + 1) * ATT_HEAD_DIM], gk))
    xn = []
    for x16, g in xs:
        x = x16.astype(F32)
        ms = jnp.mean(x * x, axis=-1, keepdims=True)
        xn.append(x * lax.rsqrt(ms + EPS) * g)
    his = [v.astype(BF16) for v in xn]
    los = [(v - hi.astype(F32)).astype(BF16) for v, hi in zip(xn, his)]
    partner = [_dot(hi, swap) + _dot(lo, swap) for hi, lo in zip(his, los)]
    roped = [v * cos2 + pt * sin2 for v, pt in zip(xn, partner)]
    qts = [transposed((roped[hd] * scale).astype(BF16)) for hd in range(ATT_HEADS)]
    vts = [transposed(v_ref[:, hd * ATT_HEAD_DIM:(hd + 1) * ATT_HEAD_DIM]) for hd in range(ATT_KV_HEADS)]
    for hd in range(ATT_HEADS):
        qt_ref[hd * ATT_HEAD_DIM:(hd + 1) * ATT_HEAD_DIM, :] = qts[hd]
    for hd in range(ATT_KV_HEADS):
        sl = slice(hd * ATT_HEAD_DIM, (hd + 1) * ATT_HEAD_DIM)
        ko_ref[:, sl] = roped[ATT_HEADS + hd].astype(BF16)
        vt_ref[sl, :] = vts[hd]


def _att_prep(h, cos2, sin2, gq, gk):
    seq = h.shape[0]
    tm = min(512, seq)
    return pl.pallas_call(
        _att_prep_kernel,
        grid=(seq // tm,),
        in_specs=[
            pl.BlockSpec((tm, ATT_WIDTH // 2), lambda i: (i, OFF_AQ // (ATT_WIDTH // 2))),
            pl.BlockSpec((tm, ATT_WIDTH // 2), lambda i: (i, OFF_AQ // (ATT_WIDTH // 2) + 1)),
            pl.BlockSpec((tm, ATT_KV_WIDTH), lambda i: (i, OFF_AK // ATT_KV_WIDTH)),
            pl.BlockSpec((tm, ATT_KV_WIDTH), lambda i: (i, OFF_AV // ATT_KV_WIDTH)),
            pl.BlockSpec((tm, ATT_HEAD_DIM), lambda i: (i, 0)),
            pl.BlockSpec((tm, ATT_HEAD_DIM), lambda i: (i, 0)),
            pl.BlockSpec((1, ATT_HEAD_DIM), lambda i: (0, 0)),
            pl.BlockSpec((1, ATT_HEAD_DIM), lambda i: (0, 0)),
        ],
        out_specs=[
            pl.BlockSpec((ATT_WIDTH, tm), lambda i: (0, i)),
            pl.BlockSpec((tm, ATT_KV_WIDTH), lambda i: (i, 0)),
            pl.BlockSpec((ATT_KV_WIDTH, tm), lambda i: (0, i)),
        ],
        out_shape=[
            jax.ShapeDtypeStruct((ATT_WIDTH, seq), BF16),
            jax.ShapeDtypeStruct((seq, ATT_KV_WIDTH), BF16),
            jax.ShapeDtypeStruct((ATT_KV_WIDTH, seq), BF16),
        ],
        compiler_params=_cparams(("parallel",)),
        name="att_prep",
    )(h, h, h, h, cos2, sin2, gq, gk)


def _att_kernel(qt_ref, k_ref, vt_ref, z_ref, o_ref, m_ref, l_ref, acc_ref, *, kc, nkc):
    grp = qt_ref.shape[0] // ATT_HEAD_DIM
    m_ref[...] = jnp.full(m_ref.shape, -1e30, F32)
    l_ref[...] = jnp.zeros(l_ref.shape, F32)
    acc_ref[...] = jnp.zeros(acc_ref.shape, F32)

    def chunk(c, carry):
        k0 = pl.multiple_of(c * kc, kc)
        k_c = k_ref[pl.ds(k0, kc), :]
        vt_c = vt_ref[:, pl.ds(k0, kc)]
        s = [_dot(k_c, qt_ref[hh * ATT_HEAD_DIM:(hh + 1) * ATT_HEAD_DIM, :])
             for hh in range(grp)]
        p = [None] * grp
        alpha = [None] * grp
        for hh in range(grp):
            m_old = m_ref[hh]
            m_new = jnp.maximum(m_old, jnp.max(s[hh], axis=0, keepdims=True))
            alpha[hh] = jnp.exp2(m_old - m_new)
            ph = jnp.exp2(s[hh] - m_new)
            l_ref[hh] = alpha[hh] * l_ref[hh] + jnp.sum(ph, axis=0, keepdims=True)
            m_ref[hh] = m_new
            p[hh] = ph.astype(BF16)
        for hh in range(grp):
            acc_ref[hh] = alpha[hh] * acc_ref[hh] + _dot(vt_c, p[hh])
        return carry

    lax.fori_loop(0, nkc, chunk, 0)
    for hh in range(grp):
        sl = slice(hh * ATT_HEAD_DIM, (hh + 1) * ATT_HEAD_DIM)
        o = (acc_ref[hh] / l_ref[hh]).T
        o_ref[:, sl] = (o * _silu(z_ref[:, sl].astype(F32))).astype(BF16)


ATT_HPS = 4


def _att(qt, kr, vt, h):
    seq = kr.shape[0]
    tq = min(256, seq)
    kc = min(4096, seq)
    hw = ATT_HPS * ATT_HEAD_DIM
    steps_per_kv = ATT_HEADS // ATT_KV_HEADS // ATT_HPS
    return pl.pallas_call(
        functools.partial(_att_kernel, kc=kc, nkc=seq // kc),
        grid=(ATT_HEADS // ATT_HPS, seq // tq),
        in_specs=[
            pl.BlockSpec((hw, tq), lambda a, i: (a, i)),
            pl.BlockSpec((seq, ATT_HEAD_DIM), lambda a, i: (0, a // steps_per_kv)),
            pl.BlockSpec((ATT_HEAD_DIM, seq), lambda a, i: (a // steps_per_kv, 0)),
            pl.BlockSpec((tq, hw), lambda a, i: (i, OFF_ZC // hw + a)),
        ],
        out_specs=pl.BlockSpec((tq, hw), lambda a, i: (i, a)),
        out_shape=jax.ShapeDtypeStruct((seq, ATT_WIDTH), BF16),
        scratch_shapes=[
            pltpu.VMEM((ATT_HPS, 1, tq), F32),
            pltpu.VMEM((ATT_HPS, 1, tq), F32),
            pltpu.VMEM((ATT_HPS, ATT_HEAD_DIM, tq), F32),
        ],
        compiler_params=_cparams(("parallel", "parallel")),
        name="grid_att",
    )(qt, kr, vt, h)


def _mem_kv_kernel(mem_ref, g_ref, w_ref, o_ref):
    x = mem_ref[...]
    ms = jnp.mean(x * x, axis=-1, keepdims=True)
    xn = (x * lax.rsqrt(ms + EPS) * g_ref[...]).astype(BF16)
    o_ref[...] = _dot(xn, w_ref[...].astype(BF16)).astype(BF16)


def _mem_kv(mem, g, w_kv):
    n_mem = mem.shape[0]
    tn = 512
    return pl.pallas_call(
        _mem_kv_kernel,
        grid=(2 * MEM_WIDTH // tn,),
        in_specs=[
            pl.BlockSpec((n_mem, D_MODEL), lambda j: (0, 0)),
            pl.BlockSpec((1, D_MODEL), lambda j: (0, 0)),
            pl.BlockSpec((D_MODEL, tn), lambda j: (0, j)),
        ],
        out_specs=pl.BlockSpec((n_mem, tn), lambda j: (0, j)),
        out_shape=jax.ShapeDtypeStruct((n_mem, 2 * MEM_WIDTH), BF16),
        compiler_params=_cparams(("parallel",)),
        name="mem_kv",
    )(mem, g, w_kv)


def _mem_att_kernel(q_ref, z_ref, kv_ref, o_ref):
    scale = MEM_HEAD_DIM ** -0.5
    sls = [slice(hd * MEM_HEAD_DIM, (hd + 1) * MEM_HEAD_DIM) for hd in range(MEM_HEADS)]
    s = [_dot_nt(q_ref[:, sl], kv_ref[:, sl]) * scale for sl in sls]
    p = []
    l = []
    for hd in range(MEM_HEADS):
        m = jnp.max(s[hd], axis=-1, keepdims=True)
        ph = jnp.exp(s[hd] - m)
        l.append(jnp.sum(ph, axis=-1, keepdims=True))
        p.append(ph.astype(BF16))
    o = [_dot(p[hd], kv_ref[:, MEM_WIDTH + hd * MEM_HEAD_DIM:MEM_WIDTH + (hd + 1) * MEM_HEAD_DIM]) / l[hd]
         for hd in range(MEM_HEADS)]
    for hd, sl in enumerate(sls):
        o_ref[:, sl] = (o[hd] * _silu(z_ref[:, sl].astype(F32))).astype(BF16)


def _mem_att(h, kv):
    seq = h.shape[0]
    tm = min(1024, seq)
    n_mem = kv.shape[0]
    return pl.pallas_call(
        _mem_att_kernel,
        grid=(seq // tm,),
        in_specs=[
            pl.BlockSpec((tm, MEM_WIDTH), lambda i: (i, OFF_MQ // MEM_WIDTH)),
            pl.BlockSpec((tm, MEM_WIDTH), lambda i: (i, OFF_ZM // MEM_WIDTH)),
            pl.BlockSpec((n_mem, 2 * MEM_WIDTH), lambda i: (0, 0)),
        ],
        out_specs=pl.BlockSpec((tm, MEM_WIDTH), lambda i: (i, 0)),
        out_shape=jax.ShapeDtypeStruct((seq, MEM_WIDTH), BF16),
        compiler_params=_cparams(("parallel",)),
        name="mem_att",
    )(h, h, kv)


def _merge_kernel(ya_ref, yb_ref, yc_ref, ym_ref, ga_ref, gb_ref, gc_ref, gm_ref,
                  wa_ref, wb_ref, wc0_ref, wc1_ref, wm_ref, o_ref):
    half = ATT_WIDTH // 2
    ya_w = _dot(ya_ref[...], wa_ref[...])
    yb_w = _dot(yb_ref[...], wb_ref[...])
    yc_w = _dot(yc_ref[:, :half], wc0_ref[...]) + _dot(yc_ref[:, half:], wc1_ref[...])
    ym_w = _dot(ym_ref[...], wm_ref[...])
    merged = jax.nn.sigmoid(ga_ref[...].astype(F32)) * ya_w
    merged = merged + jax.nn.sigmoid(gb_ref[...].astype(F32)) * yb_w
    merged = merged + jax.nn.sigmoid(gc_ref[...].astype(F32)) * yc_w
    merged = merged + jax.nn.sigmoid(gm_ref[...].astype(F32)) * ym_w
    o_ref[...] = merged.astype(BF16)


def _outproj_kernel(m_ref, wo_ref, x_ref, fg_ref, o_ref, *, final_norm):
    y = x_ref[...] + _dot(m_ref[...], wo_ref[...])
    if final_norm:
        ms = jnp.mean(y * y, axis=-1, keepdims=True)
        y = y * lax.rsqrt(ms + EPS) * fg_ref[...]
    o_ref[...] = y


def _outproj(merged, w_out, x, final_g, layer, final_norm):
    seq = x.shape[0]
    tm = min(512, seq)
    return pl.pallas_call(
        functools.partial(_outproj_kernel, final_norm=final_norm),
        grid=(seq // tm,),
        in_specs=[
            pl.BlockSpec((tm, D_MODEL), lambda i: (i, 0)),
            pl.BlockSpec((None, D_MODEL, D_MODEL), lambda i: (layer, 0, 0)),
            pl.BlockSpec((tm, D_MODEL), lambda i: (i, 0)),
            pl.BlockSpec((1, D_MODEL), lambda i: (0, 0)),
        ],
        out_specs=pl.BlockSpec((tm, D_MODEL), lambda i: (i, 0)),
        out_shape=jax.ShapeDtypeStruct((seq, D_MODEL), F32),
        compiler_params=_cparams(("parallel",)),
        name="out_proj",
    )(merged, w_out, x, final_g)


def _merge(ys, h, w_branch, layer):
    seq = h.shape[0]
    tm = min(1024, seq)
    tn = 512
    widths = (SSM_WIDTH, DN_WIDTH, ATT_WIDTH, MEM_WIDTH)
    half = ATT_WIDTH // 2
    assert SSM_WIDTH == DN_WIDTH and (SSM_WIDTH + DN_WIDTH) % half == 0 and MEM_WIDTH == half
    c_blk = (SSM_WIDTH + DN_WIDTH) // half
    in_specs = [pl.BlockSpec((tm, wd), lambda i, j: (i, 0)) for wd in widths]
    for b in range(N_BRANCH):
        base = (OFF_GATE + b * D_MODEL) // tn
        in_specs.append(pl.BlockSpec((tm, tn), lambda i, j, base=base: (i, base + j)))
    in_specs += [
        pl.BlockSpec((None, SSM_WIDTH, tn), lambda i, j: (layer, 0, j)),
        pl.BlockSpec((None, DN_WIDTH, tn), lambda i, j: (layer, 1, j)),
        pl.BlockSpec((None, half, tn), lambda i, j: (layer, c_blk, j)),
        pl.BlockSpec((None, half, tn), lambda i, j: (layer, c_blk + 1, j)),
        pl.BlockSpec((None, half, tn), lambda i, j: (layer, c_blk + 2, j)),
    ]
    return pl.pallas_call(
        _merge_kernel,
        grid=(seq // tm, D_MODEL // tn),
        in_specs=in_specs,
        out_specs=pl.BlockSpec((tm, tn), lambda i, j: (i, j)),
        out_shape=jax.ShapeDtypeStruct((seq, D_MODEL), BF16),
        compiler_params=_cparams(("parallel", "parallel")),
        name="gate_merge",
    )(*ys, h, h, h, h, w_branch, w_branch, w_branch, w_branch, w_branch)


def _rope_tables(seq):
    rows = seq // GRID_W
    row = np.repeat(np.arange(rows), GRID_W).astype(np.float64)
    col = np.tile(np.arange(GRID_W), rows).astype(np.float64)
    axis_dim = ATT_HEAD_DIM // 2
    freqs = ROPE_THETA ** (-np.arange(0, axis_dim, 2, dtype=np.float64) / axis_dim)
    ang = np.concatenate([row[:, None] * freqs, col[:, None] * freqs], axis=-1)
    cos = np.cos(ang).astype(np.float32)
    sin = np.sin(ang).astype(np.float32)
    return (jnp.asarray(np.repeat(cos, 2, axis=-1)),
            jnp.asarray(np.stack([-sin, sin], axis=-1).reshape(seq, ATT_HEAD_DIM)))


def _pair_rows(p):
    return p.reshape(2, SSM_PAIRS, 1, 2 * SSM_STATE).transpose(1, 0, 2, 3)


def _stacked_weights(w_in, w_branch, w_out):
    w_main = jnp.concatenate([w_in[:, :, :IN_AB], w_in[:, :, IN_AB_END:]], axis=2).astype(BF16)
    w_ab = jnp.pad(w_in[:, :, IN_AB:IN_AB_END],
                   ((0, 0), (0, 0), (0, AB_PAD - (IN_AB_END - IN_AB)))).astype(BF16)
    return w_main, w_ab, w_branch.astype(BF16), w_out.astype(BF16)


def _layer_params(layer, ssm_a_re, ssm_a_im, ssm_log_step, ssm_b_re, ssm_b_im, ssm_c_re,
                  ssm_c_im, ssm_d, ssm_w_glu, ssm_b_glu, dn_conv, dn_a_log, dn_dt_bias,
                  dn_norm_g, attn_q_norm, attn_k_norm):
    ls_n = jnp.broadcast_to(ssm_log_step[layer][:, :, None], ssm_a_re[layer].shape)
    flat = lambda p: p.reshape(2, 1, SSM_GROUPS * SSM_STATE)

    def b_pairs(b):
        bt = b.transpose(0, 1, 3, 2).reshape(2, SSM_PAIRS, 2, SSM_GROUP, SSM_STATE)
        return bt.transpose(1, 0, 3, 2, 4).reshape(SSM_PAIRS, 2, SSM_GROUP, 2 * SSM_STATE)

    def c_pairs(c):
        ct = c.transpose(0, 1, 3, 2).reshape(2, SSM_PAIRS, 2 * SSM_STATE, SSM_GROUP)
        return ct.transpose(1, 0, 2, 3)

    conv = dn_conv[layer].T.reshape(DN_CONV, 3, DN_WIDTH).transpose(1, 0, 2)
    conv = jnp.pad(conv, ((0, 0), (0, 8 - DN_CONV), (0, 0)))

    def lane_vec(p):
        return jnp.pad(p.reshape(1, -1), ((0, 0), (0, AB_PAD - p.size)))

    return {
        "ssm_rows": [_pair_rows(ssm_a_re[layer]), _pair_rows(ssm_a_im[layer]), _pair_rows(ls_n)],
        "ssm_b": [b_pairs(ssm_b_re[layer]), b_pairs(ssm_b_im[layer])],
        "ssm_c": [c_pairs(ssm_c_re[layer]), c_pairs(ssm_c_im[layer])],
        "ssm_flat": [flat(ssm_a_re[layer]), flat(ssm_a_im[layer]), flat(ls_n)],
        "ssm_d": ssm_d[layer].reshape(1, SSM_WIDTH),
        "ssm_w_glu": ssm_w_glu[layer],
        "ssm_b_glu": ssm_b_glu[layer].reshape(1, SSM_WIDTH),
        "dn_conv": conv,
        "dn_alog": lane_vec(dn_a_log[layer]),
        "dn_dtb": lane_vec(dn_dt_bias[layer]),
        "dn_norm_g": dn_norm_g[layer].reshape(1, DN_HEAD_DIM),
        "att_gq": attn_q_norm[layer].reshape(1, ATT_HEAD_DIM),
        "att_gk": attn_k_norm[layer].reshape(1, ATT_HEAD_DIM),
    }


def kernel(x, mem, norm_g, w_in, ssm_a_re, ssm_a_im, ssm_log_step, ssm_b_re, ssm_b_im, ssm_c_re, ssm_c_im, ssm_d, ssm_w_glu, ssm_b_glu, dn_conv, dn_a_log, dn_dt_bias, dn_norm_g, attn_q_norm, attn_k_norm, mem_norm_g, w_mem_kv, w_branch, w_out, final_norm_g):
    bsz, seq, _ = x.shape
    depth = w_in.shape[0]
    cos2, sin2 = _rope_tables(seq)
    final_g = final_norm_g.reshape(1, D_MODEL)
    w_main, w_ab, w_br, w_o = _stacked_weights(w_in, w_branch, w_out)
    outs = []
    for b in range(bsz):
        xb = x[b]
        for layer in range(depth):
            lp = _layer_params(layer, ssm_a_re, ssm_a_im, ssm_log_step, ssm_b_re, ssm_b_im,
                               ssm_c_re, ssm_c_im, ssm_d, ssm_w_glu, ssm_b_glu, dn_conv, dn_a_log,
                               dn_dt_bias, dn_norm_g, attn_q_norm, attn_k_norm)
            h, ab = _inproj(xb, norm_g[layer].reshape(1, D_MODEL), w_main, w_ab, layer)
            y_a = _ssm_branch(h, lp)
            y_b = _dn_branch(h, ab, lp)
            qt, kr, vt = _att_prep(h, cos2, sin2, lp["att_gq"], lp["att_gk"])
            y_c = _att(qt, kr, vt, h)
            kv = _mem_kv(mem[b], mem_norm_g[layer].reshape(1, D_MODEL), w_mem_kv[layer])
            y_m = _mem_att(h, kv)
            merged = _merge((y_a, y_b, y_c, y_m), h, w_br, layer)
            xb = _outproj(merged, w_o, xb, final_g, layer, final_norm=(layer == depth - 1))
        outs.append(xb)
    return outs[0][None] if bsz == 1 else jnp.stack(outs, axis=0)
```

```python
import functools
import math

import numpy as np
import jax
import jax.numpy as jnp
from jax import lax
from jax.experimental import pallas as pl
from jax.experimental.pallas import tpu as pltpu

F32 = jnp.float32
BF16 = jnp.bfloat16

D_MODEL = 2048
GRID_W = 64
EPS = 1e-6

SSM_GROUP = 16
SSM_STATE = 64
SSM_GROUPS = 48
SSM_WIDTH = SSM_GROUPS * SSM_GROUP
SSM_T = 16
SSM_PAIRS = SSM_GROUPS // 2
SSM_CW = SSM_T * SSM_GROUP

DN_HEADS = 6
DN_HEAD_DIM = 128
DN_WIDTH = DN_HEADS * DN_HEAD_DIM
DN_CONV = 5
DN_CHUNK = 64
DN_GROUP = 256
DN_HB = 6

ATT_HEADS = 8
ATT_KV_HEADS = 2
ATT_HEAD_DIM = 128
ATT_WIDTH = ATT_HEADS * ATT_HEAD_DIM
ATT_KV_WIDTH = ATT_KV_HEADS * ATT_HEAD_DIM
ROPE_THETA = 10000.0

MEM_HEADS = 4
MEM_HEAD_DIM = 128
MEM_WIDTH = MEM_HEADS * MEM_HEAD_DIM

N_BRANCH = 4

OFF_UA = 0
OFF_ZA = 768
OFF_DQ = 1536
OFF_DK = 2304
OFF_DV = 3072
OFF_ZB = 3840
OFF_AQ = 4608
OFF_AK = 5632
OFF_AV = 5888
OFF_ZC = 6144
OFF_MQ = 7168
OFF_ZM = 7680
OFF_GATE = 8192
IN_AB = 3840
IN_AB_END = IN_AB + 4 * DN_HEADS
H_WIDTH = OFF_GATE + N_BRANCH * D_MODEL
AB_PAD = 128

VMEM_LIMIT = 56 * 1024 * 1024


def _cparams(sem):
    return pltpu.CompilerParams(dimension_semantics=sem, vmem_limit_bytes=VMEM_LIMIT)


def _silu(x):
    return x * jax.nn.sigmoid(x)


def _dot(a, b):
    return jnp.dot(a, b, preferred_element_type=F32)


def _dot_nt(a, b):
    return lax.dot_general(a, b, (((1,), (1,)), ((), ())), preferred_element_type=F32)


def _dot_tn(a, b):
    return lax.dot_general(a, b, (((0,), (0,)), ((), ())), preferred_element_type=F32)


def _split3(x):
    x1 = x.astype(BF16)
    r1 = x - x1.astype(F32)
    x2 = r1.astype(BF16)
    x3 = (r1 - x2.astype(F32)).astype(BF16)
    return x1, x2, x3


def _dot_exact_lhs(a_bf16, x):
    x1, x2, x3 = _split3(x)
    return _dot(a_bf16, x1) + _dot(a_bf16, x2) + _dot(a_bf16, x3)


def _dot_exact_rhs(x, b_bf16):
    x1, x2, x3 = _split3(x)
    return _dot(x1, b_bf16) + _dot(x2, b_bf16) + _dot(x3, b_bf16)


def _dot_f32(a, b):
    a1, a2, a3 = _split3(a)
    b1, b2, b3 = _split3(b)
    return (_dot(a1, b1) + (_dot(a1, b2) + _dot(a2, b1))
            + (_dot(a1, b3) + _dot(a2, b2) + _dot(a3, b1)))


def _inproj_kernel(x_ref, g_ref, w_ref, wab_ref, h_ref, ab_ref, xn_ref):
    @pl.when(pl.program_id(1) == 0)
    def _():
        x = x_ref[...]
        ms = jnp.mean(x * x, axis=-1, keepdims=True)
        xn_ref[...] = (x * lax.rsqrt(ms + EPS) * g_ref[...]).astype(BF16)
        ab_ref[...] = _dot(xn_ref[...], wab_ref[...])

    h_ref[...] = _dot(xn_ref[...], w_ref[...]).astype(BF16)


def _inproj(x, g, w_main, w_ab, layer):
    seq = x.shape[0]
    tm = min(1024, seq)
    tn = 2048
    return pl.pallas_call(
        _inproj_kernel,
        grid=(seq // tm, H_WIDTH // tn),
        in_specs=[
            pl.BlockSpec((tm, D_MODEL), lambda i, j: (i, 0)),
            pl.BlockSpec((1, D_MODEL), lambda i, j: (0, 0)),
            pl.BlockSpec((None, D_MODEL, tn), lambda i, j: (layer, 0, j)),
            pl.BlockSpec((None, D_MODEL, AB_PAD), lambda i, j: (layer, 0, 0)),
        ],
        out_specs=[
            pl.BlockSpec((tm, tn), lambda i, j: (i, j)),
            pl.BlockSpec((tm, AB_PAD), lambda i, j: (i, 0)),
        ],
        out_shape=[
            jax.ShapeDtypeStruct((seq, H_WIDTH), BF16),
            jax.ShapeDtypeStruct((seq, AB_PAD), F32),
        ],
        scratch_shapes=[pltpu.VMEM((tm, D_MODEL), BF16)],
        compiler_params=_cparams(("parallel", "arbitrary")),
        name="inproj",
    )(x, g, w_main, w_ab)


def _cpow_table(base_re, base_im, n):
    re = [jnp.ones_like(base_re)]
    im = [jnp.zeros_like(base_im)]
    for _ in range(n):
        re_n = re[-1] * base_re - im[-1] * base_im
        im_n = re[-1] * base_im + im[-1] * base_re
        re.append(re_n)
        im.append(im_n)
    return re, im


def _ssm_prep_kernel(are_r, aim_r, ls_r, bre_ref, bim_ref, cre_ref, cim_ref,
                     win_ref, wout_ref, m_ref):
    t = SSM_T
    cw = SSM_CW
    lane128 = lax.broadcasted_iota(jnp.int32, (1, 128), 1)
    col_t = lax.broadcasted_iota(jnp.int32, (1, cw), 1) // SSM_GROUP
    sub128 = lax.broadcasted_iota(jnp.int32, (128, 1), 0)
    lane_cw = lax.broadcasted_iota(jnp.int32, (SSM_GROUP, cw), 1)
    tile_p = jnp.where(lane_cw % SSM_GROUP == lax.broadcasted_iota(jnp.int32, (SSM_GROUP, cw), 0),
                       1.0, 0.0).astype(BF16)

    m_ref[...] = jnp.zeros(m_ref.shape, m_ref.dtype)

    dirs = ({}, {})
    for d, st in enumerate(dirs):
        step = jnp.exp(ls_r[d])
        a_re = are_r[d]
        a_im = aim_r[d]
        mag = jnp.exp(a_re * step)
        lam_re = mag * jnp.cos(a_im * step)
        lam_im = mag * jnp.sin(a_im * step)
        den = a_re * a_re + a_im * a_im
        nr = lam_re - 1.0
        ni = lam_im
        coef_re = (nr * a_re + ni * a_im) / den
        coef_im = (ni * a_re - nr * a_im) / den
        b_re = bre_ref[d]
        b_im = bim_ref[d]
        st["bbt_re"] = coef_re * b_re - coef_im * b_im
        st["bbt_im"] = coef_re * b_im + coef_im * b_re
        st["pr"] = _cpow_table(lam_re, lam_im, t)
    for d, st in enumerate(dirs):
        pr_re, pr_im = st["pr"]
        bb_re = jnp.concatenate([st["bbt_re"]] * t, axis=0)
        bb_im = jnp.concatenate([st["bbt_im"]] * t, axis=0)
        e_in = [(t - 1 - s) if d == 0 else s for s in range(t)]
        p_re = jnp.concatenate([jnp.broadcast_to(pr_re[e], (SSM_GROUP, 128)) for e in e_in], axis=0)
        p_im = jnp.concatenate([jnp.broadcast_to(pr_im[e], (SSM_GROUP, 128)) for e in e_in], axis=0)
        w_re = p_re * bb_re - p_im * bb_im
        w_im = p_re * bb_im + p_im * bb_re
        for par in range(2):
            keep = (lane128 < 64) if par == 0 else (lane128 >= 64)
            win_ref[par * cw:(par + 1) * cw, (2 * d) * 128:(2 * d + 1) * 128] = (
                jnp.where(keep, w_re, 0.0).astype(BF16))
            win_ref[par * cw:(par + 1) * cw, (2 * d + 1) * 128:(2 * d + 2) * 128] = (
                jnp.where(keep, w_im, 0.0).astype(BF16))
    for d, st in enumerate(dirs):
        pr_re, pr_im = st["pr"]
        tab_re = jnp.zeros((128, 128), F32)
        tab_im = jnp.zeros((128, 128), F32)
        for e in range(t + 1):
            tab_re = jnp.where(sub128 == e, pr_re[e], tab_re)
            tab_im = jnp.where(sub128 == e, pr_im[e], tab_im)
        st["tab_re"] = tab_re.T
        st["tab_im"] = tab_im.T
    for d, st in enumerate(dirs):
        st["c_re"] = _dot_exact_rhs(cre_ref[d], tile_p)
        st["c_im"] = _dot_exact_rhs(cim_ref[d], tile_p)

    def c_lam(st, expo_row):
        sel = jnp.where(sub128 == expo_row, 1.0, 0.0).astype(BF16)
        q_re = _dot_exact_rhs(st["tab_re"], sel)
        q_im = _dot_exact_rhs(st["tab_im"], sel)
        return (st["c_re"] * q_re - st["c_im"] * q_im, -(st["c_re"] * q_im + st["c_im"] * q_re))

    for d, st in enumerate(dirs):
        st["o"] = c_lam(st, (col_t + 1) if d == 0 else (t - col_t))
    for d, st in enumerate(dirs):
        st["r"] = c_lam(st, col_t if d == 0 else (t - 1 - col_t))
    for d, st in enumerate(dirs):
        o_re, o_im = st["o"]
        for par in range(2):
            keep = (sub128 < 64) if par == 0 else (sub128 >= 64)
            wout_ref[(2 * d) * 128:(2 * d + 1) * 128, par * cw:(par + 1) * cw] = (
                jnp.where(keep, o_re, 0.0).astype(BF16))
            wout_ref[(2 * d + 1) * 128:(2 * d + 2) * 128, par * cw:(par + 1) * cw] = (
                jnp.where(keep, o_im, 0.0).astype(BF16))
    krows = [[None, None], [None, None]]
    for d, st in enumerate(dirs):
        r_re, r_im = st["r"]
        for par in range(2):
            keep = (lane128 < 64) if par == 0 else (lane128 >= 64)
            krows[d][par] = (_dot_f32(jnp.where(keep, st["bbt_re"], 0.0), r_re)
                             + _dot_f32(jnp.where(keep, st["bbt_im"], 0.0), r_im))

    for par in range(2):
        kf = krows[0][par]
        kb = krows[1][par]
        for s in range(t):
            sh_f = SSM_GROUP * s
            blk = jnp.where(lane_cw >= sh_f, pltpu.roll(kf, sh_f, axis=1) if sh_f else kf, 0.0)
            sh_b = SSM_GROUP * (t - 1 - s)
            rolled_b = pltpu.roll(kb, cw - sh_b, axis=1) if sh_b else kb
            blk = blk + jnp.where(lane_cw < cw - sh_b, rolled_b, 0.0)
            m_ref[par * cw + s * SSM_GROUP:par * cw + (s + 1) * SSM_GROUP,
                  par * cw:(par + 1) * cw] = blk.astype(BF16)


def _ssm_prep(rows, b_t, c_t):
    pw = 2 * SSM_CW
    row_spec = pl.BlockSpec((None, 2, 1, 128), lambda g: (g, 0, 0, 0))
    b_spec = pl.BlockSpec((None, 2, SSM_GROUP, 128), lambda g: (g, 0, 0, 0))
    c_spec = pl.BlockSpec((None, 2, 128, SSM_GROUP), lambda g: (g, 0, 0, 0))
    w_spec = pl.BlockSpec((None, pw, pw), lambda g: (g, 0, 0))
    return pl.pallas_call(
        _ssm_prep_kernel,
        grid=(SSM_PAIRS,),
        in_specs=[row_spec] * 3 + [b_spec] * 2 + [c_spec] * 2,
        out_specs=[w_spec] * 3,
        out_shape=[jax.ShapeDtypeStruct((SSM_PAIRS, pw, pw), BF16)] * 3,
        compiler_params=_cparams(("parallel",)),
        name="ssm_prep",
    )(*rows, *b_t, *c_t)


SSM_BG = 8
SSM_BP = SSM_BG // 2
SSM_BW = SSM_BG * SSM_CW


def _ssm_place(t0):
    r = lax.broadcasted_iota(jnp.int32, (256, SSM_BW), 0)
    c = lax.broadcasted_iota(jnp.int32, (256, SSM_BW), 1)
    j = r % 128
    target = (j // SSM_GROUP) * SSM_CW + (t0 + r // 128) * SSM_GROUP + j % SSM_GROUP
    return jnp.where(c == target, 1.0, 0.0).astype(BF16)


def _ssm_in_kernel(u_ref, win_ref, u2_ref, h_ref, uf_ref):
    nchunk = u2_ref.shape[0]
    uf_ref[...] = u_ref[...].astype(F32)
    acc = jnp.zeros((nchunk, SSM_BW), F32)
    for t0 in range(0, SSM_T, 2):
        lhs = jnp.concatenate([uf_ref[pl.ds(t0, nchunk, stride=SSM_T), :],
                               uf_ref[pl.ds(t0 + 1, nchunk, stride=SSM_T), :]], axis=1)
        acc = acc + _dot(lhs.astype(BF16), _ssm_place(t0))
    u2_ref[...] = acc.astype(BF16)
    pw = 2 * SSM_CW
    for j in range(SSM_BP):
        h = _dot(u2_ref[:, j * pw:(j + 1) * pw], win_ref[j])
        for k in range(4):
            h_ref[k, :, j * 128:(j + 1) * 128] = h[:, k * 128:(k + 1) * 128]


def _ssm_in(h, win):
    seq = h.shape[0]
    nchunk = seq // SSM_T
    pw = 2 * SSM_CW
    nblk = SSM_GROUPS // SSM_BG
    return pl.pallas_call(
        _ssm_in_kernel,
        grid=(nblk,),
        in_specs=[
            pl.BlockSpec((seq, 128), lambda b: (0, OFF_UA // 128 + b)),
            pl.BlockSpec((SSM_BP, pw, pw), lambda b: (b, 0, 0)),
        ],
        out_specs=[
            pl.BlockSpec((nchunk, SSM_BW), lambda b: (0, b)),
            pl.BlockSpec((4, nchunk, SSM_BP * 128), lambda b: (0, 0, b)),
        ],
        out_shape=[
            jax.ShapeDtypeStruct((nchunk, SSM_GROUPS * SSM_CW), BF16),
            jax.ShapeDtypeStruct((4, nchunk, SSM_PAIRS * 128), F32),
        ],
        scratch_shapes=[pltpu.VMEM((seq, 128), F32)],
        compiler_params=_cparams(("parallel",)),
        name="ssm_in",
    )(h, win)


def _ssm_scan_kernel(are_ref, aim_ref, ls_ref, h_ref, p_ref, *, nchunk):
    width = h_ref.shape[2]

    def lam_pow_t(d):
        step = jnp.exp(ls_ref[d]) * float(SSM_T)
        mag = jnp.exp(are_ref[d] * step)
        ang = aim_ref[d] * step
        return mag * jnp.cos(ang), mag * jnp.sin(ang)

    lfr, lfi = lam_pow_t(0)
    lbr, lbi = lam_pow_t(1)

    def body(c, carry):
        fr, fi, br, bi = carry
        cb = nchunk - 1 - c
        p_ref[0, pl.ds(c, 1), :] = fr
        p_ref[1, pl.ds(c, 1), :] = fi
        p_ref[2, pl.ds(cb, 1), :] = br
        p_ref[3, pl.ds(cb, 1), :] = bi
        hfr = h_ref[0, pl.ds(c, 1), :]
        hfi = h_ref[1, pl.ds(c, 1), :]
        hbr = h_ref[2, pl.ds(cb, 1), :]
        hbi = h_ref[3, pl.ds(cb, 1), :]
        nfr = lfr * fr - lfi * fi + hfr
        nfi = lfr * fi + lfi * fr + hfi
        nbr = lbr * br - lbi * bi + hbr
        nbi = lbr * bi + lbi * br + hbi
        return nfr, nfi, nbr, nbi

    z = jnp.zeros((1, width), F32)
    lax.fori_loop(0, nchunk, body, (z, z, z, z))


def _ssm_scan(flat_params, h):
    nchunk = h.shape[1]
    width = h.shape[2]
    wt = 768
    return pl.pallas_call(
        functools.partial(_ssm_scan_kernel, nchunk=nchunk),
        grid=(width // wt,),
        in_specs=[pl.BlockSpec((2, 1, wt), lambda j: (0, 0, j))] * 3 + [
            pl.BlockSpec((4, nchunk, wt), lambda j: (0, 0, j)),
        ],
        out_specs=pl.BlockSpec((4, nchunk, wt), lambda j: (0, 0, j)),
        out_shape=jax.ShapeDtypeStruct(h.shape, F32),
        compiler_params=_cparams(("parallel",)),
        name="ssm_scan",
    )(*flat_params, h)


def _ssm_out_kernel(u2_ref, p_ref, m_ref, wout_ref, y_ref):
    nchunk = u2_ref.shape[0]
    pw = 2 * SSM_CW
    parts = []
    for j in range(SSM_BP):
        acc = _dot(u2_ref[:, j * pw:(j + 1) * pw], m_ref[j])
        for k in range(4):
            acc = acc + _dot(p_ref[k, :, j * 128:(j + 1) * 128].astype(BF16),
                             wout_ref[j, k * 128:(k + 1) * 128, :])
        parts.append(acc.astype(BF16))
    y16 = jnp.concatenate(parts, axis=1)
    for t0 in range(0, SSM_T, 2):
        yt = _dot_nt(y16, _ssm_place(t0))
        y_ref[pl.ds(t0, nchunk, stride=SSM_T), :] = yt[:, :128]
        y_ref[pl.ds(t0 + 1, nchunk, stride=SSM_T), :] = yt[:, 128:]


def _ssm_out(u2, p, m, wout):
    nchunk = u2.shape[0]
    seq = nchunk * SSM_T
    pw = 2 * SSM_CW
    nblk = SSM_GROUPS // SSM_BG
    return pl.pallas_call(
        _ssm_out_kernel,
        grid=(nblk,),
        in_specs=[
            pl.BlockSpec((nchunk, SSM_BW), lambda b: (0, b)),
            pl.BlockSpec((4, nchunk, SSM_BP * 128), lambda b: (0, 0, b)),
            pl.BlockSpec((SSM_BP, pw, pw), lambda b: (b, 0, 0)),
            pl.BlockSpec((SSM_BP, pw, pw), lambda b: (b, 0, 0)),
        ],
        out_specs=pl.BlockSpec((seq, 128), lambda b: (0, b)),
        out_shape=jax.ShapeDtypeStruct((seq, SSM_WIDTH), F32),
        compiler_params=_cparams(("parallel",)),
        name="ssm_out",
    )(u2, p, m, wout)


def _ssm_epi_kernel(y_ref, u_ref, z_ref, d_ref, wg_ref, bg_ref, o_ref):
    y = y_ref[...] + d_ref[...] * u_ref[...].astype(F32)
    y = jax.nn.gelu(y)
    glu = _dot(y.astype(BF16), wg_ref[...].astype(BF16)) + bg_ref[...]
    y = y * jax.nn.sigmoid(glu)
    o_ref[...] = (y * _silu(z_ref[...].astype(F32))).astype(BF16)


def _ssm_epi(y, h, d, w_glu, b_glu):
    seq = y.shape[0]
    tm = min(1024, seq)
    w = SSM_WIDTH
    return pl.pallas_call(
        _ssm_epi_kernel,
        grid=(seq // tm,),
        in_specs=[
            pl.BlockSpec((tm, w), lambda i: (i, 0)),
            pl.BlockSpec((tm, w), lambda i: (i, OFF_UA // w)),
            pl.BlockSpec((tm, w), lambda i: (i, OFF_ZA // w)),
            pl.BlockSpec((1, w), lambda i: (0, 0)),
            pl.BlockSpec((w, w), lambda i: (0, 0)),
            pl.BlockSpec((1, w), lambda i: (0, 0)),
        ],
        out_specs=pl.BlockSpec((tm, w), lambda i: (i, 0)),
        out_shape=jax.ShapeDtypeStruct((seq, w), BF16),
        compiler_params=_cparams(("parallel",)),
        name="ssm_epi",
    )(y, h, h, d, w_glu, b_glu)


def _ssm_branch(h, lp):
    win, wout, m = _ssm_prep(lp["ssm_rows"], lp["ssm_b"], lp["ssm_c"])
    u2, hs = _ssm_in(h, win)
    p = _ssm_scan(lp["ssm_flat"], hs)
    y = _ssm_out(u2, p, m, wout)
    return _ssm_epi(y, h, lp["ssm_d"], lp["ssm_w_glu"], lp["ssm_b_glu"])


def _dn_prep_kernel(qc, qp, qn, kc, kp, kn, vc, vp, vn, ab_ref, cw_ref, alog_ref, dtb_ref,
                    qo_ref, ko_ref, vo_ref, sc_ref, sct_ref, ext_ref, *, tm, nblk):
    i = pl.program_id(0)
    halo = 16
    pad = DN_CONV // 2

    def conv_silu(cur, prev, nxt, part):
        ext_ref[0:halo, :] = jnp.where(i > 0, prev[...].astype(F32), 0.0)
        ext_ref[halo:halo + tm, :] = cur[...].astype(F32)
        ext_ref[halo + tm:halo + tm + halo, :] = jnp.where(i < nblk - 1, nxt[...].astype(F32), 0.0)
        acc = jnp.zeros((tm, DN_WIDTH), F32)
        for j in range(DN_CONV):
            acc = acc + cw_ref[part, j:j + 1, :] * ext_ref[pl.ds(halo - pad + j, tm), :]
        return _silu(acc)

    def l2n(x):
        outs = []
        for hd in range(DN_HEADS):
            xh = x[:, hd * DN_HEAD_DIM:(hd + 1) * DN_HEAD_DIM]
            outs.append(xh * lax.rsqrt(jnp.sum(xh * xh, axis=-1, keepdims=True) + EPS))
        return jnp.concatenate(outs, axis=1)

    qo_ref[...] = (l2n(conv_silu(qc, qp, qn, 0)) * (DN_HEAD_DIM ** -0.5)).astype(BF16)
    ko_ref[...] = l2n(conv_silu(kc, kp, kn, 1)).astype(BF16)
    vo_ref[...] = conv_silu(vc, vp, vn, 2).astype(BF16)

    ab = ab_ref[...]
    g_all = -jnp.exp(alog_ref[...]) * jax.nn.softplus(ab + dtb_ref[...])
    beta_all = jax.nn.sigmoid(ab)
    r = lax.broadcasted_iota(jnp.int32, (tm, tm), 0)
    c = lax.broadcasted_iota(jnp.int32, (tm, tm), 1)
    same = (r // DN_CHUNK) == (c // DN_CHUNK)
    tri_f = jnp.where(same & (c <= r), 1.0, 0.0).astype(BF16)
    tri_b = jnp.where(same & (c >= r), 1.0, 0.0).astype(BF16)
    blk = jnp.where(same, 1.0, 0.0).astype(BF16)
    gcf = _dot_exact_lhs(tri_f, g_all)
    gcb = _dot_exact_lhs(tri_b, g_all)
    gtot = _dot_exact_lhs(blk, g_all)
    lane = lax.broadcasted_iota(jnp.int32, (tm, AB_PAD), 1)
    sc = jnp.where(lane < 6, gcf,
                   jnp.where(lane < 12, gcb,
                             jnp.where(lane < 24, beta_all,
                                       jnp.where(lane < 36, pltpu.roll(gtot, 24, axis=1), 0.0))))
    sc_ref[...] = sc
    sct_ref[...] = sc.T


def _dn_prep(h, ab, conv_w, alog, dtb):
    seq = h.shape[0]
    tm = min(256, seq)
    nblk = seq // tm
    w = DN_WIDTH
    hb = tm // 16
    nh = seq // 16

    def cur(ci):
        return pl.BlockSpec((tm, w), lambda i: (i, ci))

    def prev(ci):
        return pl.BlockSpec((16, w), lambda i: (jnp.maximum(i * hb - 1, 0), ci))

    def nxt(ci):
        return pl.BlockSpec((16, w), lambda i: (jnp.minimum((i + 1) * hb, nh - 1), ci))

    in_specs = []
    for off in (OFF_DQ, OFF_DK, OFF_DV):
        ci = off // w
        in_specs += [cur(ci), prev(ci), nxt(ci)]
    in_specs += [
        pl.BlockSpec((tm, AB_PAD), lambda i: (i, 0)),
        pl.BlockSpec((3, 8, w), lambda i: (0, 0, 0)),
        pl.BlockSpec((1, AB_PAD), lambda i: (0, 0)),
        pl.BlockSpec((1, AB_PAD), lambda i: (0, 0)),
    ]
    return pl.pallas_call(
        functools.partial(_dn_prep_kernel, tm=tm, nblk=nblk),
        grid=(nblk,),
        in_specs=in_specs,
        out_specs=[pl.BlockSpec((tm, w), lambda i: (i, 0))] * 3
        + [pl.BlockSpec((tm, AB_PAD), lambda i: (i, 0)), pl.BlockSpec((AB_PAD, tm), lambda i: (0, i))],
        out_shape=[jax.ShapeDtypeStruct((seq, w), BF16)] * 3
        + [jax.ShapeDtypeStruct((seq, AB_PAD), F32), jax.ShapeDtypeStruct((AB_PAD, seq), F32)],
        scratch_shapes=[pltpu.VMEM((tm + 32, w), F32)],
        compiler_params=_cparams(("parallel",)),
        name="dn_prep",
    )(h, h, h, h, h, h, h, h, h, ab, conv_w, alog, dtb)


def _dn_main_kernel(qf_ref, kf_ref, vf_ref, scf_ref, sctf_ref,
                    qb_ref, kb_ref, vb_ref, scb_ref, sctb_ref, of_ref, ob_ref, s_ref):
    gsz = DN_GROUP
    nck = gsz // DN_CHUNK

    @pl.when(pl.program_id(1) == 0)
    def _():
        s_ref[...] = jnp.zeros(s_ref.shape, F32)

    refs = ((qf_ref, kf_ref, vf_ref, scf_ref, sctf_ref, of_ref),
            (qb_ref, kb_ref, vb_ref, scb_ref, sctb_ref, ob_ref))
    r = lax.broadcasted_iota(jnp.int32, (gsz, gsz), 0)
    c = lax.broadcasted_iota(jnp.int32, (gsz, gsz), 1)
    same = (r // DN_CHUNK) == (c // DN_CHUNK)
    incl = (same & (r >= c), same & (r <= c))
    strict = (same & (r > c), same & (r < c))

    chains = []
    for d in range(2):
        q_ref, k_ref, v_ref, sc_ref, sct_ref, o_ref = refs[d]
        sc = sc_ref[...]
        sct = sct_ref[...]
        lane = lax.broadcasted_iota(jnp.int32, sc.shape, 1)
        sub = lax.broadcasted_iota(jnp.int32, sct.shape, 0)
        for hl in range(DN_HB):
            cidx = d * DN_HEADS + pl.program_id(0) * DN_HB + hl
            hs = slice(hl * DN_HEAD_DIM, (hl + 1) * DN_HEAD_DIM)

            def col(ci, sc=sc, lane=lane):
                return jnp.sum(jnp.where(lane == ci, sc, 0.0), axis=1, keepdims=True)

            ch = {"d": d, "hl": hl, "hs": hs, "o_ref": o_ref}
            ch["q"] = q_ref[:, hs]
            ch["k"] = k_ref[:, hs]
            ch["v"] = v_ref[:, hs]
            ch["gc_col"] = col(cidx)
            ch["gc_row"] = jnp.sum(jnp.where(sub == cidx, sct, 0.0), axis=0, keepdims=True)
            ch["beta_col"] = col(12 + cidx)
            ch["gtot_col"] = col(24 + cidx)
            chains.append(ch)

    for ch in chains:
        ch["kf"] = ch["k"].astype(F32)
        ch["kb"] = ch["kf"] * ch["beta_col"]
        ch["gram"] = _dot_nt(ch["kb"].astype(BF16), ch["k"])
    for ch in chains:
        ch["qk"] = _dot_nt(ch["q"], ch["k"])
    for ch in chains:
        d = ch["d"]
        ch["decay"] = jnp.where(incl[d], jnp.exp(jnp.where(incl[d], ch["gc_col"] - ch["gc_row"], 0.0)), 0.0)
        ch["x"] = jnp.concatenate([ch["v"].astype(F32) * ch["beta_col"],
                                   ch["kb"] * jnp.exp(ch["gc_col"])], axis=1)
        ch["qd16"] = (ch["q"].astype(F32) * jnp.exp(ch["gc_col"])).astype(BF16)
        ch["kd16"] = (ch["kf"] * jnp.exp(ch["gtot_col"] - ch["gc_col"])).astype(BF16)
    for ch in chains:
        ch["p"] = jnp.where(strict[ch["d"]], ch["gram"] * ch["decay"], 0.0).astype(BF16)
        ch["intra"] = (ch["qk"] * ch["decay"]).astype(BF16)
    for ch in chains:
        ch["x"] = ch["x"] - _dot(ch["p"], ch["x"].astype(BF16))
    npow = 2
    while npow < DN_CHUNK:
        for ch in chains:
            ch["p"] = _dot(ch["p"], ch["p"]).astype(BF16)
        for ch in chains:
            ch["x"] = ch["x"] + _dot(ch["p"], ch["x"].astype(BF16))
        npow *= 2
    for ch in chains:
        ch["u"] = ch["x"][:, :DN_HEAD_DIM]
        ch["w16"] = ch["x"][:, DN_HEAD_DIM:].astype(BF16)
        ch["s"] = s_ref[ch["d"], ch["hl"]]
        ch["vnew"] = [None] * nck
        ch["oq"] = [None] * nck

    for step in range(nck):
        for ch in chains:
            j = step if ch["d"] == 0 else nck - 1 - step
            lo, hi = j * DN_CHUNK, (j + 1) * DN_CHUNK
            s16 = ch["s"].astype(BF16)
            ws = _dot(jnp.concatenate([ch["w16"][lo:hi], ch["qd16"][lo:hi]], axis=0), s16)
            vn = ch["u"][lo:hi] - ws[:DN_CHUNK]
            ch["oq"][j] = ws[DN_CHUNK:]
            ch["vnew"][j] = vn
            ch["s"] = (ch["s"] * jnp.exp(ch["gtot_col"][lo:lo + 1, :])
                       + _dot_tn(ch["kd16"][lo:hi], vn.astype(BF16)))
    for ch in chains:
        s_ref[ch["d"], ch["hl"]] = ch["s"]
        vn_all = jnp.concatenate(ch["vnew"], axis=0).astype(BF16)
        ch["o_ref"][:, ch["hs"]] = jnp.concatenate(ch["oq"], axis=0) + _dot(ch["intra"], vn_all)


def _dn_main(qn, kn, vc, sc, sct):
    seq = qn.shape[0]
    gsz = DN_GROUP
    ng = seq // gsz
    hw = DN_HB * DN_HEAD_DIM
    sct_rows = sct.shape[0]
    fwd = lambda h, i: (i, h)
    bwd = lambda h, i: (ng - 1 - i, h)
    in_specs = [
        pl.BlockSpec((gsz, hw), fwd), pl.BlockSpec((gsz, hw), fwd), pl.BlockSpec((gsz, hw), fwd),
        pl.BlockSpec((gsz, AB_PAD), lambda h, i: (i, 0)),
        pl.BlockSpec((sct_rows, gsz), lambda h, i: (0, i)),
        pl.BlockSpec((gsz, hw), bwd), pl.BlockSpec((gsz, hw), bwd), pl.BlockSpec((gsz, hw), bwd),
        pl.BlockSpec((gsz, AB_PAD), lambda h, i: (ng - 1 - i, 0)),
        pl.BlockSpec((sct_rows, gsz), lambda h, i: (0, ng - 1 - i)),
    ]
    return pl.pallas_call(
        _dn_main_kernel,
        grid=(DN_HEADS // DN_HB, ng),
        in_specs=in_specs,
        out_specs=[pl.BlockSpec((gsz, hw), fwd), pl.BlockSpec((gsz, hw), bwd)],
        out_shape=[jax.ShapeDtypeStruct((seq, DN_WIDTH), F32)] * 2,
        scratch_shapes=[pltpu.VMEM((2, DN_HB, DN_HEAD_DIM, DN_HEAD_DIM), F32)],
        compiler_params=_cparams(("parallel", "arbitrary")),
        name="dn_main",
    )(qn, kn, vc, sc, sct, qn, kn, vc, sc, sct)


def _dn_epi_kernel(of_ref, ob_ref, z_ref, g_ref, o_ref):
    o = of_ref[...] + ob_ref[...]
    g = g_ref[...]
    outs = []
    for hd in range(DN_HEADS):
        oh = o[:, hd * DN_HEAD_DIM:(hd + 1) * DN_HEAD_DIM]
        ms = jnp.mean(oh * oh, axis=-1, keepdims=True)
        outs.append(oh * lax.rsqrt(ms + EPS) * g)
    y = jnp.concatenate(outs, axis=1)
    o_ref[...] = (y * _silu(z_ref[...].astype(F32))).astype(BF16)


def _dn_epi(o_f, o_b, h, norm_g):
    seq = o_f.shape[0]
    tm = min(1024, seq)
    w = DN_WIDTH
    return pl.pallas_call(
        _dn_epi_kernel,
        grid=(seq // tm,),
        in_specs=[
            pl.BlockSpec((tm, w), lambda i: (i, 0)),
            pl.BlockSpec((tm, w), lambda i: (i, 0)),
            pl.BlockSpec((tm, w), lambda i: (i, OFF_ZB // w)),
            pl.BlockSpec((1, DN_HEAD_DIM), lambda i: (0, 0)),
        ],
        out_specs=pl.BlockSpec((tm, w), lambda i: (i, 0)),
        out_shape=jax.ShapeDtypeStruct((seq, w), BF16),
        compiler_params=_cparams(("parallel",)),
        name="dn_epi",
    )(o_f, o_b, h, norm_g)


def _dn_branch(h, ab, lp):
    qn, kn, vc, sc, sct = _dn_prep(h, ab, lp["dn_conv"], lp["dn_alog"], lp["dn_dtb"])
    o_f, o_b = _dn_main(qn, kn, vc, sc, sct)
    return _dn_epi(o_f, o_b, h, lp["dn_norm_g"])


def _att_prep_kernel(qlo_ref, qhi_ref, k_ref, v_ref, cos_ref, sin_ref, gq_ref, gk_ref,
                     qt_ref, ko_ref, vt_ref):
    cos2 = cos_ref[...]
    sin2 = sin_ref[...]
    rr = lax.broadcasted_iota(jnp.int32, (ATT_HEAD_DIM, ATT_HEAD_DIM), 0)
    cc = lax.broadcasted_iota(jnp.int32, (ATT_HEAD_DIM, ATT_HEAD_DIM), 1)
    eye = jnp.where(rr == cc, 1.0, 0.0).astype(BF16)
    swap = jnp.where(rr == (cc ^ 1), 1.0, 0.0).astype(BF16)

    gq = gq_ref[...]
    gk = gk_ref[...]
    scale = ATT_HEAD_DIM ** -0.5 * math.log2(math.e)

    def transposed(x16):
        return _dot_nt(eye, x16).astype(BF16)

    half = ATT_HEADS // 2
    xs = []
    for hd in range(ATT_HEADS):
        src = qlo_ref if hd < half else qhi_ref
        xs.append((src[:, (hd % half) * ATT_HEAD_DIM:(hd % half + 1) * ATT_HEAD_DIM], gq))
    for hd in range(ATT_KV_HEADS):
        xs.append((k_ref[:, hd * ATT_HEAD_DIM:(hd + 1) * ATT_HEAD_DIM], gk))
    xn = []
    for x16, g in xs:
        x = x16.astype(F32)
        ms = jnp.mean(x * x, axis=-1, keepdims=True)
        xn.append(x * lax.rsqrt(ms + EPS) * g)
    his = [v.astype(BF16) for v in xn]
    los = [(v - hi.astype(F32)).astype(BF16) for v, hi in zip(xn, his)]
    partner = [_dot(hi, swap) + _dot(lo, swap) for hi, lo in zip(his, los)]
    roped = [v * cos2 + pt * sin2 for v, pt in zip(xn, partner)]
    qts = [transposed((roped[hd] * scale).astype(BF16)) for hd in range(ATT_HEADS)]
    vts = [transposed(v_ref[:, hd * ATT_HEAD_DIM:(hd + 1) * ATT_HEAD_DIM]) for hd in range(ATT_KV_HEADS)]
    for hd in range(ATT_HEADS):
        qt_ref[hd * ATT_HEAD_DIM:(hd + 1) * ATT_HEAD_DIM, :] = qts[hd]
    for hd in range(ATT_KV_HEADS):
        sl = slice(hd * ATT_HEAD_DIM, (hd + 1) * ATT_HEAD_DIM)
        ko_ref[:, sl] = roped[ATT_HEADS + hd].astype(BF16)
        vt_ref[sl, :] = vts[hd]


def _att_prep(h, cos2, sin2, gq, gk):
    seq = h.shape[0]
    tm = min(1024, seq)
    return pl.pallas_call(
        _att_prep_kernel,
        grid=(seq // tm,),
        in_specs=[
            pl.BlockSpec((tm, ATT_WIDTH // 2), lambda i: (i, OFF_AQ // (ATT_WIDTH // 2))),
            pl.BlockSpec((tm, ATT_WIDTH // 2), lambda i: (i, OFF_AQ // (ATT_WIDTH // 2) + 1)),
            pl.BlockSpec((tm, ATT_KV_WIDTH), lambda i: (i, OFF_AK // ATT_KV_WIDTH)),
            pl.BlockSpec((tm, ATT_KV_WIDTH), lambda i: (i, OFF_AV // ATT_KV_WIDTH)),
            pl.BlockSpec((tm, ATT_HEAD_DIM), lambda i: (i, 0)),
            pl.BlockSpec((tm, ATT_HEAD_DIM), lambda i: (i, 0)),
            pl.BlockSpec((1, ATT_HEAD_DIM), lambda i: (0, 0)),
            pl.BlockSpec((1, ATT_HEAD_DIM), lambda i: (0, 0)),
        ],
        out_specs=[
            pl.BlockSpec((ATT_WIDTH, tm), lambda i: (0, i)),
            pl.BlockSpec((tm, ATT_KV_WIDTH), lambda i: (i, 0)),
            pl.BlockSpec((ATT_KV_WIDTH, tm), lambda i: (0, i)),
        ],
        out_shape=[
            jax.ShapeDtypeStruct((ATT_WIDTH, seq), BF16),
            jax.ShapeDtypeStruct((seq, ATT_KV_WIDTH), BF16),
            jax.ShapeDtypeStruct((ATT_KV_WIDTH, seq), BF16),
        ],
        compiler_params=_cparams(("parallel",)),
        name="att_prep",
    )(h, h, h, h, cos2, sin2, gq, gk)


def _att_kernel(qt_ref, k_ref, vt_ref, z_ref, o_ref, m_ref, l_ref, acc_ref, *, kc, nkc):
    grp = qt_ref.shape[0] // ATT_HEAD_DIM
    m_ref[...] = jnp.full(m_ref.shape, -1e30, F32)
    l_ref[...] = jnp.zeros(l_ref.shape, F32)
    acc_ref[...] = jnp.zeros(acc_ref.shape, F32)

    def chunk(c, carry):
        k0 = pl.multiple_of(c * kc, kc)
        k_c = k_ref[pl.ds(k0, kc), :]
        vt_c = vt_ref[:, pl.ds(k0, kc)]
        s = [_dot(k_c, qt_ref[hh * ATT_HEAD_DIM:(hh + 1) * ATT_HEAD_DIM, :])
             for hh in range(grp)]
        p = [None] * grp
        alpha = [None] * grp
        for hh in range(grp):
            m_old = m_ref[hh]
            m_new = jnp.maximum(m_old, jnp.max(s[hh], axis=0, keepdims=True))
            alpha[hh] = jnp.exp2(m_old - m_new)
            ph = jnp.exp2(s[hh] - m_new)
            l_ref[hh] = alpha[hh] * l_ref[hh] + jnp.sum(ph, axis=0, keepdims=True)
            m_ref[hh] = m_new
            p[hh] = ph.astype(BF16)
        for hh in range(grp):
            acc_ref[hh] = alpha[hh] * acc_ref[hh] + _dot(vt_c, p[hh])
        return carry

    lax.fori_loop(0, nkc, chunk, 0)
    for hh in range(grp):
        sl = slice(hh * ATT_HEAD_DIM, (hh + 1) * ATT_HEAD_DIM)
        o = (acc_ref[hh] / l_ref[hh]).T
        o_ref[:, sl] = (o * _silu(z_ref[:, sl].astype(F32))).astype(BF16)


ATT_HPS = 4


def _att(qt, kr, vt, h):
    seq = kr.shape[0]
    tq = min(256, seq)
    kc = min(4096, seq)
    hw = ATT_HPS * ATT_HEAD_DIM
    steps_per_kv = ATT_HEADS // ATT_KV_HEADS // ATT_HPS
    return pl.pallas_call(
        functools.partial(_att_kernel, kc=kc, nkc=seq // kc),
        grid=(ATT_HEADS // ATT_HPS, seq // tq),
        in_specs=[
            pl.BlockSpec((hw, tq), lambda a, i: (a, i)),
            pl.BlockSpec((seq, ATT_HEAD_DIM), lambda a, i: (0, a // steps_per_kv)),
            pl.BlockSpec((ATT_HEAD_DIM, seq), lambda a, i: (a // steps_per_kv, 0)),
            pl.BlockSpec((tq, hw), lambda a, i: (i, OFF_ZC // hw + a)),
        ],
        out_specs=pl.BlockSpec((tq, hw), lambda a, i: (i, a)),
        out_shape=jax.ShapeDtypeStruct((seq, ATT_WIDTH), BF16),
        scratch_shapes=[
            pltpu.VMEM((ATT_HPS, 1, tq), F32),
            pltpu.VMEM((ATT_HPS, 1, tq), F32),
            pltpu.VMEM((ATT_HPS, ATT_HEAD_DIM, tq), F32),
        ],
        compiler_params=_cparams(("parallel", "parallel")),
        name="grid_att",
    )(qt, kr, vt, h)


def _mem_kv_kernel(mem_ref, g_ref, w_ref, o_ref):
    x = mem_ref[...]
    ms = jnp.mean(x * x, axis=-1, keepdims=True)
    xn = (x * lax.rsqrt(ms + EPS) * g_ref[...]).astype(BF16)
    o_ref[...] = _dot(xn, w_ref[...].astype(BF16)).astype(BF16)


def _mem_kv(mem, g, w_kv):
    n_mem = mem.shape[0]
    tn = 512
    return pl.pallas_call(
        _mem_kv_kernel,
        grid=(2 * MEM_WIDTH // tn,),
        in_specs=[
            pl.BlockSpec((n_mem, D_MODEL), lambda j: (0, 0)),
            pl.BlockSpec((1, D_MODEL), lambda j: (0, 0)),
            pl.BlockSpec((D_MODEL, tn), lambda j: (0, j)),
        ],
        out_specs=pl.BlockSpec((n_mem, tn), lambda j: (0, j)),
        out_shape=jax.ShapeDtypeStruct((n_mem, 2 * MEM_WIDTH), BF16),
        compiler_params=_cparams(("parallel",)),
        name="mem_kv",
    )(mem, g, w_kv)


def _mem_att_kernel(q_ref, z_ref, kv_ref, o_ref):
    scale = MEM_HEAD_DIM ** -0.5
    sls = [slice(hd * MEM_HEAD_DIM, (hd + 1) * MEM_HEAD_DIM) for hd in range(MEM_HEADS)]
    s = [_dot_nt(q_ref[:, sl], kv_ref[:, sl]) * scale for sl in sls]
    p = []
    l = []
    for hd in range(MEM_HEADS):
        m = jnp.max(s[hd], axis=-1, keepdims=True)
        ph = jnp.exp(s[hd] - m)
        l.append(jnp.sum(ph, axis=-1, keepdims=True))
        p.append(ph.astype(BF16))
    o = [_dot(p[hd], kv_ref[:, MEM_WIDTH + hd * MEM_HEAD_DIM:MEM_WIDTH + (hd + 1) * MEM_HEAD_DIM]) / l[hd]
         for hd in range(MEM_HEADS)]
    for hd, sl in enumerate(sls):
        o_ref[:, sl] = (o[hd] * _silu(z_ref[:, sl].astype(F32))).astype(BF16)


def _mem_att(h, kv):
    seq = h.shape[0]
    tm = min(1024, seq)
    n_mem = kv.shape[0]
    return pl.pallas_call(
        _mem_att_kernel,
        grid=(seq // tm,),
        in_specs=[
            pl.BlockSpec((tm, MEM_WIDTH), lambda i: (i, OFF_MQ // MEM_WIDTH)),
            pl.BlockSpec((tm, MEM_WIDTH), lambda i: (i, OFF_ZM // MEM_WIDTH)),
            pl.BlockSpec((n_mem, 2 * MEM_WIDTH), lambda i: (0, 0)),
        ],
        out_specs=pl.BlockSpec((tm, MEM_WIDTH), lambda i: (i, 0)),
        out_shape=jax.ShapeDtypeStruct((seq, MEM_WIDTH), BF16),
        compiler_params=_cparams(("parallel",)),
        name="mem_att",
    )(h, h, kv)


def _merge_kernel(ya_ref, yb_ref, yc_ref, ym_ref, ga_ref, gb_ref, gc_ref, gm_ref,
                  wa_ref, wb_ref, wc0_ref, wc1_ref, wm_ref, o_ref):
    half = ATT_WIDTH // 2
    ya_w = _dot(ya_ref[...], wa_ref[...])
    yb_w = _dot(yb_ref[...], wb_ref[...])
    yc_w = _dot(yc_ref[:, :half], wc0_ref[...]) + _dot(yc_ref[:, half:], wc1_ref[...])
    ym_w = _dot(ym_ref[...], wm_ref[...])
    merged = jax.nn.sigmoid(ga_ref[...].astype(F32)) * ya_w
    merged = merged + jax.nn.sigmoid(gb_ref[...].astype(F32)) * yb_w
    merged = merged + jax.nn.sigmoid(gc_ref[...].astype(F32)) * yc_w
    merged = merged + jax.nn.sigmoid(gm_ref[...].astype(F32)) * ym_w
    o_ref[...] = merged.astype(BF16)


def _outproj_kernel(m_ref, wo_ref, x_ref, fg_ref, o_ref, *, final_norm):
    y = x_ref[...] + _dot(m_ref[...], wo_ref[...])
    if final_norm:
        ms = jnp.mean(y * y, axis=-1, keepdims=True)
        y = y * lax.rsqrt(ms + EPS) * fg_ref[...]
    o_ref[...] = y


def _outproj(merged, w_out, x, final_g, layer, final_norm):
    seq = x.shape[0]
    tm = min(512, seq)
    return pl.pallas_call(
        functools.partial(_outproj_kernel, final_norm=final_norm),
        grid=(seq // tm,),
        in_specs=[
            pl.BlockSpec((tm, D_MODEL), lambda i: (i, 0)),
            pl.BlockSpec((None, D_MODEL, D_MODEL), lambda i: (layer, 0, 0)),
            pl.BlockSpec((tm, D_MODEL), lambda i: (i, 0)),
            pl.BlockSpec((1, D_MODEL), lambda i: (0, 0)),
        ],
        out_specs=pl.BlockSpec((tm, D_MODEL), lambda i: (i, 0)),
        out_shape=jax.ShapeDtypeStruct((seq, D_MODEL), F32),
        compiler_params=_cparams(("parallel",)),
        name="out_proj",
    )(merged, w_out, x, final_g)


def _merge(ys, h, w_branch, layer):
    seq = h.shape[0]
    tm = min(1024, seq)
    tn = 512
    widths = (SSM_WIDTH, DN_WIDTH, ATT_WIDTH, MEM_WIDTH)
    half = ATT_WIDTH // 2
    assert SSM_WIDTH == DN_WIDTH and (SSM_WIDTH + DN_WIDTH) % half == 0 and MEM_WIDTH == half
    c_blk = (SSM_WIDTH + DN_WIDTH) // half
    in_specs = [pl.BlockSpec((tm, wd), lambda i, j: (i, 0)) for wd in widths]
    for b in range(N_BRANCH):
        base = (OFF_GATE + b * D_MODEL) // tn
        in_specs.append(pl.BlockSpec((tm, tn), lambda i, j, base=base: (i, base + j)))
    in_specs += [
        pl.BlockSpec((None, SSM_WIDTH, tn), lambda i, j: (layer, 0, j)),
        pl.BlockSpec((None, DN_WIDTH, tn), lambda i, j: (layer, 1, j)),
        pl.BlockSpec((None, half, tn), lambda i, j: (layer, c_blk, j)),
        pl.BlockSpec((None, half, tn), lambda i, j: (layer, c_blk + 1, j)),
        pl.BlockSpec((None, half, tn), lambda i, j: (layer, c_blk + 2, j)),
    ]
    return pl.pallas_call(
        _merge_kernel,
        grid=(seq // tm, D_MODEL // tn),
        in_specs=in_specs,
        out_specs=pl.BlockSpec((tm, tn), lambda i, j: (i, j)),
        out_shape=jax.ShapeDtypeStruct((seq, D_MODEL), BF16),
        compiler_params=_cparams(("parallel", "parallel")),
        name="gate_merge",
    )(*ys, h, h, h, h, w_branch, w_branch, w_branch, w_branch, w_branch)


def _rope_tables(seq):
    rows = seq // GRID_W
    row = np.repeat(np.arange(rows), GRID_W).astype(np.float64)
    col = np.tile(np.arange(GRID_W), rows).astype(np.float64)
    axis_dim = ATT_HEAD_DIM // 2
    freqs = ROPE_THETA ** (-np.arange(0, axis_dim, 2, dtype=np.float64) / axis_dim)
    ang = np.concatenate([row[:, None] * freqs, col[:, None] * freqs], axis=-1)
    cos = np.cos(ang).astype(np.float32)
    sin = np.sin(ang).astype(np.float32)
    return (jnp.asarray(np.repeat(cos, 2, axis=-1)),
            jnp.asarray(np.stack([-sin, sin], axis=-1).reshape(seq, ATT_HEAD_DIM)))


def _pair_rows(p):
    return p.reshape(2, SSM_PAIRS, 1, 2 * SSM_STATE).transpose(1, 0, 2, 3)


def _stacked_weights(w_in, w_branch, w_out):
    w_main = jnp.concatenate([w_in[:, :, :IN_AB], w_in[:, :, IN_AB_END:]], axis=2).astype(BF16)
    w_ab = jnp.pad(w_in[:, :, IN_AB:IN_AB_END],
                   ((0, 0), (0, 0), (0, AB_PAD - (IN_AB_END - IN_AB)))).astype(BF16)
    return w_main, w_ab, w_branch.astype(BF16), w_out.astype(BF16)


def _layer_params(layer, ssm_a_re, ssm_a_im, ssm_log_step, ssm_b_re, ssm_b_im, ssm_c_re,
                  ssm_c_im, ssm_d, ssm_w_glu, ssm_b_glu, dn_conv, dn_a_log, dn_dt_bias,
                  dn_norm_g, attn_q_norm, attn_k_norm):
    ls_n = jnp.broadcast_to(ssm_log_step[layer][:, :, None], ssm_a_re[layer].shape)
    flat = lambda p: p.reshape(2, 1, SSM_GROUPS * SSM_STATE)

    def b_pairs(b):
        bt = b.transpose(0, 1, 3, 2).reshape(2, SSM_PAIRS, 2, SSM_GROUP, SSM_STATE)
        return bt.transpose(1, 0, 3, 2, 4).reshape(SSM_PAIRS, 2, SSM_GROUP, 2 * SSM_STATE)

    def c_pairs(c):
        ct = c.transpose(0, 1, 3, 2).reshape(2, SSM_PAIRS, 2 * SSM_STATE, SSM_GROUP)
        return ct.transpose(1, 0, 2, 3)

    conv = dn_conv[layer].T.reshape(DN_CONV, 3, DN_WIDTH).transpose(1, 0, 2)
    conv = jnp.pad(conv, ((0, 0), (0, 8 - DN_CONV), (0, 0)))

    def lane_vec(p):
        return jnp.pad(p.reshape(1, -1), ((0, 0), (0, AB_PAD - p.size)))

    return {
        "ssm_rows": [_pair_rows(ssm_a_re[layer]), _pair_rows(ssm_a_im[layer]), _pair_rows(ls_n)],
        "ssm_b": [b_pairs(ssm_b_re[layer]), b_pairs(ssm_b_im[layer])],
        "ssm_c": [c_pairs(ssm_c_re[layer]), c_pairs(ssm_c_im[layer])],
        "ssm_flat": [flat(ssm_a_re[layer]), flat(ssm_a_im[layer]), flat(ls_n)],
        "ssm_d": ssm_d[layer].reshape(1, SSM_WIDTH),
        "ssm_w_glu": ssm_w_glu[layer],
        "ssm_b_glu": ssm_b_glu[layer].reshape(1, SSM_WIDTH),
        "dn_conv": conv,
        "dn_alog": lane_vec(dn_a_log[layer]),
        "dn_dtb": lane_vec(dn_dt_bias[layer]),
        "dn_norm_g": dn_norm_g[layer].reshape(1, DN_HEAD_DIM),
        "att_gq": attn_q_norm[layer].reshape(1, ATT_HEAD_DIM),
        "att_gk": attn_k_norm[layer].reshape(1, ATT_HEAD_DIM),
    }


def kernel(x, mem, norm_g, w_in, ssm_a_re, ssm_a_im, ssm_log_step, ssm_b_re, ssm_b_im, ssm_c_re, ssm_c_im, ssm_d, ssm_w_glu, ssm_b_glu, dn_conv, dn_a_log, dn_dt_bias, dn_norm_g, attn_q_norm, attn_k_norm, mem_norm_g, w_mem_kv, w_branch, w_out, final_norm_g):
    bsz, seq, _ = x.shape
    depth = w_in.shape[0]
    cos2, sin2 = _rope_tables(seq)
    final_g = final_norm_g.reshape(1, D_MODEL)
    w_main, w_ab, w_br, w_o = _stacked_weights(w_in, w_branch, w_out)
    outs = []
    for b in range(bsz):
        xb = x[b]
        for layer in range(depth):
            lp = _layer_params(layer, ssm_a_re, ssm_a_im, ssm_log_step, ssm_b_re, ssm_b_im,
                               ssm_c_re, ssm_c_im, ssm_d, ssm_w_glu, ssm_b_glu, dn_conv, dn_a_log,
                               dn_dt_bias, dn_norm_g, attn_q_norm, attn_k_norm)
            h, ab = _inproj(xb, norm_g[layer].reshape(1, D_MODEL), w_main, w_ab, layer)
            y_a = _ssm_branch(h, lp)
            y_b = _dn_branch(h, ab, lp)
            qt, kr, vt = _att_prep(h, cos2, sin2, lp["att_gq"], lp["att_gk"])
            y_c = _att(qt, kr, vt, h)
            kv = _mem_kv(mem[b], mem_norm_g[layer].reshape(1, D_MODEL), w_mem_kv[layer])
            y_m = _mem_att(h, kv)
            merged = _merge((y_a, y_b, y_c, y_m), h, w_br, layer)
            xb = _outproj(merged, w_o, xb, final_g, layer, final_norm=(layer == depth - 1))
        outs.append(xb)
    return outs[0][None] if bsz == 1 else jnp.stack(outs, axis=0)
```

```python
import functools
import math

import numpy as np
import jax
import jax.numpy as jnp
from jax import lax
from jax.experimental import pallas as pl
from jax.experimental.pallas import tpu as pltpu

F32 = jnp.float32
BF16 = jnp.bfloat16

D_MODEL = 2048
GRID_W = 64
EPS = 1e-6

SSM_GROUP = 16
SSM_STATE = 64
SSM_GROUPS = 48
SSM_WIDTH = SSM_GROUPS * SSM_GROUP
SSM_T = 16
SSM_PAIRS = SSM_GROUPS // 2
SSM_CW = SSM_T * SSM_GROUP

DN_HEADS = 6
DN_HEAD_DIM = 128
DN_WIDTH = DN_HEADS * DN_HEAD_DIM
DN_CONV = 5
DN_CHUNK = 64
DN_GROUP = 256
DN_HB = 6

ATT_HEADS = 8
ATT_KV_HEADS = 2
ATT_HEAD_DIM = 128
ATT_WIDTH = ATT_HEADS * ATT_HEAD_DIM
ATT_KV_WIDTH = ATT_KV_HEADS * ATT_HEAD_DIM
ROPE_THETA = 10000.0

MEM_HEADS = 4
MEM_HEAD_DIM = 128
MEM_WIDTH = MEM_HEADS * MEM_HEAD_DIM

N_BRANCH = 4

OFF_UA = 0
OFF_ZA = 768
OFF_DQ = 1536
OFF_DK = 2304
OFF_DV = 3072
OFF_ZB = 3840
OFF_AQ = 4608
OFF_AK = 5632
OFF_AV = 5888
OFF_ZC = 6144
OFF_MQ = 7168
OFF_ZM = 7680
OFF_GATE = 8192
IN_AB = 3840
IN_AB_END = IN_AB + 4 * DN_HEADS
H_WIDTH = OFF_GATE + N_BRANCH * D_MODEL
AB_PAD = 128

VMEM_LIMIT = 56 * 1024 * 1024


def _cparams(sem):
    return pltpu.CompilerParams(dimension_semantics=sem, vmem_limit_bytes=VMEM_LIMIT)


def _silu(x):
    return x * jax.nn.sigmoid(x)


def _dot(a, b):
    return jnp.dot(a, b, preferred_element_type=F32)


def _dot_nt(a, b):
    return lax.dot_general(a, b, (((1,), (1,)), ((), ())), preferred_element_type=F32)


def _dot_tn(a, b):
    return lax.dot_general(a, b, (((0,), (0,)), ((), ())), preferred_element_type=F32)


def _split3(x):
    x1 = x.astype(BF16)
    r1 = x - x1.astype(F32)
    x2 = r1.astype(BF16)
    x3 = (r1 - x2.astype(F32)).astype(BF16)
    return x1, x2, x3


def _dot_exact_lhs(a_bf16, x):
    x1, x2, x3 = _split3(x)
    return _dot(a_bf16, x1) + _dot(a_bf16, x2) + _dot(a_bf16, x3)


def _dot_exact_rhs(x, b_bf16):
    x1, x2, x3 = _split3(x)
    return _dot(x1, b_bf16) + _dot(x2, b_bf16) + _dot(x3, b_bf16)


def _dot_f32(a, b):
    a1, a2, a3 = _split3(a)
    b1, b2, b3 = _split3(b)
    return (_dot(a1, b1) + (_dot(a1, b2) + _dot(a2, b1))
            + (_dot(a1, b3) + _dot(a2, b2) + _dot(a3, b1)))


def _inproj_kernel(x_ref, g_ref, w_ref, wab_ref, h_ref, ab_ref, xn_ref):
    @pl.when(pl.program_id(1) == 0)
    def _():
        x = x_ref[...]
        ms = jnp.mean(x * x, axis=-1, keepdims=True)
        xn_ref[...] = (x * lax.rsqrt(ms + EPS) * g_ref[...]).astype(BF16)
        ab_ref[...] = _dot(xn_ref[...], wab_ref[...])

    h_ref[...] = _dot(xn_ref[...], w_ref[...]).astype(BF16)


def _inproj(x, g, w_main, w_ab, layer):
    seq = x.shape[0]
    tm = min(1024, seq)
    tn = 2048
    return pl.pallas_call(
        _inproj_kernel,
        grid=(seq // tm, H_WIDTH // tn),
        in_specs=[
            pl.BlockSpec((tm, D_MODEL), lambda i, j: (i, 0)),
            pl.BlockSpec((1, D_MODEL), lambda i, j: (0, 0)),
            pl.BlockSpec((None, D_MODEL, tn), lambda i, j: (layer, 0, j)),
            pl.BlockSpec((None, D_MODEL, AB_PAD), lambda i, j: (layer, 0, 0)),
        ],
        out_specs=[
            pl.BlockSpec((tm, tn), lambda i, j: (i, j)),
            pl.BlockSpec((tm, AB_PAD), lambda i, j: (i, 0)),
        ],
        out_shape=[
            jax.ShapeDtypeStruct((seq, H_WIDTH), BF16),
            jax.ShapeDtypeStruct((seq, AB_PAD), F32),
        ],
        scratch_shapes=[pltpu.VMEM((tm, D_MODEL), BF16)],
        compiler_params=_cparams(("parallel", "arbitrary")),
        name="inproj",
    )(x, g, w_main, w_ab)


def _cpow_table(base_re, base_im, n):
    re = [jnp.ones_like(base_re)]
    im = [jnp.zeros_like(base_im)]
    for _ in range(n):
        re_n = re[-1] * base_re - im[-1] * base_im
        im_n = re[-1] * base_im + im[-1] * base_re
        re.append(re_n)
        im.append(im_n)
    return re, im


def _ssm_prep_kernel(are_r, aim_r, ls_r, bre_ref, bim_ref, cre_ref, cim_ref,
                     win_ref, wout_ref, m_ref):
    t = SSM_T
    cw = SSM_CW
    lane128 = lax.broadcasted_iota(jnp.int32, (1, 128), 1)
    col_t = lax.broadcasted_iota(jnp.int32, (1, cw), 1) // SSM_GROUP
    sub128 = lax.broadcasted_iota(jnp.int32, (128, 1), 0)
    lane_cw = lax.broadcasted_iota(jnp.int32, (SSM_GROUP, cw), 1)
    tile_p = jnp.where(lane_cw % SSM_GROUP == lax.broadcasted_iota(jnp.int32, (SSM_GROUP, cw), 0),
                       1.0, 0.0).astype(BF16)

    m_ref[...] = jnp.zeros(m_ref.shape, m_ref.dtype)

    dirs = ({}, {})
    for d, st in enumerate(dirs):
        step = jnp.exp(ls_r[d])
        a_re = are_r[d]
        a_im = aim_r[d]
        mag = jnp.exp(a_re * step)
        lam_re = mag * jnp.cos(a_im * step)
        lam_im = mag * jnp.sin(a_im * step)
        den = a_re * a_re + a_im * a_im
        nr = lam_re - 1.0
        ni = lam_im
        coef_re = (nr * a_re + ni * a_im) / den
        coef_im = (ni * a_re - nr * a_im) / den
        b_re = bre_ref[d]
        b_im = bim_ref[d]
        st["bbt_re"] = coef_re * b_re - coef_im * b_im
        st["bbt_im"] = coef_re * b_im + coef_im * b_re
        st["pr"] = _cpow_table(lam_re, lam_im, t)
    for d, st in enumerate(dirs):
        pr_re, pr_im = st["pr"]
        bb_re = jnp.concatenate([st["bbt_re"]] * t, axis=0)
        bb_im = jnp.concatenate([st["bbt_im"]] * t, axis=0)
        e_in = [(t - 1 - s) if d == 0 else s for s in range(t)]
        p_re = jnp.concatenate([jnp.broadcast_to(pr_re[e], (SSM_GROUP, 128)) for e in e_in], axis=0)
        p_im = jnp.concatenate([jnp.broadcast_to(pr_im[e], (SSM_GROUP, 128)) for e in e_in], axis=0)
        w_re = p_re * bb_re - p_im * bb_im
        w_im = p_re * bb_im + p_im * bb_re
        for par in range(2):
            keep = (lane128 < 64) if par == 0 else (lane128 >= 64)
            win_ref[par * cw:(par + 1) * cw, (2 * d) * 128:(2 * d + 1) * 128] = (
                jnp.where(keep, w_re, 0.0).astype(BF16))
            win_ref[par * cw:(par + 1) * cw, (2 * d + 1) * 128:(2 * d + 2) * 128] = (
                jnp.where(keep, w_im, 0.0).astype(BF16))
    for d, st in enumerate(dirs):
        pr_re, pr_im = st["pr"]
        tab_re = jnp.zeros((128, 128), F32)
        tab_im = jnp.zeros((128, 128), F32)
        for e in range(t + 1):
            tab_re = jnp.where(sub128 == e, pr_re[e], tab_re)
            tab_im = jnp.where(sub128 == e, pr_im[e], tab_im)
        st["tab_re"] = tab_re.T
        st["tab_im"] = tab_im.T
    for d, st in enumerate(dirs):
        st["c_re"] = _dot_exact_rhs(cre_ref[d], tile_p)
        st["c_im"] = _dot_exact_rhs(cim_ref[d], tile_p)

    def c_lam(st, expo_row):
        sel = jnp.where(sub128 == expo_row, 1.0, 0.0).astype(BF16)
        q_re = _dot_exact_rhs(st["tab_re"], sel)
        q_im = _dot_exact_rhs(st["tab_im"], sel)
        return (st["c_re"] * q_re - st["c_im"] * q_im, -(st["c_re"] * q_im + st["c_im"] * q_re))

    for d, st in enumerate(dirs):
        st["o"] = c_lam(st, (col_t + 1) if d == 0 else (t - col_t))
    for d, st in enumerate(dirs):
        st["r"] = c_lam(st, col_t if d == 0 else (t - 1 - col_t))
    for d, st in enumerate(dirs):
        o_re, o_im = st["o"]
        for par in range(2):
            keep = (sub128 < 64) if par == 0 else (sub128 >= 64)
            wout_ref[(2 * d) * 128:(2 * d + 1) * 128, par * cw:(par + 1) * cw] = (
                jnp.where(keep, o_re, 0.0).astype(BF16))
            wout_ref[(2 * d + 1) * 128:(2 * d + 2) * 128, par * cw:(par + 1) * cw] = (
                jnp.where(keep, o_im, 0.0).astype(BF16))
    krows = [[None, None], [None, None]]
    for d, st in enumerate(dirs):
        r_re, r_im = st["r"]
        for par in range(2):
            keep = (lane128 < 64) if par == 0 else (lane128 >= 64)
            krows[d][par] = (_dot_f32(jnp.where(keep, st["bbt_re"], 0.0), r_re)
                             + _dot_f32(jnp.where(keep, st["bbt_im"], 0.0), r_im))

    for par in range(2):
        kf = krows[0][par]
        kb = krows[1][par]
        for s in range(t):
            sh_f = SSM_GROUP * s
            blk = jnp.where(lane_cw >= sh_f, pltpu.roll(kf, sh_f, axis=1) if sh_f else kf, 0.0)
            sh_b = SSM_GROUP * (t - 1 - s)
            rolled_b = pltpu.roll(kb, cw - sh_b, axis=1) if sh_b else kb
            blk = blk + jnp.where(lane_cw < cw - sh_b, rolled_b, 0.0)
            m_ref[par * cw + s * SSM_GROUP:par * cw + (s + 1) * SSM_GROUP,
                  par * cw:(par + 1) * cw] = blk.astype(BF16)


def _ssm_prep(rows, b_t, c_t):
    pw = 2 * SSM_CW
    row_spec = pl.BlockSpec((None, 2, 1, 128), lambda g: (g, 0, 0, 0))
    b_spec = pl.BlockSpec((None, 2, SSM_GROUP, 128), lambda g: (g, 0, 0, 0))
    c_spec = pl.BlockSpec((None, 2, 128, SSM_GROUP), lambda g: (g, 0, 0, 0))
    w_spec = pl.BlockSpec((None, pw, pw), lambda g: (g, 0, 0))
    return pl.pallas_call(
        _ssm_prep_kernel,
        grid=(SSM_PAIRS,),
        in_specs=[row_spec] * 3 + [b_spec] * 2 + [c_spec] * 2,
        out_specs=[w_spec] * 3,
        out_shape=[jax.ShapeDtypeStruct((SSM_PAIRS, pw, pw), BF16)] * 3,
        compiler_params=_cparams(("parallel",)),
        name="ssm_prep",
    )(*rows, *b_t, *c_t)


SSM_BG = 8
SSM_BP = SSM_BG // 2
SSM_BW = SSM_BG * SSM_CW


def _ssm_place(t0):
    r = lax.broadcasted_iota(jnp.int32, (256, SSM_BW), 0)
    c = lax.broadcasted_iota(jnp.int32, (256, SSM_BW), 1)
    j = r % 128
    target = (j // SSM_GROUP) * SSM_CW + (t0 + r // 128) * SSM_GROUP + j % SSM_GROUP
    return jnp.where(c == target, 1.0, 0.0).astype(BF16)


def _ssm_in_kernel(u_ref, win_ref, u2_ref, h_ref, uf_ref):
    nchunk = u2_ref.shape[0]
    uf_ref[...] = u_ref[...].astype(F32)
    acc = jnp.zeros((nchunk, SSM_BW), F32)
    for t0 in range(0, SSM_T, 2):
        lhs = jnp.concatenate([uf_ref[pl.ds(t0, nchunk, stride=SSM_T), :],
                               uf_ref[pl.ds(t0 + 1, nchunk, stride=SSM_T), :]], axis=1)
        acc = acc + _dot(lhs.astype(BF16), _ssm_place(t0))
    u2_ref[...] = acc.astype(BF16)
    pw = 2 * SSM_CW
    for j in range(SSM_BP):
        h = _dot(u2_ref[:, j * pw:(j + 1) * pw], win_ref[j])
        for k in range(4):
            h_ref[k, :, j * 128:(j + 1) * 128] = h[:, k * 128:(k + 1) * 128]


def _ssm_in(h, win):
    seq = h.shape[0]
    nchunk = seq // SSM_T
    pw = 2 * SSM_CW
    nblk = SSM_GROUPS // SSM_BG
    return pl.pallas_call(
        _ssm_in_kernel,
        grid=(nblk,),
        in_specs=[
            pl.BlockSpec((seq, 128), lambda b: (0, OFF_UA // 128 + b)),
            pl.BlockSpec((SSM_BP, pw, pw), lambda b: (b, 0, 0)),
        ],
        out_specs=[
            pl.BlockSpec((nchunk, SSM_BW), lambda b: (0, b)),
            pl.BlockSpec((4, nchunk, SSM_BP * 128), lambda b: (0, 0, b)),
        ],
        out_shape=[
            jax.ShapeDtypeStruct((nchunk, SSM_GROUPS * SSM_CW), BF16),
            jax.ShapeDtypeStruct((4, nchunk, SSM_PAIRS * 128), F32),
        ],
        scratch_shapes=[pltpu.VMEM((seq, 128), F32)],
        compiler_params=_cparams(("parallel",)),
        name="ssm_in",
    )(h, win)


def _ssm_scan_kernel(are_ref, aim_ref, ls_ref, h_ref, p_ref, *, nchunk):
    width = h_ref.shape[2]

    def lam_pow_t(d):
        step = jnp.exp(ls_ref[d]) * float(SSM_T)
        mag = jnp.exp(are_ref[d] * step)
        ang = aim_ref[d] * step
        return mag * jnp.cos(ang), mag * jnp.sin(ang)

    lfr, lfi = lam_pow_t(0)
    lbr, lbi = lam_pow_t(1)

    def body(c, carry):
        fr, fi, br, bi = carry
        cb = nchunk - 1 - c
        p_ref[0, pl.ds(c, 1), :] = fr
        p_ref[1, pl.ds(c, 1), :] = fi
        p_ref[2, pl.ds(cb, 1), :] = br
        p_ref[3, pl.ds(cb, 1), :] = bi
        hfr = h_ref[0, pl.ds(c, 1), :]
        hfi = h_ref[1, pl.ds(c, 1), :]
        hbr = h_ref[2, pl.ds(cb, 1), :]
        hbi = h_ref[3, pl.ds(cb, 1), :]
        nfr = lfr * fr - lfi * fi + hfr
        nfi = lfr * fi + lfi * fr + hfi
        nbr = lbr * br - lbi * bi + hbr
        nbi = lbr * bi + lbi * br + hbi
        return nfr, nfi, nbr, nbi

    z = jnp.zeros((1, width), F32)
    lax.fori_loop(0, nchunk, body, (z, z, z, z))


def _ssm_scan(flat_params, h):
    nchunk = h.shape[1]
    width = h.shape[2]
    wt = 768
    return pl.pallas_call(
        functools.partial(_ssm_scan_kernel, nchunk=nchunk),
        grid=(width // wt,),
        in_specs=[pl.BlockSpec((2, 1, wt), lambda j: (0, 0, j))] * 3 + [
            pl.BlockSpec((4, nchunk, wt), lambda j: (0, 0, j)),
        ],
        out_specs=pl.BlockSpec((4, nchunk, wt), lambda j: (0, 0, j)),
        out_shape=jax.ShapeDtypeStruct(h.shape, F32),
        compiler_params=_cparams(("parallel",)),
        name="ssm_scan",
    )(*flat_params, h)


def _ssm_out_kernel(u2_ref, p_ref, m_ref, wout_ref, y_ref):
    nchunk = u2_ref.shape[0]
    pw = 2 * SSM_CW
    parts = []
    for j in range(SSM_BP):
        acc = _dot(u2_ref[:, j * pw:(j + 1) * pw], m_ref[j])
        for k in range(4):
            acc = acc + _dot(p_ref[k, :, j * 128:(j + 1) * 128].astype(BF16),
                             wout_ref[j, k * 128:(k + 1) * 128, :])
        parts.append(acc.astype(BF16))
    y16 = jnp.concatenate(parts, axis=1)
    for t0 in range(0, SSM_T, 2):
        yt = _dot_nt(y16, _ssm_place(t0))
        y_ref[pl.ds(t0, nchunk, stride=SSM_T), :] = yt[:, :128]
        y_ref[pl.ds(t0 + 1, nchunk, stride=SSM_T), :] = yt[:, 128:]


def _ssm_out(u2, p, m, wout):
    nchunk = u2.shape[0]
    seq = nchunk * SSM_T
    pw = 2 * SSM_CW
    nblk = SSM_GROUPS // SSM_BG
    return pl.pallas_call(
        _ssm_out_kernel,
        grid=(nblk,),
        in_specs=[
            pl.BlockSpec((nchunk, SSM_BW), lambda b: (0, b)),
            pl.BlockSpec((4, nchunk, SSM_BP * 128), lambda b: (0, 0, b)),
            pl.BlockSpec((SSM_BP, pw, pw), lambda b: (b, 0, 0)),
            pl.BlockSpec((SSM_BP, pw, pw), lambda b: (b, 0, 0)),
        ],
        out_specs=pl.BlockSpec((seq, 128), lambda b: (0, b)),
        out_shape=jax.ShapeDtypeStruct((seq, SSM_WIDTH), F32),
        compiler_params=_cparams(("parallel",)),
        name="ssm_out",
    )(u2, p, m, wout)


def _ssm_epi_kernel(y_ref, u_ref, z_ref, d_ref, wg_ref, bg_ref, o_ref):
    y = y_ref[...] + d_ref[...] * u_ref[...].astype(F32)
    y = jax.nn.gelu(y)
    glu = _dot(y.astype(BF16), wg_ref[...].astype(BF16)) + bg_ref[...]
    y = y * jax.nn.sigmoid(glu)
    o_ref[...] = (y * _silu(z_ref[...].astype(F32))).astype(BF16)


def _ssm_epi(y, h, d, w_glu, b_glu):
    seq = y.shape[0]
    tm = min(1024, seq)
    w = SSM_WIDTH
    return pl.pallas_call(
        _ssm_epi_kernel,
        grid=(seq // tm,),
        in_specs=[
            pl.BlockSpec((tm, w), lambda i: (i, 0)),
            pl.BlockSpec((tm, w), lambda i: (i, OFF_UA // w)),
            pl.BlockSpec((tm, w), lambda i: (i, OFF_ZA // w)),
            pl.BlockSpec((1, w), lambda i: (0, 0)),
            pl.BlockSpec((w, w), lambda i: (0, 0)),
            pl.BlockSpec((1, w), lambda i: (0, 0)),
        ],
        out_specs=pl.BlockSpec((tm, w), lambda i: (i, 0)),
        out_shape=jax.ShapeDtypeStruct((seq, w), BF16),
        compiler_params=_cparams(("parallel",)),
        name="ssm_epi",
    )(y, h, h, d, w_glu, b_glu)


def _ssm_branch(h, lp):
    win, wout, m = _ssm_prep(lp["ssm_rows"], lp["ssm_b"], lp["ssm_c"])
    u2, hs = _ssm_in(h, win)
    p = _ssm_scan(lp["ssm_flat"], hs)
    y = _ssm_out(u2, p, m, wout)
    return _ssm_epi(y, h, lp["ssm_d"], lp["ssm_w_glu"], lp["ssm_b_glu"])


def _dn_prep_kernel(qc, qp, qn, kc, kp, kn, vc, vp, vn, ab_ref, cw_ref, alog_ref, dtb_ref,
                    qo_ref, ko_ref, vo_ref, sc_ref, sct_ref, ext_ref, *, tm, nblk):
    i = pl.program_id(0)
    halo = 16
    pad = DN_CONV // 2

    def conv_silu(cur, prev, nxt, part):
        ext_ref[0:halo, :] = jnp.where(i > 0, prev[...].astype(F32), 0.0)
        ext_ref[halo:halo + tm, :] = cur[...].astype(F32)
        ext_ref[halo + tm:halo + tm + halo, :] = jnp.where(i < nblk - 1, nxt[...].astype(F32), 0.0)
        acc = jnp.zeros((tm, DN_WIDTH), F32)
        for j in range(DN_CONV):
            acc = acc + cw_ref[part, j:j + 1, :] * ext_ref[pl.ds(halo - pad + j, tm), :]
        return _silu(acc)

    def l2n(x):
        outs = []
        for hd in range(DN_HEADS):
            xh = x[:, hd * DN_HEAD_DIM:(hd + 1) * DN_HEAD_DIM]
            outs.append(xh * lax.rsqrt(jnp.sum(xh * xh, axis=-1, keepdims=True) + EPS))
        return jnp.concatenate(outs, axis=1)

    qo_ref[...] = (l2n(conv_silu(qc, qp, qn, 0)) * (DN_HEAD_DIM ** -0.5)).astype(BF16)
    ko_ref[...] = l2n(conv_silu(kc, kp, kn, 1)).astype(BF16)
    vo_ref[...] = conv_silu(vc, vp, vn, 2).astype(BF16)

    ab = ab_ref[...]
    g_all = -jnp.exp(alog_ref[...]) * jax.nn.softplus(ab + dtb_ref[...])
    beta_all = jax.nn.sigmoid(ab)
    r = lax.broadcasted_iota(jnp.int32, (tm, tm), 0)
    c = lax.broadcasted_iota(jnp.int32, (tm, tm), 1)
    same = (r // DN_CHUNK) == (c // DN_CHUNK)
    tri_f = jnp.where(same & (c <= r), 1.0, 0.0).astype(BF16)
    tri_b = jnp.where(same & (c >= r), 1.0, 0.0).astype(BF16)
    blk = jnp.where(same, 1.0, 0.0).astype(BF16)
    gcf = _dot_exact_lhs(tri_f, g_all)
    gcb = _dot_exact_lhs(tri_b, g_all)
    gtot = _dot_exact_lhs(blk, g_all)
    lane = lax.broadcasted_iota(jnp.int32, (tm, AB_PAD), 1)
    sc = jnp.where(lane < 6, gcf,
                   jnp.where(lane < 12, gcb,
                             jnp.where(lane < 24, beta_all,
                                       jnp.where(lane < 36, pltpu.roll(gtot, 24, axis=1), 0.0))))
    sc_ref[...] = sc
    sct_ref[...] = sc.T


def _dn_prep(h, ab, conv_w, alog, dtb):
    seq = h.shape[0]
    tm = min(256, seq)
    nblk = seq // tm
    w = DN_WIDTH
    hb = tm // 16
    nh = seq // 16

    def cur(ci):
        return pl.BlockSpec((tm, w), lambda i: (i, ci))

    def prev(ci):
        return pl.BlockSpec((16, w), lambda i: (jnp.maximum(i * hb - 1, 0), ci))

    def nxt(ci):
        return pl.BlockSpec((16, w), lambda i: (jnp.minimum((i + 1) * hb, nh - 1), ci))

    in_specs = []
    for off in (OFF_DQ, OFF_DK, OFF_DV):
        ci = off // w
        in_specs += [cur(ci), prev(ci), nxt(ci)]
    in_specs += [
        pl.BlockSpec((tm, AB_PAD), lambda i: (i, 0)),
        pl.BlockSpec((3, 8, w), lambda i: (0, 0, 0)),
        pl.BlockSpec((1, AB_PAD), lambda i: (0, 0)),
        pl.BlockSpec((1, AB_PAD), lambda i: (0, 0)),
    ]
    return pl.pallas_call(
        functools.partial(_dn_prep_kernel, tm=tm, nblk=nblk),
        grid=(nblk,),
        in_specs=in_specs,
        out_specs=[pl.BlockSpec((tm, w), lambda i: (i, 0))] * 3
        + [pl.BlockSpec((tm, AB_PAD), lambda i: (i, 0)), pl.BlockSpec((AB_PAD, tm), lambda i: (0, i))],
        out_shape=[jax.ShapeDtypeStruct((seq, w), BF16)] * 3
        + [jax.ShapeDtypeStruct((seq, AB_PAD), F32), jax.ShapeDtypeStruct((AB_PAD, seq), F32)],
        scratch_shapes=[pltpu.VMEM((tm + 32, w), F32)],
        compiler_params=_cparams(("parallel",)),
        name="dn_prep",
    )(h, h, h, h, h, h, h, h, h, ab, conv_w, alog, dtb)


def _dn_main_kernel(qf_ref, kf_ref, vf_ref, scf_ref, sctf_ref,
                    qb_ref, kb_ref, vb_ref, scb_ref, sctb_ref, of_ref, ob_ref, s_ref):
    gsz = DN_GROUP
    nck = gsz // DN_CHUNK

    @pl.when(pl.program_id(1) == 0)
    def _():
        s_ref[...] = jnp.zeros(s_ref.shape, F32)

    refs = ((qf_ref, kf_ref, vf_ref, scf_ref, sctf_ref, of_ref),
            (qb_ref, kb_ref, vb_ref, scb_ref, sctb_ref, ob_ref))
    r = lax.broadcasted_iota(jnp.int32, (gsz, gsz), 0)
    c = lax.broadcasted_iota(jnp.int32, (gsz, gsz), 1)
    same = (r // DN_CHUNK) == (c // DN_CHUNK)
    incl = (same & (r >= c), same & (r <= c))
    strict = (same & (r > c), same & (r < c))

    chains = []
    for d in range(2):
        q_ref, k_ref, v_ref, sc_ref, sct_ref, o_ref = refs[d]
        sc = sc_ref[...]
        sct = sct_ref[...]
        lane = lax.broadcasted_iota(jnp.int32, sc.shape, 1)
        sub = lax.broadcasted_iota(jnp.int32, sct.shape, 0)
        for hl in range(DN_HB):
            cidx = d * DN_HEADS + pl.program_id(0) * DN_HB + hl
            hs = slice(hl * DN_HEAD_DIM, (hl + 1) * DN_HEAD_DIM)

            def col(ci, sc=sc, lane=lane):
                return jnp.sum(jnp.where(lane == ci, sc, 0.0), axis=1, keepdims=True)

            ch = {"d": d, "hl": hl, "hs": hs, "o_ref": o_ref}
            ch["q"] = q_ref[:, hs]
            ch["k"] = k_ref[:, hs]
            ch["v"] = v_ref[:, hs]
            ch["gc_col"] = col(cidx)
            ch["gc_row"] = jnp.sum(jnp.where(sub == cidx, sct, 0.0), axis=0, keepdims=True)
            ch["beta_col"] = col(12 + cidx)
            ch["gtot_col"] = col(24 + cidx)
            chains.append(ch)

    for ch in chains:
        ch["kf"] = ch["k"].astype(F32)
        ch["kb"] = ch["kf"] * ch["beta_col"]
        ch["gram"] = _dot_nt(ch["kb"].astype(BF16), ch["k"])
    for ch in chains:
        ch["qk"] = _dot_nt(ch["q"], ch["k"])
    for ch in chains:
        d = ch["d"]
        ch["decay"] = jnp.where(incl[d], jnp.exp(jnp.where(incl[d], ch["gc_col"] - ch["gc_row"], 0.0)), 0.0)
        ch["x"] = jnp.concatenate([ch["v"].astype(F32) * ch["beta_col"],
                                   ch["kb"] * jnp.exp(ch["gc_col"])], axis=1)
        ch["qd16"] = (ch["q"].astype(F32) * jnp.exp(ch["gc_col"])).astype(BF16)
        ch["kd16"] = (ch["kf"] * jnp.exp(ch["gtot_col"] - ch["gc_col"])).astype(BF16)
    for ch in chains:
        ch["p"] = jnp.where(strict[ch["d"]], ch["gram"] * ch["decay"], 0.0).astype(BF16)
        ch["intra"] = (ch["qk"] * ch["decay"]).astype(BF16)
    for ch in chains:
        ch["x"] = ch["x"] - _dot(ch["p"], ch["x"].astype(BF16))
    npow = 2
    while npow < DN_CHUNK:
        for ch in chains:
            ch["p"] = _dot(ch["p"], ch["p"]).astype(BF16)
        for ch in chains:
            ch["x"] = ch["x"] + _dot(ch["p"], ch["x"].astype(BF16))
        npow *= 2
    for ch in chains:
        ch["u"] = ch["x"][:, :DN_HEAD_DIM]
        ch["w16"] = ch["x"][:, DN_HEAD_DIM:].astype(BF16)
        ch["s"] = s_ref[ch["d"], ch["hl"]]
        ch["vnew"] = [None] * nck
        ch["oq"] = [None] * nck

    for step in range(nck):
        for ch in chains:
            j = step if ch["d"] == 0 else nck - 1 - step
            lo, hi = j * DN_CHUNK, (j + 1) * DN_CHUNK
            s16 = ch["s"].astype(BF16)
            ws = _dot(jnp.concatenate([ch["w16"][lo:hi], ch["qd16"][lo:hi]], axis=0), s16)
            vn = ch["u"][lo:hi] - ws[:DN_CHUNK]
            ch["oq"][j] = ws[DN_CHUNK:]
            ch["vnew"][j] = vn
            ch["s"] = (ch["s"] * jnp.exp(ch["gtot_col"][lo:lo + 1, :])
                       + _dot_tn(ch["kd16"][lo:hi], vn.astype(BF16)))
    for ch in chains:
        s_ref[ch["d"], ch["hl"]] = ch["s"]
        vn_all = jnp.concatenate(ch["vnew"], axis=0).astype(BF16)
        ch["o_ref"][:, ch["hs"]] = jnp.concatenate(ch["oq"], axis=0) + _dot(ch["intra"], vn_all)


def _dn_main(qn, kn, vc, sc, sct):
    seq = qn.shape[0]
    gsz = DN_GROUP
    ng = seq // gsz
    hw = DN_HB * DN_HEAD_DIM
    sct_rows = sct.shape[0]
    fwd = lambda h, i: (i, h)
    bwd = lambda h, i: (ng - 1 - i, h)
    in_specs = [
        pl.BlockSpec((gsz, hw), fwd), pl.BlockSpec((gsz, hw), fwd), pl.BlockSpec((gsz, hw), fwd),
        pl.BlockSpec((gsz, AB_PAD), lambda h, i: (i, 0)),
        pl.BlockSpec((sct_rows, gsz), lambda h, i: (0, i)),
        pl.BlockSpec((gsz, hw), bwd), pl.BlockSpec((gsz, hw), bwd), pl.BlockSpec((gsz, hw), bwd),
        pl.BlockSpec((gsz, AB_PAD), lambda h, i: (ng - 1 - i, 0)),
        pl.BlockSpec((sct_rows, gsz), lambda h, i: (0, ng - 1 - i)),
    ]
    return pl.pallas_call(
        _dn_main_kernel,
        grid=(DN_HEADS // DN_HB, ng),
        in_specs=in_specs,
        out_specs=[pl.BlockSpec((gsz, hw), fwd), pl.BlockSpec((gsz, hw), bwd)],
        out_shape=[jax.ShapeDtypeStruct((seq, DN_WIDTH), F32)] * 2,
        scratch_shapes=[pltpu.VMEM((2, DN_HB, DN_HEAD_DIM, DN_HEAD_DIM), F32)],
        compiler_params=_cparams(("parallel", "arbitrary")),
        name="dn_main",
    )(qn, kn, vc, sc, sct, qn, kn, vc, sc, sct)


def _dn_epi_kernel(of_ref, ob_ref, z_ref, g_ref, o_ref):
    o = of_ref[...] + ob_ref[...]
    g = g_ref[...]
    outs = []
    for hd in range(DN_HEADS):
        oh = o[:, hd * DN_HEAD_DIM:(hd + 1) * DN_HEAD_DIM]
        ms = jnp.mean(oh * oh, axis=-1, keepdims=True)
        outs.append(oh * lax.rsqrt(ms + EPS) * g)
    y = jnp.concatenate(outs, axis=1)
    o_ref[...] = (y * _silu(z_ref[...].astype(F32))).astype(BF16)


def _dn_epi(o_f, o_b, h, norm_g):
    seq = o_f.shape[0]
    tm = min(1024, seq)
    w = DN_WIDTH
    return pl.pallas_call(
        _dn_epi_kernel,
        grid=(seq // tm,),
        in_specs=[
            pl.BlockSpec((tm, w), lambda i: (i, 0)),
            pl.BlockSpec((tm, w), lambda i: (i, 0)),
            pl.BlockSpec((tm, w), lambda i: (i, OFF_ZB // w)),
            pl.BlockSpec((1, DN_HEAD_DIM), lambda i: (0, 0)),
        ],
        out_specs=pl.BlockSpec((tm, w), lambda i: (i, 0)),
        out_shape=jax.ShapeDtypeStruct((seq, w), BF16),
        compiler_params=_cparams(("parallel",)),
        name="dn_epi",
    )(o_f, o_b, h, norm_g)


def _dn_branch(h, ab, lp):
    qn, kn, vc, sc, sct = _dn_prep(h, ab, lp["dn_conv"], lp["dn_alog"], lp["dn_dtb"])
    o_f, o_b = _dn_main(qn, kn, vc, sc, sct)
    return _dn_epi(o_f, o_b, h, lp["dn_norm_g"])


def _att_prep_kernel(qlo_ref, qhi_ref, k_ref, v_ref, cos_ref, sin_ref, gq_ref, gk_ref,
                     qt_ref, ko_ref, vt_ref):
    cos2 = cos_ref[...]
    sin2 = sin_ref[...]
    rr = lax.broadcasted_iota(jnp.int32, (ATT_HEAD_DIM, ATT_HEAD_DIM), 0)
    cc = lax.broadcasted_iota(jnp.int32, (ATT_HEAD_DIM, ATT_HEAD_DIM), 1)
    eye = jnp.where(rr == cc, 1.0, 0.0).astype(BF16)
    swap = jnp.where(rr == (cc ^ 1), 1.0, 0.0).astype(BF16)

    gq = gq_ref[...]
    gk = gk_ref[...]
    scale = ATT_HEAD_DIM ** -0.5 * math.log2(math.e)

    def transposed(x16):
        return _dot_nt(eye, x16).astype(BF16)

    half = ATT_HEADS // 2
    xs = []
    for hd in range(ATT_HEADS):
        src = qlo_ref if hd < half else qhi_ref
        xs.append((src[:, (hd % half) * ATT_HEAD_DIM:(hd % half + 1) * ATT_HEAD_DIM], gq))
    for hd in range(ATT_KV_HEADS):
        xs.append((k_ref[:, hd * ATT_HEAD_DIM:(hd + 1) * ATT_HEAD_DIM], gk))
    xn = []
    for x16, g in xs:
        x = x16.astype(F32)
        ms = jnp.mean(x * x, axis=-1, keepdims=True)
        xn.append(x * lax.rsqrt(ms + EPS) * g)
    his = [v.astype(BF16) for v in xn]
    los = [(v - hi.astype(F32)).astype(BF16) for v, hi in zip(xn, his)]
    partner = [_dot(hi, swap) + _dot(lo, swap) for hi, lo in zip(his, los)]
    roped = [v * cos2 + pt * sin2 for v, pt in zip(xn, partner)]
    qts = [transposed((roped[hd] * scale).astype(BF16)) for hd in range(ATT_HEADS)]
    vts = [transposed(v_ref[:, hd * ATT_HEAD_DIM:(hd + 1) * ATT_HEAD_DIM]) for hd in range(ATT_KV_HEADS)]
    for hd in range(ATT_HEADS):
        qt_ref[hd * ATT_HEAD_DIM:(hd + 1) * ATT_HEAD_DIM, :] = qts[hd]
    for hd in range(ATT_KV_HEADS):
        sl = slice(hd * ATT_HEAD_DIM, (hd + 1) * ATT_HEAD_DIM)
        ko_ref[:, sl] = roped[ATT_HEADS + hd].astype(BF16)
        vt_ref[sl, :] = vts[hd]


def _att_prep(h, cos2, sin2, gq, gk):
    seq = h.shape[0]
    tm = min(1024, seq)
    return pl.pallas_call(
        _att_prep_kernel,
        grid=(seq // tm,),
        in_specs=[
            pl.BlockSpec((tm, ATT_WIDTH // 2), lambda i: (i, OFF_AQ // (ATT_WIDTH // 2))),
            pl.BlockSpec((tm, ATT_WIDTH // 2), lambda i: (i, OFF_AQ // (ATT_WIDTH // 2) + 1)),
            pl.BlockSpec((tm, ATT_KV_WIDTH), lambda i: (i, OFF_AK // ATT_KV_WIDTH)),
            pl.BlockSpec((tm, ATT_KV_WIDTH), lambda i: (i, OFF_AV // ATT_KV_WIDTH)),
            pl.BlockSpec((tm, ATT_HEAD_DIM), lambda i: (i, 0)),
            pl.BlockSpec((tm, ATT_HEAD_DIM), lambda i: (i, 0)),
            pl.BlockSpec((1, ATT_HEAD_DIM), lambda i: (0, 0)),
            pl.BlockSpec((1, ATT_HEAD_DIM), lambda i: (0, 0)),
        ],
        out_specs=[
            pl.BlockSpec((ATT_WIDTH, tm), lambda i: (0, i)),
            pl.BlockSpec((tm, ATT_KV_WIDTH), lambda i: (i, 0)),
            pl.BlockSpec((ATT_KV_WIDTH, tm), lambda i: (0, i)),
        ],
        out_shape=[
            jax.ShapeDtypeStruct((ATT_WIDTH, seq), BF16),
            jax.ShapeDtypeStruct((seq, ATT_KV_WIDTH), BF16),
            jax.ShapeDtypeStruct((ATT_KV_WIDTH, seq), BF16),
        ],
        compiler_params=_cparams(("parallel",)),
        name="att_prep",
    )(h, h, h, h, cos2, sin2, gq, gk)


def _att_kernel(qt_ref, k_ref, vt_ref, z_ref, o_ref, m_ref, l_ref, acc_ref, *, kc, nkc):
    grp = qt_ref.shape[0] // ATT_HEAD_DIM
    m_ref[...] = jnp.full(m_ref.shape, -1e30, F32)
    l_ref[...] = jnp.zeros(l_ref.shape, F32)
    acc_ref[...] = jnp.zeros(acc_ref.shape, F32)

    def chunk(c, carry):
        k0 = pl.multiple_of(c * kc, kc)
        k_c = k_ref[pl.ds(k0, kc), :]
        vt_c = vt_ref[:, pl.ds(k0, kc)]
        s = [_dot(k_c, qt_ref[hh * ATT_HEAD_DIM:(hh + 1) * ATT_HEAD_DIM, :])
             for hh in range(grp)]
        p = [None] * grp
        alpha = [None] * grp
        for hh in range(grp):
            m_old = m_ref[hh]
            m_new = jnp.maximum(m_old, jnp.max(s[hh], axis=0, keepdims=True))
            alpha[hh] = jnp.exp2(m_old - m_new)
            ph = jnp.exp2(s[hh] - m_new)
            l_ref[hh] = alpha[hh] * l_ref[hh] + jnp.sum(ph, axis=0, keepdims=True)
            m_ref[hh] = m_new
            p[hh] = ph.astype(BF16)
        for hh in range(grp):
            acc_ref[hh] = alpha[hh] * acc_ref[hh] + _dot(vt_c, p[hh])
        return carry

    lax.fori_loop(0, nkc, chunk, 0, unroll=True)
    for hh in range(grp):
        sl = slice(hh * ATT_HEAD_DIM, (hh + 1) * ATT_HEAD_DIM)
        o = (acc_ref[hh] / l_ref[hh]).T
        o_ref[:, sl] = (o * _silu(z_ref[:, sl].astype(F32))).astype(BF16)


ATT_HPS = 4


def _att(qt, kr, vt, h):
    seq = kr.shape[0]
    tq = min(256, seq)
    kc = min(4096, seq)
    hw = ATT_HPS * ATT_HEAD_DIM
    steps_per_kv = ATT_HEADS // ATT_KV_HEADS // ATT_HPS
    return pl.pallas_call(
        functools.partial(_att_kernel, kc=kc, nkc=seq // kc),
        grid=(ATT_HEADS // ATT_HPS, seq // tq),
        in_specs=[
            pl.BlockSpec((hw, tq), lambda a, i: (a, i)),
            pl.BlockSpec((seq, ATT_HEAD_DIM), lambda a, i: (0, a // steps_per_kv)),
            pl.BlockSpec((ATT_HEAD_DIM, seq), lambda a, i: (a // steps_per_kv, 0)),
            pl.BlockSpec((tq, hw), lambda a, i: (i, OFF_ZC // hw + a)),
        ],
        out_specs=pl.BlockSpec((tq, hw), lambda a, i: (i, a)),
        out_shape=jax.ShapeDtypeStruct((seq, ATT_WIDTH), BF16),
        scratch_shapes=[
            pltpu.VMEM((ATT_HPS, 1, tq), F32),
            pltpu.VMEM((ATT_HPS, 1, tq), F32),
            pltpu.VMEM((ATT_HPS, ATT_HEAD_DIM, tq), F32),
        ],
        compiler_params=_cparams(("parallel", "parallel")),
        name="grid_att",
    )(qt, kr, vt, h)


def _mem_kv_kernel(mem_ref, g_ref, w_ref, o_ref):
    x = mem_ref[...]
    ms = jnp.mean(x * x, axis=-1, keepdims=True)
    xn = (x * lax.rsqrt(ms + EPS) * g_ref[...]).astype(BF16)
    o_ref[...] = _dot(xn, w_ref[...].astype(BF16)).astype(BF16)


def _mem_kv(mem, g, w_kv):
    n_mem = mem.shape[0]
    tn = 512
    return pl.pallas_call(
        _mem_kv_kernel,
        grid=(2 * MEM_WIDTH // tn,),
        in_specs=[
            pl.BlockSpec((n_mem, D_MODEL), lambda j: (0, 0)),
            pl.BlockSpec((1, D_MODEL), lambda j: (0, 0)),
            pl.BlockSpec((D_MODEL, tn), lambda j: (0, j)),
        ],
        out_specs=pl.BlockSpec((n_mem, tn), lambda j: (0, j)),
        out_shape=jax.ShapeDtypeStruct((n_mem, 2 * MEM_WIDTH), BF16),
        compiler_params=_cparams(("parallel",)),
        name="mem_kv",
    )(mem, g, w_kv)


def _mem_att_kernel(q_ref, z_ref, kv_ref, o_ref):
    scale = MEM_HEAD_DIM ** -0.5
    sls = [slice(hd * MEM_HEAD_DIM, (hd + 1) * MEM_HEAD_DIM) for hd in range(MEM_HEADS)]
    s = [_dot_nt(q_ref[:, sl], kv_ref[:, sl]) * scale for sl in sls]
    p = []
    l = []
    for hd in range(MEM_HEADS):
        m = jnp.max(s[hd], axis=-1, keepdims=True)
        ph = jnp.exp(s[hd] - m)
        l.append(jnp.sum(ph, axis=-1, keepdims=True))
        p.append(ph.astype(BF16))
    o = [_dot(p[hd], kv_ref[:, MEM_WIDTH + hd * MEM_HEAD_DIM:MEM_WIDTH + (hd + 1) * MEM_HEAD_DIM]) / l[hd]
         for hd in range(MEM_HEADS)]
    for hd, sl in enumerate(sls):
        o_ref[:, sl] = (o[hd] * _silu(z_ref[:, sl].astype(F32))).astype(BF16)


def _mem_att(h, kv):
    seq = h.shape[0]
    tm = min(1024, seq)
    n_mem = kv.shape[0]
    return pl.pallas_call(
        _mem_att_kernel,
        grid=(seq // tm,),
        in_specs=[
            pl.BlockSpec((tm, MEM_WIDTH), lambda i: (i, OFF_MQ // MEM_WIDTH)),
            pl.BlockSpec((tm, MEM_WIDTH), lambda i: (i, OFF_ZM // MEM_WIDTH)),
            pl.BlockSpec((n_mem, 2 * MEM_WIDTH), lambda i: (0, 0)),
        ],
        out_specs=pl.BlockSpec((tm, MEM_WIDTH), lambda i: (i, 0)),
        out_shape=jax.ShapeDtypeStruct((seq, MEM_WIDTH), BF16),
        compiler_params=_cparams(("parallel",)),
        name="mem_att",
    )(h, h, kv)


def _merge_kernel(ya_ref, yb_ref, yc_ref, ym_ref, ga_ref, gb_ref, gc_ref, gm_ref,
                  wa_ref, wb_ref, wc0_ref, wc1_ref, wm_ref, o_ref):
    half = ATT_WIDTH // 2
    ya_w = _dot(ya_ref[...], wa_ref[...])
    yb_w = _dot(yb_ref[...], wb_ref[...])
    yc_w = _dot(yc_ref[:, :half], wc0_ref[...]) + _dot(yc_ref[:, half:], wc1_ref[...])
    ym_w = _dot(ym_ref[...], wm_ref[...])
    merged = jax.nn.sigmoid(ga_ref[...].astype(F32)) * ya_w
    merged = merged + jax.nn.sigmoid(gb_ref[...].astype(F32)) * yb_w
    merged = merged + jax.nn.sigmoid(gc_ref[...].astype(F32)) * yc_w
    merged = merged + jax.nn.sigmoid(gm_ref[...].astype(F32)) * ym_w
    o_ref[...] = merged.astype(BF16)


def _outproj_kernel(m_ref, wo_ref, x_ref, fg_ref, o_ref, *, final_norm):
    y = x_ref[...] + _dot(m_ref[...], wo_ref[...])
    if final_norm:
        ms = jnp.mean(y * y, axis=-1, keepdims=True)
        y = y * lax.rsqrt(ms + EPS) * fg_ref[...]
    o_ref[...] = y


def _outproj(merged, w_out, x, final_g, layer, final_norm):
    seq = x.shape[0]
    tm = min(512, seq)
    return pl.pallas_call(
        functools.partial(_outproj_kernel, final_norm=final_norm),
        grid=(seq // tm,),
        in_specs=[
            pl.BlockSpec((tm, D_MODEL), lambda i: (i, 0)),
            pl.BlockSpec((None, D_MODEL, D_MODEL), lambda i: (layer, 0, 0)),
            pl.BlockSpec((tm, D_MODEL), lambda i: (i, 0)),
            pl.BlockSpec((1, D_MODEL), lambda i: (0, 0)),
        ],
        out_specs=pl.BlockSpec((tm, D_MODEL), lambda i: (i, 0)),
        out_shape=jax.ShapeDtypeStruct((seq, D_MODEL), F32),
        compiler_params=_cparams(("parallel",)),
        name="out_proj",
    )(merged, w_out, x, final_g)


def _merge(ys, h, w_branch, layer):
    seq = h.shape[0]
    tm = min(1024, seq)
    tn = 512
    widths = (SSM_WIDTH, DN_WIDTH, ATT_WIDTH, MEM_WIDTH)
    half = ATT_WIDTH // 2
    assert SSM_WIDTH == DN_WIDTH and (SSM_WIDTH + DN_WIDTH) % half == 0 and MEM_WIDTH == half
    c_blk = (SSM_WIDTH + DN_WIDTH) // half
    in_specs = [pl.BlockSpec((tm, wd), lambda i, j: (i, 0)) for wd in widths]
    for b in range(N_BRANCH):
        base = (OFF_GATE + b * D_MODEL) // tn
        in_specs.append(pl.BlockSpec((tm, tn), lambda i, j, base=base: (i, base + j)))
    in_specs += [
        pl.BlockSpec((None, SSM_WIDTH, tn), lambda i, j: (layer, 0, j)),
        pl.BlockSpec((None, DN_WIDTH, tn), lambda i, j: (layer, 1, j)),
        pl.BlockSpec((None, half, tn), lambda i, j: (layer, c_blk, j)),
        pl.BlockSpec((None, half, tn), lambda i, j: (layer, c_blk + 1, j)),
        pl.BlockSpec((None, half, tn), lambda i, j: (layer, c_blk + 2, j)),
    ]
    return pl.pallas_call(
        _merge_kernel,
        grid=(seq // tm, D_MODEL // tn),
        in_specs=in_specs,
        out_specs=pl.BlockSpec((tm, tn), lambda i, j: (i, j)),
        out_shape=jax.ShapeDtypeStruct((seq, D_MODEL), BF16),
        compiler_params=_cparams(("parallel", "parallel")),
        name="gate_merge",
    )(*ys, h, h, h, h, w_branch, w_branch, w_branch, w_branch, w_branch)


def _rope_tables(seq):
    rows = seq // GRID_W
    row = np.repeat(np.arange(rows), GRID_W).astype(np.float64)
    col = np.tile(np.arange(GRID_W), rows).astype(np.float64)
    axis_dim = ATT_HEAD_DIM // 2
    freqs = ROPE_THETA ** (-np.arange(0, axis_dim, 2, dtype=np.float64) / axis_dim)
    ang = np.concatenate([row[:, None] * freqs, col[:, None] * freqs], axis=-1)
    cos = np.cos(ang).astype(np.float32)
    sin = np.sin(ang).astype(np.float32)
    return (jnp.asarray(np.repeat(cos, 2, axis=-1)),
            jnp.asarray(np.stack([-sin, sin], axis=-1).reshape(seq, ATT_HEAD_DIM)))


def _pair_rows(p):
    return p.reshape(2, SSM_PAIRS, 1, 2 * SSM_STATE).transpose(1, 0, 2, 3)


def _stacked_weights(w_in, w_branch, w_out):
    w_main = jnp.concatenate([w_in[:, :, :IN_AB], w_in[:, :, IN_AB_END:]], axis=2).astype(BF16)
    w_ab = jnp.pad(w_in[:, :, IN_AB:IN_AB_END],
                   ((0, 0), (0, 0), (0, AB_PAD - (IN_AB_END - IN_AB)))).astype(BF16)
    return w_main, w_ab, w_branch.astype(BF16), w_out.astype(BF16)


def _layer_params(layer, ssm_a_re, ssm_a_im, ssm_log_step, ssm_b_re, ssm_b_im, ssm_c_re,
                  ssm_c_im, ssm_d, ssm_w_glu, ssm_b_glu, dn_conv, dn_a_log, dn_dt_bias,
                  dn_norm_g, attn_q_norm, attn_k_norm):
    ls_n = jnp.broadcast_to(ssm_log_step[layer][:, :, None], ssm_a_re[layer].shape)
    flat = lambda p: p.reshape(2, 1, SSM_GROUPS * SSM_STATE)

    def b_pairs(b):
        bt = b.transpose(0, 1, 3, 2).reshape(2, SSM_PAIRS, 2, SSM_GROUP, SSM_STATE)
        return bt.transpose(1, 0, 3, 2, 4).reshape(SSM_PAIRS, 2, SSM_GROUP, 2 * SSM_STATE)

    def c_pairs(c):
        ct = c.transpose(0, 1, 3, 2).reshape(2, SSM_PAIRS, 2 * SSM_STATE, SSM_GROUP)
        return ct.transpose(1, 0, 2, 3)

    conv = dn_conv[layer].T.reshape(DN_CONV, 3, DN_WIDTH).transpose(1, 0, 2)
    conv = jnp.pad(conv, ((0, 0), (0, 8 - DN_CONV), (0, 0)))

    def lane_vec(p):
        return jnp.pad(p.reshape(1, -1), ((0, 0), (0, AB_PAD - p.size)))

    return {
        "ssm_rows": [_pair_rows(ssm_a_re[layer]), _pair_rows(ssm_a_im[layer]), _pair_rows(ls_n)],
        "ssm_b": [b_pairs(ssm_b_re[layer]), b_pairs(ssm_b_im[layer])],
        "ssm_c": [c_pairs(ssm_c_re[layer]), c_pairs(ssm_c_im[layer])],
        "ssm_flat": [flat(ssm_a_re[layer]), flat(ssm_a_im[layer]), flat(ls_n)],
        "ssm_d": ssm_d[layer].reshape(1, SSM_WIDTH),
        "ssm_w_glu": ssm_w_glu[layer],
        "ssm_b_glu": ssm_b_glu[layer].reshape(1, SSM_WIDTH),
        "dn_conv": conv,
        "dn_alog": lane_vec(dn_a_log[layer]),
        "dn_dtb": lane_vec(dn_dt_bias[layer]),
        "dn_norm_g": dn_norm_g[layer].reshape(1, DN_HEAD_DIM),
        "att_gq": attn_q_norm[layer].reshape(1, ATT_HEAD_DIM),
        "att_gk": attn_k_norm[layer].reshape(1, ATT_HEAD_DIM),
    }


def kernel(x, mem, norm_g, w_in, ssm_a_re, ssm_a_im, ssm_log_step, ssm_b_re, ssm_b_im, ssm_c_re, ssm_c_im, ssm_d, ssm_w_glu, ssm_b_glu, dn_conv, dn_a_log, dn_dt_bias, dn_norm_g, attn_q_norm, attn_k_norm, mem_norm_g, w_mem_kv, w_branch, w_out, final_norm_g):
    bsz, seq, _ = x.shape
    depth = w_in.shape[0]
    cos2, sin2 = _rope_tables(seq)
    final_g = final_norm_g.reshape(1, D_MODEL)
    w_main, w_ab, w_br, w_o = _stacked_weights(w_in, w_branch, w_out)
    outs = []
    for b in range(bsz):
        xb = x[b]
        for layer in range(depth):
            lp = _layer_params(layer, ssm_a_re, ssm_a_im, ssm_log_step, ssm_b_re, ssm_b_im,
                               ssm_c_re, ssm_c_im, ssm_d, ssm_w_glu, ssm_b_glu, dn_conv, dn_a_log,
                               dn_dt_bias, dn_norm_g, attn_q_norm, attn_k_norm)
            h, ab = _inproj(xb, norm_g[layer].reshape(1, D_MODEL), w_main, w_ab, layer)
            y_a = _ssm_branch(h, lp)
            y_b = _dn_branch(h, ab, lp)
            qt, kr, vt = _att_prep(h, cos2, sin2, lp["att_gq"], lp["att_gk"])
            y_c = _att(qt, kr, vt, h)
            kv = _mem_kv(mem[b], mem_norm_g[layer].reshape(1, D_MODEL), w_mem_kv[layer])
            y_m = _mem_att(h, kv)
            merged = _merge((y_a, y_b, y_c, y_m), h, w_br, layer)
            xb = _outproj(merged, w_o, xb, final_g, layer, final_norm=(layer == depth - 1))
        outs.append(xb)
    return outs[0][None] if bsz == 1 else jnp.stack(outs, axis=0)
```
